```python
import math
import jax, jax.numpy as jnp
from jax import lax
import numpy as np

D_MODEL = 2048
BATCH = 2
SEQ = 16384
DEPTH = 1

CHUNK = 64
D_MIX = D_MODEL
D_CONV = D_MIX // 2
D_MLSTM = D_MIX - D_CONV
CONV_K = 3
M_HEADS = 8
M_HEAD_DIM = D_MLSTM // M_HEADS
D_IN = 3 * D_CONV + 4 * D_MLSTM + 2 * M_HEADS
PEER_HEADS = 8
N_KEYS = 128
N_EXPERTS = N_KEYS * N_KEYS
PEER_TOPK = 16
PEER_QDIM = 256
PEER_HALF = PEER_QDIM // 2
PEER_BLOCK = 128
ALPHA = (2 * DEPTH) ** 0.25
BETA = (8 * DEPTH) ** -0.25
LN_EPS = 1e-5

kernel_name = "hybrid_conv_mlstm_peer_deepnorm"


def layernorm(x, g, b):
    xf = x.astype(jnp.float32)
    mu = jnp.mean(xf, axis=-1, keepdims=True)
    xc = xf - mu
    var = jnp.mean(xc * xc, axis=-1, keepdims=True)
    y = xc * lax.rsqrt(var + LN_EPS) * g.astype(jnp.float32) + b.astype(jnp.float32)
    return y.astype(x.dtype)


def causal_dwconv(z, w, b):
    S = z.shape[1]
    zp = jnp.pad(z, ((0, 0), (CONV_K - 1, 0), (0, 0)))
    y = b
    for j in range(CONV_K):
        y = y + w[j] * zp[:, j:j + S]
    return y


def mlstm_chunkwise(q, k, v, logi, logf):
    Bn, S, H, dh = q.shape
    L = CHUNK
    NC = S // L
    f32 = jnp.float32
    def to_chunks(t):
        return t.astype(f32).reshape(Bn, NC, L, H, dh).transpose(1, 0, 3, 2, 4)
    qc = to_chunks(q)
    kc = to_chunks(k) * (dh ** -0.5)
    vc = to_chunks(v)
    li = logi.astype(f32).reshape(Bn, NC, L, H).transpose(1, 0, 3, 2)
    lf = logf.astype(f32).reshape(Bn, NC, L, H).transpose(1, 0, 3, 2)
    mask = jnp.tril(jnp.ones((L, L), dtype=bool))

    def step(carry, inp):
        C, n, m = carry
        qb, kb, vb, lib, lfb = inp
        bcum = jnp.cumsum(lfb, axis=-1)
        Dm = bcum[..., :, None] - bcum[..., None, :] + lib[..., None, :]
        Dm = jnp.where(mask, Dm, -jnp.inf)
        inter = bcum + m[..., None]
        m_t = jnp.maximum(inter, jnp.max(Dm, axis=-1))
        w_intra = jnp.exp(Dm - m_t[..., None])
        w_inter = jnp.exp(inter - m_t)
        s = jnp.einsum('bhld,bhjd->bhlj', qb, kb) * w_intra
        num = (jnp.einsum('bhlj,bhjd->bhld', s, vb)
               + w_inter[..., None] * jnp.einsum('bhed,bhld->bhle', C, qb))
        den = jnp.sum(s, axis=-1) + w_inter * jnp.einsum('bhd,bhld->bhl', n, qb)
        h = num / jnp.maximum(jnp.abs(den), jnp.exp(-m_t))[..., None]
        bL = bcum[..., -1]
        g = bL[..., None] - bcum + lib
        m_new = jnp.maximum(bL + m, jnp.max(g, axis=-1))
        wk = jnp.exp(g - m_new[..., None])
        decay = jnp.exp(bL + m - m_new)
        C_new = decay[..., None, None] * C + jnp.einsum('bhl,bhle,bhld->bhed', wk, vb, kb)
        n_new = decay[..., None] * n + jnp.einsum('bhl,bhld->bhd', wk, kb)
        return (C_new, n_new, m_new), h

    init = (jnp.zeros((Bn, H, dh, dh), f32), jnp.zeros((Bn, H, dh), f32),
            jnp.zeros((Bn, H), f32))
    _, hs = lax.scan(step, init, (qc, kc, vc, li, lf))
    return hs.transpose(1, 0, 3, 2, 4).reshape(Bn, S, H, dh).astype(q.dtype)


def peer_ffn(h, wq, keys, u, v):
    Bn, S, D = h.shape
    xb = h.reshape(-1, PEER_BLOCK, D)

    def block(xt):
        q = (xt @ wq).reshape(-1, PEER_HEADS, 2, PEER_HALF)
        s = jnp.einsum('thpc,hpnc->thpn', q, keys)
        sv, si = lax.top_k(s, PEER_TOPK)
        cand = (sv[:, :, 0, :, None] + sv[:, :, 1, None, :]).reshape(-1, PEER_HEADS, PEER_TOPK * PEER_TOPK)
        cid = (si[:, :, 0, :, None] * N_KEYS + si[:, :, 1, None, :]).reshape(-1, PEER_HEADS, PEER_TOPK * PEER_TOPK)
        top_s, top_j = lax.top_k(cand, PEER_TOPK)
        eid = jnp.take_along_axis(cid, top_j, axis=-1)
        gate = jax.nn.softmax(top_s.astype(jnp.float32), axis=-1)
        ue = jnp.take(u, eid, axis=0)
        act = jax.nn.gelu(jnp.einsum('thkd,td->thk', ue, xt).astype(jnp.float32), approximate=False)
        ve = jnp.take(v, eid, axis=0)
        return jnp.einsum('thk,thkd->td', (gate * act).astype(xt.dtype), ve)

    return lax.map(block, xb).reshape(Bn, S, D)


def setup_inputs(seed: int = 0) -> dict:
    key = jax.random.key(seed)
    ks = jax.random.split(key, 20)
    f32 = jnp.float32
    nrm = lambda k, shape: jax.random.normal(k, shape, f32)
    x = nrm(ks[0], (BATCH, SEQ, D_MODEL))
    ln_in_g = 1.0 + 0.05 * nrm(ks[1], (D_MODEL,))
    ln_in_b = 0.01 * nrm(ks[2], (D_MODEL,))
    col_scale = jnp.concatenate([
        jnp.ones((2 * D_CONV,), f32), jnp.full((D_CONV,), BETA, f32),
        jnp.ones((2 * D_MLSTM,), f32), jnp.full((D_MLSTM,), BETA, f32),
        jnp.ones((D_MLSTM + 2 * M_HEADS,), f32)])
    w_in = nrm(ks[3], (DEPTH, D_MODEL, D_IN)) * (D_MODEL ** -0.5) * col_scale
    b_i = 0.1 * nrm(ks[4], (DEPTH, M_HEADS))
    b_f = jnp.linspace(3.0, 6.0, M_HEADS, dtype=f32) + 0.1 * nrm(ks[5], (DEPTH, M_HEADS))
    b_gate = jnp.concatenate([b_i, b_f], axis=-1)
    conv_w = nrm(ks[6], (DEPTH, CONV_K, D_CONV)) * (CONV_K ** -0.5)
    conv_b = 0.01 * nrm(ks[7], (DEPTH, D_CONV))
    mh_norm_g = 1.0 + 0.05 * nrm(ks[8], (DEPTH, D_MLSTM))
    w_out = nrm(ks[9], (DEPTH, D_MIX, D_MODEL)) * (D_MIX ** -0.5) * BETA
    ln1_g = 1.0 + 0.05 * nrm(ks[10], (DEPTH, D_MODEL))
    ln1_b = 0.01 * nrm(ks[11], (DEPTH, D_MODEL))
    peer_wq = nrm(ks[12], (DEPTH, D_MODEL, PEER_HEADS * PEER_QDIM)) * (D_MODEL ** -0.5)
    peer_keys = nrm(ks[13], (DEPTH, PEER_HEADS, 2, N_KEYS, PEER_HALF)) * (PEER_HALF ** -0.5)
    peer_u = nrm(ks[14], (DEPTH, N_EXPERTS, D_MODEL)) * (D_MODEL ** -0.5) * BETA
    peer_v = nrm(ks[15], (DEPTH, N_EXPERTS, D_MODEL)) * (PEER_HEADS ** -0.5) * BETA
    ln2_g = 1.0 + 0.05 * nrm(ks[16], (DEPTH, D_MODEL))
    ln2_b = 0.01 * nrm(ks[17], (DEPTH, D_MODEL))
    return {"x": x, "ln_in_g": ln_in_g, "ln_in_b": ln_in_b, "w_in": w_in, "b_gate": b_gate,
            "conv_w": conv_w, "conv_b": conv_b, "mh_norm_g": mh_norm_g, "w_out": w_out,
            "ln1_g": ln1_g, "ln1_b": ln1_b, "peer_wq": peer_wq, "peer_keys": peer_keys,
            "peer_u": peer_u, "peer_v": peer_v, "ln2_g": ln2_g, "ln2_b": ln2_b}


def reference(x, ln_in_g, ln_in_b, w_in, b_gate, conv_w, conv_b, mh_norm_g, w_out,
              ln1_g, ln1_b, peer_wq, peer_keys, peer_u, peer_v, ln2_g, ln2_b):
    Bn, S, D = x.shape
    split_at = np.cumsum([D_CONV, D_CONV, D_CONV, D_MLSTM, D_MLSTM, D_MLSTM, D_MLSTM, M_HEADS]).tolist()
    h = layernorm(x, ln_in_g, ln_in_b)
    for l in range(DEPTH):
        proj = h @ w_in[l]
        cB, cC, ch, q, k, v, o, gi, gf = jnp.split(proj, split_at, axis=-1)
        y_conv = cB * causal_dwconv(cC * ch, conv_w[l], conv_b[l])
        gates = (jnp.concatenate([gi, gf], axis=-1) + b_gate[l]).astype(jnp.float32)
        logi = gates[..., :M_HEADS]
        logf = jax.nn.log_sigmoid(gates[..., M_HEADS:])
        hd = lambda t: t.reshape(Bn, S, M_HEADS, M_HEAD_DIM)
        hm = mlstm_chunkwise(hd(q), hd(k), hd(v), logi, logf)
        hm = layernorm(hm, mh_norm_g[l].reshape(M_HEADS, M_HEAD_DIM),
                       jnp.zeros((M_HEADS, M_HEAD_DIM), hm.dtype))
        y_m = jax.nn.sigmoid(o) * hm.reshape(Bn, S, D_MLSTM)
        mix = jnp.concatenate([y_conv, y_m], axis=-1) @ w_out[l]
        h = layernorm(ALPHA * h + mix, ln1_g[l], ln1_b[l])
        ff = peer_ffn(h, peer_wq[l], peer_keys[l], peer_u[l], peer_v[l])
        h = layernorm(ALPHA * h + ff, ln2_g[l], ln2_b[l])
    return h
```

```python
import functools
import math

import jax
import jax.numpy as jnp
from jax import lax
from jax.experimental import pallas as pl
from jax.experimental.pallas import tpu as pltpu

F32 = jnp.float32
BF16 = jnp.bfloat16

D_MODEL = 2048
CHUNK = 64
D_CONV = 1024
D_MLSTM = 1024
M_HEADS = 8
M_HEAD_DIM = 128
D_MAIN = 3 * D_CONV + 4 * D_MLSTM
PEER_HEADS = 8
N_KEYS = 128
N_EXPERTS = N_KEYS * N_KEYS
PEER_TOPK = 16
PEER_HALF = 128
DEPTH = 1
ALPHA = (2 * DEPTH) ** 0.25
LN_EPS = 1e-5
LANES = 128
VMEM_LIMIT = 56 * 1024 * 1024

_TILES = dict(proj_tm=512, scan_tl=2048, mlstm_ts=1024, mix_tm=512, retrieve_tb=128,
              scatter_tb=64, experts_tm=512, experts_ec=1024)


def _cparams(sem):
    return pltpu.CompilerParams(dimension_semantics=sem, vmem_limit_bytes=VMEM_LIMIT)


def _layernorm(x, g, b):
    mu = jnp.mean(x, axis=-1, keepdims=True)
    xc = x - mu
    var = jnp.mean(xc * xc, axis=-1, keepdims=True)
    return xc * lax.rsqrt(var + LN_EPS) * g + b


def _ln_proj_kernel(x_ref, g_ref, b_ref, w_ref, wg_ref, h_ref, proj_ref, gate_ref, hb_ref):
    @pl.when(pl.program_id(1) == 0)
    def _():
        h = _layernorm(x_ref[...], g_ref[...], b_ref[...])
        h_ref[...] = h
        hb_ref[...] = h.astype(BF16)
        gate_ref[...] = jnp.dot(h, wg_ref[...], preferred_element_type=F32,
                                precision=lax.Precision.HIGHEST)

    proj_ref[...] = jnp.dot(hb_ref[...], w_ref[...], preferred_element_type=F32).astype(BF16)


def _ln_proj(x, g, b, w_main, w_gate, tm=512, tn=1024):
    n = x.shape[0]
    return pl.pallas_call(
        _ln_proj_kernel,
        grid=(n // tm, D_MAIN // tn),
        in_specs=[
            pl.BlockSpec((tm, D_MODEL), lambda i, j: (i, 0)),
            pl.BlockSpec((1, D_MODEL), lambda i, j: (0, 0)),
            pl.BlockSpec((1, D_MODEL), lambda i, j: (0, 0)),
            pl.BlockSpec((D_MODEL, tn), lambda i, j: (0, j)),
            pl.BlockSpec((D_MODEL, LANES), lambda i, j: (0, 0)),
        ],
        out_specs=[
            pl.BlockSpec((tm, D_MODEL), lambda i, j: (i, 0)),
            pl.BlockSpec((tm, tn), lambda i, j: (i, j)),
            pl.BlockSpec((tm, LANES), lambda i, j: (i, 0)),
        ],
        out_shape=[
            jax.ShapeDtypeStruct((n, D_MODEL), F32),
            jax.ShapeDtypeStruct((n, D_MAIN), BF16),
            jax.ShapeDtypeStruct((n, LANES), F32),
        ],
        scratch_shapes=[pltpu.VMEM((tm, D_MODEL), BF16)],
        compiler_params=_cparams(("parallel", "arbitrary")),
        name="ln_proj",
    )(x, g, b, w_main, w_gate)


def _gate_scan_kernel(gt_ref, b_ref, out_ref):
    z = gt_ref[...] + b_ref[...]
    li = z[:M_HEADS]
    lf = jax.nn.log_sigmoid(z[M_HEADS:])
    pos = lax.broadcasted_iota(jnp.int32, lf.shape, 1) % CHUNK
    bcum = lf
    shift = 1
    while shift < CHUNK:
        bcum = bcum + jnp.where(pos >= shift, pltpu.roll(bcum, shift, axis=1), 0.0)
        shift *= 2
    r = li - bcum
    cm = r
    shift = 1
    while shift < CHUNK:
        cm = jnp.maximum(cm, jnp.where(pos >= shift, pltpu.roll(cm, shift, axis=1), -jnp.inf))
        shift *= 2
    out_ref[0:8, :] = bcum
    out_ref[8:16, :] = r
    out_ref[16:24, :] = cm


def _gate_scan(gt, bias, tl=2048):
    n = gt.shape[1]
    tl = min(tl, n)
    return pl.pallas_call(
        _gate_scan_kernel,
        grid=(n // tl,),
        in_specs=[pl.BlockSpec((2 * M_HEADS, tl), lambda i: (0, i)),
                  pl.BlockSpec((2 * M_HEADS, 1), lambda i: (0, 0))],
        out_specs=pl.BlockSpec((3 * M_HEADS, tl), lambda i: (0, i)),
        out_shape=jax.ShapeDtypeStruct((3 * M_HEADS, n), F32),
        compiler_params=_cparams(("parallel",)),
        name="gate_scan",
    )(gt, bias)


def _mlstm_kernel(q_ref, k_ref, v_ref, o_ref, row_ref, col_ref, ng_ref, y_ref,
                  ct_ref, n_ref, m_ref, *, ts):
    @pl.when(pl.program_id(1) == 0)
    def _():
        ct_ref[...] = jnp.zeros_like(ct_ref)
        n_ref[...] = jnp.zeros_like(n_ref)
        m_ref[...] = jnp.zeros_like(m_ref)

    L = CHUNK
    dh = M_HEAD_DIM
    scale = dh ** -0.5
    li = lax.broadcasted_iota(jnp.int32, (L, L), 0)
    lj = lax.broadcasted_iota(jnp.int32, (L, L), 1)
    causal = lj <= li

    def chunk_body(c, carry):
        t0 = pl.multiple_of(c * L, L)
        rows = pl.ds(t0, L)
        colg = col_ref[rows, :]
        rowg = row_ref[c]
        for h in range(M_HEADS):
            hs = slice(h * dh, (h + 1) * dh)
            q = q_ref[rows, hs]
            k = k_ref[rows, hs]
            v = v_ref[rows, hs]
            bc = colg[:, h:h + 1]
            rc = colg[:, M_HEADS + h:M_HEADS + h + 1]
            cm = colg[:, 2 * M_HEADS + h:2 * M_HEADS + h + 1]
            rr = rowg[h:h + 1, :]
            m = m_ref[h]
            a = jnp.maximum(m, cm)
            w_intra = jnp.exp(jnp.where(causal, rr - a, -jnp.inf))
            w_inter = jnp.exp(m - a)
            s = lax.dot_general(q, k, (((1,), (1,)), ((), ())),
                                preferred_element_type=F32) * scale * w_intra
            ct = ct_ref[h]
            nvec = n_ref[h]
            num = (jnp.dot(s.astype(BF16), v, preferred_element_type=F32)
                   + w_inter * jnp.dot(q, ct.astype(BF16), preferred_element_type=F32))
            qf = q.astype(F32)
            den = (jnp.sum(s, axis=-1, keepdims=True)
                   + w_inter * jnp.sum(qf * nvec, axis=-1, keepdims=True))
            hval = num / jnp.maximum(jnp.abs(den), jnp.exp(-(bc + a)))
            a_last = a[L - 1:L, :]
            b_last = bc[L - 1:L, :]
            wk = jnp.exp(rc - a_last)
            decay = jnp.exp(m - a_last)
            kf = k.astype(F32)
            wkv = (wk * v.astype(F32)).astype(BF16)
            upd = lax.dot_general(k, wkv, (((0,), (0,)), ((), ())),
                                  preferred_element_type=F32)
            ct_ref[h] = decay * ct + scale * upd
            n_ref[h] = decay * nvec + scale * jnp.sum(wk * kf, axis=0, keepdims=True)
            m_ref[h] = b_last + a_last
            mu = jnp.mean(hval, axis=-1, keepdims=True)
            hc = hval - mu
            var = jnp.mean(hc * hc, axis=-1, keepdims=True)
            hn = hc * lax.rsqrt(var + LN_EPS) * ng_ref[:, hs]
            og = jax.nn.sigmoid(o_ref[rows, hs].astype(F32))
            y_ref[rows, hs] = (og * hn).astype(BF16)
        return carry

    lax.fori_loop(0, ts // L, chunk_body, 0)


def _mlstm(proj, rowg, colg, norm_g, batch, seq, ts=1024):
    n = proj.shape[0]
    ts = min(ts, seq)
    nb = seq // ts
    blk = lambda col: pl.BlockSpec((ts, D_MLSTM), lambda b, i, col=col: (b * nb + i, col))
    return pl.pallas_call(
        functools.partial(_mlstm_kernel, ts=ts),
        grid=(batch, nb),
        in_specs=[
            blk(3), blk(4), blk(5), blk(6),
            pl.BlockSpec((ts // CHUNK, M_HEADS, CHUNK), lambda b, i: (b * nb + i, 0, 0)),
            pl.BlockSpec((ts, 4 * M_HEADS), lambda b, i: (b * nb + i, 0)),
            pl.BlockSpec((1, D_MLSTM), lambda b, i: (0, 0)),
        ],
        out_specs=pl.BlockSpec((ts, D_MLSTM), lambda b, i: (b * nb + i, 0)),
        out_shape=jax.ShapeDtypeStruct((n, D_MLSTM), BF16),
        scratch_shapes=[
            pltpu.VMEM((M_HEADS, M_HEAD_DIM, M_HEAD_DIM), F32),
            pltpu.VMEM((M_HEADS, 1, M_HEAD_DIM), F32),
            pltpu.VMEM((M_HEADS, 1, 1), F32),
        ],
        compiler_params=_cparams(("parallel", "arbitrary")),
        name="mlstm",
    )(proj, proj, proj, proj, rowg, colg, norm_g)


def _mix_kernel(cb_ref, cc_ref, ch_ref, pc_ref, ph_ref, ym_ref, h_ref, cw_ref, cbias_ref,
                wo_ref, g_ref, b_ref, h1_ref, h1b_ref, *, blocks_per_seq):
    tm = cb_ref.shape[0]
    z = cc_ref[...].astype(F32) * ch_ref[...].astype(F32)
    halo = pc_ref[...].astype(F32) * ph_ref[...].astype(F32)
    first = (pl.program_id(0) % blocks_per_seq) == 0
    halo = jnp.where(first, 0.0, halo)
    row = lax.broadcasted_iota(jnp.int32, z.shape, 0)
    hr = halo.shape[0]
    z1 = jnp.where(row == 0, halo[hr - 1:hr, :], pltpu.roll(z, 1, axis=0))
    z2 = pltpu.roll(z, 2, axis=0)
    z2 = jnp.where(row == 0, halo[hr - 2:hr - 1, :], z2)
    z2 = jnp.where(row == 1, halo[hr - 1:hr, :], z2)
    conv = cbias_ref[...] + cw_ref[0:1, :] * z2 + cw_ref[1:2, :] * z1 + cw_ref[2:3, :] * z
    y_conv = (cb_ref[...].astype(F32) * conv).astype(BF16)
    mix = (jnp.dot(y_conv, wo_ref[0:D_CONV, :], preferred_element_type=F32)
           + jnp.dot(ym_ref[...], wo_ref[D_CONV:, :], preferred_element_type=F32))
    h1 = _layernorm(ALPHA * h_ref[...] + mix, g_ref[...], b_ref[...])
    h1_ref[...] = h1
    h1b_ref[...] = h1.astype(BF16)


def _mix(proj, ym, h, conv_w, conv_b, w_out, g, b, seq, tm=512):
    n = proj.shape[0]
    tm = min(tm, seq)
    halo = 16
    hb = tm // halo
    cur = lambda col: pl.BlockSpec((tm, D_CONV), lambda i, col=col: (i, col))
    prev = lambda col: pl.BlockSpec((halo, D_CONV),
                                    lambda i, col=col: (jnp.maximum(i * hb - 1, 0), col))
    const = lambda shape: pl.BlockSpec(shape, lambda i: (0, 0))
    return pl.pallas_call(
        functools.partial(_mix_kernel, blocks_per_seq=seq // tm),
        grid=(n // tm,),
        in_specs=[
            cur(0), cur(1), cur(2), prev(1), prev(2),
            pl.BlockSpec((tm, D_MLSTM), lambda i: (i, 0)),
            pl.BlockSpec((tm, D_MODEL), lambda i: (i, 0)),
            const((8, D_CONV)), const((1, D_CONV)),
            const((D_MODEL, D_MODEL)), const((1, D_MODEL)), const((1, D_MODEL)),
        ],
        out_specs=[pl.BlockSpec((tm, D_MODEL), lambda i: (i, 0)),
                   pl.BlockSpec((tm, D_MODEL), lambda i: (i, 0))],
        out_shape=[jax.ShapeDtypeStruct((n, D_MODEL), F32),
                   jax.ShapeDtypeStruct((n, D_MODEL), BF16)],
        compiler_params=_cparams(("parallel",)),
        name="mix",
    )(proj, proj, proj, proj, proj, ym, h, conv_w, conv_b, w_out, g, b)


def _top16_rows(cur, idx, sv_store, si_store):
    big = jnp.int32(2 ** 30)
    for k in range(PEER_TOPK):
        mx = jnp.max(cur, axis=0, keepdims=True)
        sel = jnp.min(jnp.where(cur == mx, idx, big), axis=0, keepdims=True)
        sv_store(k, mx)
        si_store(k, sel)
        cur = jnp.where(idx == sel, -jnp.inf, cur)


def _retrieve_kernel(x_ref, wq_ref, keys_ref, eid_ref, gate_ref,
                     q_scr, sv_scr, si_scr, cand_scr, cid_scr, ts_scr, te_scr, eo_scr, go_scr):
    tb = x_ref.shape[0]
    q = jnp.dot(x_ref[...], wq_ref[...], preferred_element_type=F32).astype(BF16)
    for hp in range(2 * PEER_HEADS):
        q_scr[hp] = q[:, hp * PEER_HALF:(hp + 1) * PEER_HALF]

    key_iota = lax.broadcasted_iota(jnp.int32, (N_KEYS, tb), 0)

    def half_body(hp, carry):
        s = lax.dot_general(keys_ref[hp], q_scr[hp], (((1,), (1,)), ((), ())),
                            preferred_element_type=F32)

        def sv_store(k, row):
            sv_scr[hp, k:k + 1, :] = row

        def si_store(k, row):
            si_scr[hp, k:k + 1, :] = row

        _top16_rows(s, key_iota, sv_store, si_store)
        return carry

    lax.fori_loop(0, 2 * PEER_HEADS, half_body, 0)

    flat_iota = lax.broadcasted_iota(jnp.int32, (PEER_TOPK * PEER_TOPK, tb), 0)

    def head_body(h, carry):
        sv1 = sv_scr[2 * h]
        sv2 = sv_scr[2 * h + 1]
        si1 = si_scr[2 * h]
        si2 = si_scr[2 * h + 1]
        for a in range(PEER_TOPK):
            cand_scr[a * PEER_TOPK:(a + 1) * PEER_TOPK, :] = sv1[a:a + 1, :] + sv2
            cid_scr[a * PEER_TOPK:(a + 1) * PEER_TOPK, :] = si1[a:a + 1, :] * N_KEYS + si2

        def ts_store(k, row):
            ts_scr[k:k + 1, :] = row

        def tj_store(k, row):
            cid = cid_scr[...]
            te_scr[k:k + 1, :] = jnp.max(jnp.where(flat_iota == row, cid, -1), axis=0, keepdims=True)

        _top16_rows(cand_scr[...], flat_iota, ts_store, tj_store)
        top_s = ts_scr[...]
        e = jnp.exp(top_s - jnp.max(top_s, axis=0, keepdims=True))
        gate = e / jnp.sum(e, axis=0, keepdims=True)
        rows = pl.ds(pl.multiple_of(h * PEER_TOPK, PEER_TOPK), PEER_TOPK)
        eo_scr[rows, :] = te_scr[...]
        go_scr[rows, :] = gate
        return carry

    lax.fori_loop(0, PEER_HEADS, head_body, 0)
    eid_ref[...] = eo_scr[...].T
    gate_ref[...] = go_scr[...].T


def _retrieve(h1b, wq, keys, tb=128):
    n = h1b.shape[0]
    nj = PEER_HEADS * PEER_TOPK
    return pl.pallas_call(
        _retrieve_kernel,
        grid=(n // tb,),
        in_specs=[
            pl.BlockSpec((tb, D_MODEL), lambda i: (i, 0)),
            pl.BlockSpec((D_MODEL, 2 * PEER_HEADS * PEER_HALF), lambda i: (0, 0)),
            pl.BlockSpec((2 * PEER_HEADS, N_KEYS, PEER_HALF), lambda i: (0, 0, 0)),
        ],
        out_specs=[pl.BlockSpec((tb, nj), lambda i: (i, 0)),
                   pl.BlockSpec((tb, nj), lambda i: (i, 0))],
        out_shape=[jax.ShapeDtypeStruct((n, nj), jnp.int32),
                   jax.ShapeDtypeStruct((n, nj), F32)],
        scratch_shapes=[
            pltpu.VMEM((2 * PEER_HEADS, tb, PEER_HALF), BF16),
            pltpu.VMEM((2 * PEER_HEADS, PEER_TOPK, tb), F32),
            pltpu.VMEM((2 * PEER_HEADS, PEER_TOPK, tb), jnp.int32),
            pltpu.VMEM((PEER_TOPK * PEER_TOPK, tb), F32),
            pltpu.VMEM((PEER_TOPK * PEER_TOPK, tb), jnp.int32),
            pltpu.VMEM((PEER_TOPK, tb), F32),
            pltpu.VMEM((PEER_TOPK, tb), jnp.int32),
            pltpu.VMEM((nj, tb), jnp.int32),
            pltpu.VMEM((nj, tb), F32),
        ],
        compiler_params=_cparams(("parallel",)),
        name="retrieve",
    )(h1b, wq, keys)


_W_PITCH = N_KEYS + 8


def _scatter_w_kernel(eid_ref, gate_ref, w_ref, scr):
    tb = eid_ref.shape[0]
    nj = eid_ref.shape[1]
    sub = lax.broadcasted_iota(jnp.int32, (N_KEYS, nj), 0)

    def group_body(g8, carry):
        base = pl.multiple_of(g8 * 8, 8)
        for tl in range(8):
            e = eid_ref[pl.ds(base + tl, 1), :]
            g = gate_ref[pl.ds(base + tl, 1), :]
            p = jnp.where(sub == (e >> 7), 1.0, 0.0).astype(BF16)
            q = jnp.where(sub == (e & (N_KEYS - 1)), g, 0.0).astype(BF16)
            wt = lax.dot_general(p, q, (((1,), (1,)), ((), ())), preferred_element_type=F32)
            scr[tl * _W_PITCH:tl * _W_PITCH + N_KEYS, :] = wt
        for r in range(N_KEYS):
            w_ref[pl.ds(base, 8), r * N_KEYS:(r + 1) * N_KEYS] = scr[pl.ds(r, 8, stride=_W_PITCH), :]
        return carry

    lax.fori_loop(0, tb // 8, group_body, 0)


def _scatter_w(eid, gate, tb=64):
    n, nj = eid.shape
    return pl.pallas_call(
        _scatter_w_kernel,
        grid=(n // tb,),
        in_specs=[pl.BlockSpec((tb, nj), lambda i: (i, 0)),
                  pl.BlockSpec((tb, nj), lambda i: (i, 0))],
        out_specs=pl.BlockSpec((tb, N_EXPERTS), lambda i: (i, 0)),
        out_shape=jax.ShapeDtypeStruct((n, N_EXPERTS), F32),
        scratch_shapes=[pltpu.VMEM((8 * _W_PITCH, N_KEYS), F32)],
        compiler_params=_cparams(("parallel",)),
        name="scatter_w",
    )(eid, gate)


def _experts_kernel(xb_ref, u_ref, v_ref, w_ref, h1_ref, g_ref, b_ref, out_ref, acc_ref):
    k = pl.program_id(1)

    @pl.when(k == 0)
    def _():
        acc_ref[...] = jnp.zeros_like(acc_ref)

    a = lax.dot_general(xb_ref[...], u_ref[...], (((1,), (1,)), ((), ())),
                        preferred_element_type=F32)
    act = 0.5 * a * (1.0 + lax.erf(a * (0.5 ** 0.5)))
    wa = (w_ref[...] * act).astype(BF16)
    acc_ref[...] += jnp.dot(wa, v_ref[...], preferred_element_type=F32)

    @pl.when(k == pl.num_programs(1) - 1)
    def _():
        out_ref[...] = _layernorm(ALPHA * h1_ref[...] + acc_ref[...], g_ref[...], b_ref[...])


def _experts(h1b, u, v, w, h1, g, b, tm=512, ec=1024):
    n = h1b.shape[0]
    tm = min(tm, n)
    return pl.pallas_call(
        _experts_kernel,
        grid=(n // tm, N_EXPERTS // ec),
        in_specs=[
            pl.BlockSpec((tm, D_MODEL), lambda i, k: (i, 0)),
            pl.BlockSpec((ec, D_MODEL), lambda i, k: (k, 0)),
            pl.BlockSpec((ec, D_MODEL), lambda i, k: (k, 0)),
            pl.BlockSpec((tm, ec), lambda i, k: (i, k)),
            pl.BlockSpec((tm, D_MODEL), lambda i, k: (i, 0)),
            pl.BlockSpec((1, D_MODEL), lambda i, k: (0, 0)),
            pl.BlockSpec((1, D_MODEL), lambda i, k: (0, 0)),
        ],
        out_specs=pl.BlockSpec((tm, D_MODEL), lambda i, k: (i, 0)),
        out_shape=jax.ShapeDtypeStruct((n, D_MODEL), F32),
        scratch_shapes=[pltpu.VMEM((tm, D_MODEL), F32)],
        compiler_params=_cparams(("parallel", "arbitrary")),
        name="experts",
    )(h1b, u, v, w, h1, g, b)


def kernel(x, ln_in_g, ln_in_b, w_in, b_gate, conv_w, conv_b, mh_norm_g, w_out, ln1_g, ln1_b,
           peer_wq, peer_keys, peer_u, peer_v, ln2_g, ln2_b):
    batch, seq, d = x.shape
    n = batch * seq
    row = lambda t: t.reshape(1, -1).astype(F32)
    l = 0
    w_main = w_in[l, :, :D_MAIN].astype(BF16)
    w_gate = jnp.pad(w_in[l, :, D_MAIN:], ((0, 0), (0, LANES - 2 * M_HEADS)))

    h, proj, gates = _ln_proj(x.reshape(n, d), row(ln_in_g), row(ln_in_b), w_main, w_gate,
                              tm=_TILES["proj_tm"])

    scans = _gate_scan(gates[:, :2 * M_HEADS].T, b_gate[l].reshape(2 * M_HEADS, 1),
                       tl=_TILES["scan_tl"])
    rowg = scans[M_HEADS:2 * M_HEADS].reshape(M_HEADS, n // CHUNK, CHUNK).transpose(1, 0, 2)
    colg = jnp.pad(scans.T, ((0, 0), (0, M_HEADS)))
    ym = _mlstm(proj, rowg, colg, row(mh_norm_g[l]), batch, seq, ts=_TILES["mlstm_ts"])

    cw = jnp.pad(conv_w[l], ((0, 8 - conv_w.shape[1]), (0, 0)))
    h1, h1b = _mix(proj, ym, h, cw, row(conv_b[l]), w_out[l].astype(BF16),
                   row(ln1_g[l]), row(ln1_b[l]), seq, tm=_TILES["mix_tm"])

    keys = peer_keys[l].reshape(2 * PEER_HEADS, N_KEYS, PEER_HALF).astype(BF16)
    eid, gate = _retrieve(h1b, peer_wq[l].astype(BF16), keys, tb=_TILES["retrieve_tb"])
    w = _scatter_w(eid, gate, tb=_TILES["scatter_tb"])
    out = _experts(h1b, peer_u[l].astype(BF16), peer_v[l].astype(BF16), w, h1,
                   row(ln2_g[l]), row(ln2_b[l]), tm=_TILES["experts_tm"], ec=_TILES["experts_ec"])
    return out.reshape(batch, seq, d)
```

```python
import functools
import math

import jax
import jax.numpy as jnp
from jax import lax
from jax.experimental import pallas as pl
from jax.experimental.pallas import tpu as pltpu

F32 = jnp.float32
BF16 = jnp.bfloat16

D_MODEL = 2048
CHUNK = 64
D_CONV = 1024
D_MLSTM = 1024
M_HEADS = 8
M_HEAD_DIM = 128
D_MAIN = 3 * D_CONV + 4 * D_MLSTM
PEER_HEADS = 8
N_KEYS = 128
N_EXPERTS = N_KEYS * N_KEYS
PEER_TOPK = 16
PEER_HALF = 128
DEPTH = 1
ALPHA = (2 * DEPTH) ** 0.25
LN_EPS = 1e-5
LANES = 128
VMEM_LIMIT = 56 * 1024 * 1024

_TILES = dict(proj_tm=512, scan_tl=2048, mlstm_ts=1024, mix_tm=512, retrieve_tb=256,
              scatter_tb=128, experts_tm=1024, experts_ec=512)


def _cparams(sem):
    return pltpu.CompilerParams(dimension_semantics=sem, vmem_limit_bytes=VMEM_LIMIT)


def _layernorm(x, g, b):
    mu = jnp.mean(x, axis=-1, keepdims=True)
    xc = x - mu
    var = jnp.mean(xc * xc, axis=-1, keepdims=True)
    return xc * lax.rsqrt(var + LN_EPS) * g + b


def _ln_proj_kernel(x_ref, g_ref, b_ref, w_ref, wg_ref, h_ref, proj_ref, gate_ref, hb_ref):
    @pl.when(pl.program_id(1) == 0)
    def _():
        h = _layernorm(x_ref[...], g_ref[...], b_ref[...])
        h_ref[...] = h
        hb = h.astype(BF16)
        hb_ref[...] = hb
        gate_ref[...] = jnp.dot(hb, wg_ref[...], preferred_element_type=F32)

    proj_ref[...] = jnp.dot(hb_ref[...], w_ref[...], preferred_element_type=F32).astype(BF16)


def _ln_proj(x, g, b, w_main, w_gate, tm=512, tn=1792):
    n = x.shape[0]
    tm = min(tm, n)
    return pl.pallas_call(
        _ln_proj_kernel,
        grid=(n // tm, D_MAIN // tn),
        in_specs=[
            pl.BlockSpec((tm, D_MODEL), lambda i, j: (i, 0)),
            pl.BlockSpec((1, D_MODEL), lambda i, j: (0, 0)),
            pl.BlockSpec((1, D_MODEL), lambda i, j: (0, 0)),
            pl.BlockSpec((D_MODEL, tn), lambda i, j: (0, j)),
            pl.BlockSpec((D_MODEL, LANES), lambda i, j: (0, 0)),
        ],
        out_specs=[
            pl.BlockSpec((tm, D_MODEL), lambda i, j: (i, 0)),
            pl.BlockSpec((tm, tn), lambda i, j: (i, j)),
            pl.BlockSpec((tm, LANES), lambda i, j: (i, 0)),
        ],
        out_shape=[
            jax.ShapeDtypeStruct((n, D_MODEL), F32),
            jax.ShapeDtypeStruct((n, D_MAIN), BF16),
            jax.ShapeDtypeStruct((n, LANES), F32),
        ],
        scratch_shapes=[pltpu.VMEM((tm, D_MODEL), BF16)],
        compiler_params=_cparams(("parallel", "arbitrary")),
        name="ln_proj",
    )(x, g, b, w_main, w_gate)


def _gate_scan_kernel(gt_ref, b_ref, out_ref):
    z = gt_ref[...] + b_ref[...]
    li = z[:M_HEADS]
    lf = jax.nn.log_sigmoid(z[M_HEADS:])
    pos = lax.broadcasted_iota(jnp.int32, lf.shape, 1) % CHUNK
    bcum = lf
    shift = 1
    while shift < CHUNK:
        bcum = bcum + jnp.where(pos >= shift, pltpu.roll(bcum, shift, axis=1), 0.0)
        shift *= 2
    r = li - bcum
    cm = r
    shift = 1
    while shift < CHUNK:
        cm = jnp.maximum(cm, jnp.where(pos >= shift, pltpu.roll(cm, shift, axis=1), -jnp.inf))
        shift *= 2
    out_ref[0:8, :] = bcum
    out_ref[8:16, :] = r
    out_ref[16:24, :] = cm


def _gate_scan(gt, bias, tl=2048):
    n = gt.shape[1]
    tl = min(tl, n)
    return pl.pallas_call(
        _gate_scan_kernel,
        grid=(n // tl,),
        in_specs=[pl.BlockSpec((2 * M_HEADS, tl), lambda i: (0, i)),
                  pl.BlockSpec((2 * M_HEADS, 1), lambda i: (0, 0))],
        out_specs=pl.BlockSpec((3 * M_HEADS, tl), lambda i: (0, i)),
        out_shape=jax.ShapeDtypeStruct((3 * M_HEADS, n), F32),
        compiler_params=_cparams(("parallel",)),
        name="gate_scan",
    )(gt, bias)


def _mlstm_kernel(q_ref, k_ref, v_ref, o_ref, row_ref, col_ref, ng_ref, y_ref,
                  ct_ref, n_ref, m_ref, *, ts):
    @pl.when(pl.program_id(1) == 0)
    def _():
        ct_ref[...] = jnp.zeros_like(ct_ref)
        n_ref[...] = jnp.zeros_like(n_ref)
        m_ref[...] = jnp.zeros_like(m_ref)

    L = CHUNK
    dh = M_HEAD_DIM
    scale = dh ** -0.5
    li = lax.broadcasted_iota(jnp.int32, (L, L), 0)
    lj = lax.broadcasted_iota(jnp.int32, (L, L), 1)
    causal = lj <= li

    def chunk_body(c, carry):
        t0 = pl.multiple_of(c * L, L)
        rows = pl.ds(t0, L)
        colg = col_ref[rows, :]
        rowg = row_ref[c]
        bc8 = colg[:, 0:M_HEADS]
        rc8 = colg[:, M_HEADS:2 * M_HEADS]
        cm8 = colg[:, 2 * M_HEADS:3 * M_HEADS]
        m8 = m_ref[...]
        a8 = jnp.maximum(m8, cm8)
        w_inter8 = jnp.exp(m8 - a8)
        floor8 = jnp.exp(-(bc8 + a8))
        a_last8 = a8[L - 1:L, :]
        wk8 = jnp.exp(rc8 - a_last8)
        decay8 = jnp.exp(m8 - a_last8)
        m_ref[...] = bc8[L - 1:L, :] + a_last8
        for h in range(M_HEADS):
            hs = slice(h * dh, (h + 1) * dh)
            q = q_ref[rows, hs]
            k = k_ref[rows, hs]
            v = v_ref[rows, hs]
            rr = rowg[h:h + 1, :]
            a = a8[:, h:h + 1]
            w_inter = w_inter8[:, h:h + 1]
            w_intra = jnp.exp(jnp.where(causal, rr - a, -jnp.inf))
            s = lax.dot_general(q, k, (((1,), (1,)), ((), ())),
                                preferred_element_type=F32) * scale * w_intra
            ct = ct_ref[h]
            nvec = n_ref[h]
            num = (jnp.dot(s.astype(BF16), v, preferred_element_type=F32)
                   + w_inter * jnp.dot(q, ct.astype(BF16), preferred_element_type=F32))
            qf = q.astype(F32)
            den = (jnp.sum(s, axis=-1, keepdims=True)
                   + w_inter * jnp.sum(qf * nvec, axis=-1, keepdims=True))
            hval = num / jnp.maximum(jnp.abs(den), floor8[:, h:h + 1])
            wk = wk8[:, h:h + 1]
            decay = decay8[:, h:h + 1]
            kf = k.astype(F32)
            wkv = (wk * v.astype(F32)).astype(BF16)
            upd = lax.dot_general(k, wkv, (((0,), (0,)), ((), ())),
                                  preferred_element_type=F32)
            ct_ref[h] = decay * ct + scale * upd
            n_ref[h] = decay * nvec + scale * jnp.sum(wk * kf, axis=0, keepdims=True)
            mu = jnp.mean(hval, axis=-1, keepdims=True)
            hc = hval - mu
            var = jnp.mean(hc * hc, axis=-1, keepdims=True)
            hn = hc * lax.rsqrt(var + LN_EPS) * ng_ref[:, hs]
            og = jax.nn.sigmoid(o_ref[rows, hs].astype(F32))
            y_ref[rows, hs] = (og * hn).astype(BF16)
        return carry

    lax.fori_loop(0, ts // L, chunk_body, 0, unroll=2)


def _mlstm(proj, rowg, colg, norm_g, batch, seq, ts=1024):
    n = proj.shape[0]
    ts = min(ts, seq)
    nb = seq // ts
    blk = lambda col: pl.BlockSpec((ts, D_MLSTM), lambda b, i, col=col: (b * nb + i, col))
    return pl.pallas_call(
        functools.partial(_mlstm_kernel, ts=ts),
        grid=(batch, nb),
        in_specs=[
            blk(3), blk(4), blk(5), blk(6),
            pl.BlockSpec((ts // CHUNK, M_HEADS, CHUNK), lambda b, i: (b * nb + i, 0, 0)),
            pl.BlockSpec((ts, 4 * M_HEADS), lambda b, i: (b * nb + i, 0)),
            pl.BlockSpec((1, D_MLSTM), lambda b, i: (0, 0)),
        ],
        out_specs=pl.BlockSpec((ts, D_MLSTM), lambda b, i: (b * nb + i, 0)),
        out_shape=jax.ShapeDtypeStruct((n, D_MLSTM), BF16),
        scratch_shapes=[
            pltpu.VMEM((M_HEADS, M_HEAD_DIM, M_HEAD_DIM), F32),
            pltpu.VMEM((M_HEADS, 1, M_HEAD_DIM), F32),
            pltpu.VMEM((1, M_HEADS), F32),
        ],
        compiler_params=_cparams(("parallel", "arbitrary")),
        name="mlstm",
    )(proj, proj, proj, proj, rowg, colg, norm_g)


def _mix_kernel(cb_ref, cc_ref, ch_ref, pc_ref, ph_ref, ym_ref, h_ref, cw_ref, cbias_ref,
                wo_ref, g_ref, b_ref, h1_ref, h1b_ref, *, blocks_per_seq):
    tm = cb_ref.shape[0]
    z = cc_ref[...].astype(F32) * ch_ref[...].astype(F32)
    halo = pc_ref[...].astype(F32) * ph_ref[...].astype(F32)
    first = (pl.program_id(0) % blocks_per_seq) == 0
    halo = jnp.where(first, 0.0, halo)
    row = lax.broadcasted_iota(jnp.int32, z.shape, 0)
    hr = halo.shape[0]
    z1 = jnp.where(row == 0, halo[hr - 1:hr, :], pltpu.roll(z, 1, axis=0))
    z2 = pltpu.roll(z, 2, axis=0)
    z2 = jnp.where(row == 0, halo[hr - 2:hr - 1, :], z2)
    z2 = jnp.where(row == 1, halo[hr - 1:hr, :], z2)
    conv = cbias_ref[...] + cw_ref[0:1, :] * z2 + cw_ref[1:2, :] * z1 + cw_ref[2:3, :] * z
    y_conv = (cb_ref[...].astype(F32) * conv).astype(BF16)
    mix = (jnp.dot(y_conv, wo_ref[0:D_CONV, :], preferred_element_type=F32)
           + jnp.dot(ym_ref[...], wo_ref[D_CONV:, :], preferred_element_type=F32))
    h1 = _layernorm(ALPHA * h_ref[...] + mix, g_ref[...], b_ref[...])
    h1_ref[...] = h1
    h1b_ref[...] = h1.astype(BF16)


def _mix(proj, ym, h, conv_w, conv_b, w_out, g, b, seq, tm=512):
    n = proj.shape[0]
    tm = min(tm, seq)
    halo = 16
    hb = tm // halo
    cur = lambda col: pl.BlockSpec((tm, D_CONV), lambda i, col=col: (i, col))
    prev = lambda col: pl.BlockSpec((halo, D_CONV),
                                    lambda i, col=col: (jnp.maximum(i * hb - 1, 0), col))
    const = lambda shape: pl.BlockSpec(shape, lambda i: (0, 0))
    return pl.pallas_call(
        functools.partial(_mix_kernel, blocks_per_seq=seq // tm),
        grid=(n // tm,),
        in_specs=[
            cur(0), cur(1), cur(2), prev(1), prev(2),
            pl.BlockSpec((tm, D_MLSTM), lambda i: (i, 0)),
            pl.BlockSpec((tm, D_MODEL), lambda i: (i, 0)),
            const((8, D_CONV)), const((1, D_CONV)),
            const((D_MODEL, D_MODEL)), const((1, D_MODEL)), const((1, D_MODEL)),
        ],
        out_specs=[pl.BlockSpec((tm, D_MODEL), lambda i: (i, 0)),
                   pl.BlockSpec((tm, D_MODEL), lambda i: (i, 0))],
        out_shape=[jax.ShapeDtypeStruct((n, D_MODEL), F32),
                   jax.ShapeDtypeStruct((n, D_MODEL), BF16)],
        compiler_params=_cparams(("parallel",)),
        name="mix",
    )(proj, proj, proj, proj, proj, ym, h, conv_w, conv_b, w_out, g, b)


def _qproj_kernel(x_ref, wq_ref, q_ref):
    q_ref[...] = jnp.dot(x_ref[...], wq_ref[...], preferred_element_type=F32).astype(BF16)


def _qproj(h1b, wq, tm=1024):
    n = h1b.shape[0]
    tm = min(tm, n)
    dq = wq.shape[1]
    return pl.pallas_call(
        _qproj_kernel,
        grid=(n // tm,),
        in_specs=[pl.BlockSpec((tm, D_MODEL), lambda i: (i, 0)),
                  pl.BlockSpec((D_MODEL, dq), lambda i: (0, 0), pipeline_mode=pl.Buffered(1))],
        out_specs=pl.BlockSpec((tm, dq), lambda i: (i, 0)),
        out_shape=jax.ShapeDtypeStruct((n, dq), BF16),
        compiler_params=_cparams(("parallel",)),
        name="peer_q",
    )(h1b, wq)


def _retrieve_kernel(q_ref, kb_ref, eid_ref, gate_ref, cur_scr, sv_scr, si_scr):
    tb = q_ref.shape[0]
    q = q_ref[...]

    for p in range(2):
        s = lax.dot_general(kb_ref[p], q, (((1,), (1,)), ((), ())), preferred_element_type=F32)
        cur_scr[...] = s.reshape(N_KEYS, PEER_HEADS, tb)

        def round_body(k, taken, p=p):
            vals, ids = [], list(range(N_KEYS))
            for n in range(N_KEYS):
                c = jnp.where(taken == n, -jnp.inf, cur_scr[n])
                cur_scr[n] = c
                vals.append(c)
            while len(vals) > 1:
                right = [vals[i + 1] > vals[i] for i in range(0, len(vals), 2)]
                ids = [jnp.where(g, ids[2 * i + 1], ids[2 * i]) for i, g in enumerate(right)]
                vals = [jnp.maximum(vals[2 * i], vals[2 * i + 1]) for i in range(len(right))]
            sv_scr[p, k] = vals[0]
            si_scr[p, k] = ids[0]
            return ids[0]

        lax.fori_loop(0, PEER_TOPK, round_body, jnp.full((PEER_HEADS, tb), N_KEYS, jnp.int32))

    sv1 = sv_scr[0]
    sv2 = sv_scr[1]
    e1 = si_scr[0] * N_KEYS
    e2 = si_scr[1]
    a_iota = lax.broadcasted_iota(jnp.int32, (PEER_TOPK, PEER_HEADS, tb), 0)
    front = sv1 + sv2[0][None]
    ptr = jnp.zeros_like(a_iota)
    tops, eids = [], []
    for k in range(PEER_TOPK):
        mx = jnp.max(front, axis=0)
        awin = jnp.min(jnp.where(front == mx[None], a_iota, PEER_TOPK), axis=0)
        hit = a_iota == awin[None]
        bwin = jnp.sum(jnp.where(hit, ptr, 0), axis=0)
        eid = (jnp.sum(jnp.where(hit, e1, 0), axis=0)
               + jnp.sum(jnp.where(a_iota == bwin[None], e2, 0), axis=0))
        nxt = jnp.max(jnp.where(a_iota == (bwin + 1)[None], sv2, -jnp.inf), axis=0)
        front = jnp.where(hit, sv1 + nxt[None], front)
        ptr = jnp.where(hit, ptr + 1, ptr)
        tops.append(mx)
        eids.append(eid)
    top_s = jnp.stack(tops)
    e = jnp.exp(top_s - tops[0][None])
    gate = e / jnp.sum(e, axis=0)[None]
    nj = PEER_TOPK * PEER_HEADS
    eid_ref[...] = jnp.stack(eids).reshape(nj, tb).T
    gate_ref[...] = gate.reshape(nj, tb).T


def _retrieve(q, kb, tb=256):
    n = q.shape[0]
    tb = min(tb, n)
    nj = PEER_HEADS * PEER_TOPK
    dq = 2 * PEER_HEADS * PEER_HALF
    return pl.pallas_call(
        _retrieve_kernel,
        grid=(n // tb,),
        in_specs=[
            pl.BlockSpec((tb, dq), lambda i: (i, 0)),
            pl.BlockSpec((2, N_KEYS * PEER_HEADS, dq), lambda i: (0, 0, 0),
                         pipeline_mode=pl.Buffered(1)),
        ],
        out_specs=[pl.BlockSpec((tb, nj), lambda i: (i, 0)),
                   pl.BlockSpec((tb, nj), lambda i: (i, 0))],
        out_shape=[jax.ShapeDtypeStruct((n, nj), jnp.int32),
                   jax.ShapeDtypeStruct((n, nj), F32)],
        scratch_shapes=[
            pltpu.VMEM((N_KEYS, PEER_HEADS, tb), F32),
            pltpu.VMEM((2, PEER_TOPK, PEER_HEADS, tb), F32),
            pltpu.VMEM((2, PEER_TOPK, PEER_HEADS, tb), jnp.int32),
        ],
        compiler_params=_cparams(("parallel",)),
        name="retrieve",
    )(q, kb)


def _block_diag_keys(keys):
    eye_h = jnp.eye(PEER_HEADS, dtype=keys.dtype)
    eye_p = jnp.eye(2, dtype=keys.dtype)
    kb = jnp.einsum("hpnc,hg,pr->pnhgrc", keys, eye_h, eye_p)
    return kb.reshape(2, N_KEYS * PEER_HEADS, PEER_HEADS * 2 * PEER_HALF)


_W_PITCH = N_KEYS + 8
_W_GROUP = 16


def _scatter_w_kernel(eid_ref, gate_ref, w_ref, scr_a, scr_b):
    tb = eid_ref.shape[0]
    nj = eid_ref.shape[1]
    ngrp = tb // _W_GROUP
    sub = lax.broadcasted_iota(jnp.int32, (N_KEYS, nj), 0)

    rows_per_token = N_KEYS // _W_GROUP

    def build_token(grp, tl, scr):
        t = pl.multiple_of(grp * _W_GROUP, _W_GROUP) + tl
        e = eid_ref[pl.ds(t, 1), :]
        g = gate_ref[pl.ds(t, 1), :]
        p = jnp.where(sub == (e >> 7), 1.0, 0.0).astype(BF16)
        q = jnp.where(sub == (e & (N_KEYS - 1)), g, 0.0).astype(BF16)
        wt = lax.dot_general(p, q, (((1,), (1,)), ((), ())), preferred_element_type=F32)
        scr[tl * _W_PITCH:tl * _W_PITCH + N_KEYS, :] = wt

    def emit_rows(grp, part, scr):
        base = pl.multiple_of(grp * _W_GROUP, _W_GROUP)
        for r in range(part * rows_per_token, (part + 1) * rows_per_token):
            w_ref[pl.ds(base, _W_GROUP), r * N_KEYS:(r + 1) * N_KEYS] = (
                scr[pl.ds(r, _W_GROUP, stride=_W_PITCH), :].astype(BF16))

    def step(build_grp, build_scr, emit_grp, emit_scr):
        for tl in range(_W_GROUP):
            if build_grp is not None:
                build_token(build_grp, tl, build_scr)
            if emit_grp is not None:
                emit_rows(emit_grp, tl, emit_scr)

    step(0, scr_a, None, None)

    def pair_body(i, carry):
        step(2 * i + 1, scr_b, 2 * i, scr_a)
        step(2 * i + 2, scr_a, 2 * i + 1, scr_b)
        return carry

    lax.fori_loop(0, ngrp // 2 - 1, pair_body, 0)
    step(ngrp - 1, scr_b, ngrp - 2, scr_a)
    step(None, None, ngrp - 1, scr_b)


def _scatter_w(eid, gate, tb=128):
    n, nj = eid.shape
    tb = min(tb, n)
    assert tb % (2 * _W_GROUP) == 0
    return pl.pallas_call(
        _scatter_w_kernel,
        grid=(n // tb,),
        in_specs=[pl.BlockSpec((tb, nj), lambda i: (i, 0)),
                  pl.BlockSpec((tb, nj), lambda i: (i, 0))],
        out_specs=pl.BlockSpec((tb, N_EXPERTS), lambda i: (i, 0)),
        out_shape=jax.ShapeDtypeStruct((n, N_EXPERTS), BF16),
        scratch_shapes=[pltpu.VMEM((_W_GROUP * _W_PITCH, N_KEYS), F32)] * 2,
        compiler_params=_cparams(("parallel",)),
        name="scatter_w",
    )(eid, gate)


def _experts_kernel(xb_ref, u_ref, v_ref, w_ref, h1_ref, g_ref, b_ref, out_ref):
    k = pl.program_id(1)

    @pl.when(k == 0)
    def _():
        out_ref[...] = ALPHA * h1_ref[...]

    a = lax.dot_general(xb_ref[...], u_ref[...], (((1,), (1,)), ((), ())),
                        preferred_element_type=F32)
    act = 0.5 * a * (1.0 + lax.erf(a * (0.5 ** 0.5)))
    wa = (w_ref[...].astype(F32) * act).astype(BF16)
    out_ref[...] += jnp.dot(wa, v_ref[...], preferred_element_type=F32)

    @pl.when(k == pl.num_programs(1) - 1)
    def _():
        out_ref[...] = _layernorm(out_ref[...], g_ref[...], b_ref[...])


def _experts(h1b, u, v, w, h1, g, b, tm=1024, ec=512):
    n = h1b.shape[0]
    tm = min(tm, n)
    once = pl.Buffered(1)
    return pl.pallas_call(
        _experts_kernel,
        grid=(n // tm, N_EXPERTS // ec),
        in_specs=[
            pl.BlockSpec((tm, D_MODEL), lambda i, k: (i, 0), pipeline_mode=once),
            pl.BlockSpec((ec, D_MODEL), lambda i, k: (k, 0)),
            pl.BlockSpec((ec, D_MODEL), lambda i, k: (k, 0)),
            pl.BlockSpec((tm, ec), lambda i, k: (i, k)),
            pl.BlockSpec((tm, D_MODEL), lambda i, k: (i, 0), pipeline_mode=once),
            pl.BlockSpec((1, D_MODEL), lambda i, k: (0, 0)),
            pl.BlockSpec((1, D_MODEL), lambda i, k: (0, 0)),
        ],
        out_specs=pl.BlockSpec((tm, D_MODEL), lambda i, k: (i, 0)),
        out_shape=jax.ShapeDtypeStruct((n, D_MODEL), F32),
        compiler_params=_cparams(("parallel", "arbitrary")),
        name="experts",
    )(h1b, u, v, w, h1, g, b)


def kernel(x, ln_in_g, ln_in_b, w_in, b_gate, conv_w, conv_b, mh_norm_g, w_out, ln1_g, ln1_b,
           peer_wq, peer_keys, peer_u, peer_v, ln2_g, ln2_b):
    batch, seq, d = x.shape
    n = batch * seq
    row = lambda t: t.reshape(1, -1).astype(F32)
    l = 0
    w_main = w_in[l, :, :D_MAIN].astype(BF16)
    w_gate = jnp.pad(w_in[l, :, D_MAIN:], ((0, 0), (0, LANES - 2 * M_HEADS))).astype(BF16)

    h, proj, gates = _ln_proj(x.reshape(n, d), row(ln_in_g), row(ln_in_b), w_main, w_gate,
                              tm=_TILES["proj_tm"])

    scans = _gate_scan(gates[:, :2 * M_HEADS].T, b_gate[l].reshape(2 * M_HEADS, 1),
                       tl=_TILES["scan_tl"])
    rowg = scans[M_HEADS:2 * M_HEADS].reshape(M_HEADS, n // CHUNK, CHUNK).transpose(1, 0, 2)
    colg = jnp.pad(scans.T, ((0, 0), (0, M_HEADS)))
    ym = _mlstm(proj, rowg, colg, row(mh_norm_g[l]), batch, seq, ts=_TILES["mlstm_ts"])

    cw = jnp.pad(conv_w[l], ((0, 8 - conv_w.shape[1]), (0, 0)))
    h1, h1b = _mix(proj, ym, h, cw, row(conv_b[l]), w_out[l].astype(BF16),
                   row(ln1_g[l]), row(ln1_b[l]), seq, tm=_TILES["mix_tm"])

    kb = _block_diag_keys(peer_keys[l]).astype(BF16)
    q = _qproj(h1b, peer_wq[l].astype(BF16))
    eid, gate = _retrieve(q, kb, tb=_TILES["retrieve_tb"])
    w = _scatter_w(eid, gate, tb=_TILES["scatter_tb"])
    out = _experts(h1b, peer_u[l].astype(BF16), peer_v[l].astype(BF16), w, h1,
                   row(ln2_g[l]), row(ln2_b[l]), tm=_TILES["experts_tm"], ec=_TILES["experts_ec"])
    return out.reshape(batch, seq, d)
```

```python
import functools
import math

import jax
import jax.numpy as jnp
from jax import lax
from jax.experimental import pallas as pl
from jax.experimental.pallas import tpu as pltpu

F32 = jnp.float32
BF16 = jnp.bfloat16

D_MODEL = 2048
CHUNK = 64
D_CONV = 1024
D_MLSTM = 1024
M_HEADS = 8
M_HEAD_DIM = 128
D_MAIN = 3 * D_CONV + 4 * D_MLSTM
PEER_HEADS = 8
N_KEYS = 128
N_EXPERTS = N_KEYS * N_KEYS
PEER_TOPK = 16
PEER_HALF = 128
DEPTH = 1
ALPHA = (2 * DEPTH) ** 0.25
LN_EPS = 1e-5
LANES = 128
VMEM_LIMIT = 56 * 1024 * 1024

_TILES = dict(proj_tm=1024, scan_tl=2048, mlstm_ts=1024, mix_tm=512, retrieve_tb=256,
              scatter_tb=128, experts_tm=1024, experts_ec=1024)


def _cparams(sem):
    return pltpu.CompilerParams(dimension_semantics=sem, vmem_limit_bytes=VMEM_LIMIT)


def _layernorm(x, g, b):
    mu = jnp.mean(x, axis=-1, keepdims=True)
    xc = x - mu
    var = jnp.mean(xc * xc, axis=-1, keepdims=True)
    return xc * lax.rsqrt(var + LN_EPS) * g + b


_LN_ROWS = 256


def _ln_proj_kernel(x_ref, g_ref, b_ref, w_ref, wg_ref, proj_ref, gate_ref, hb_ref):
    @pl.when(pl.program_id(1) == 0)
    def _():
        def rows_body(r, carry):
            rows = pl.ds(pl.multiple_of(r * _LN_ROWS, _LN_ROWS), _LN_ROWS)
            hb = _layernorm(x_ref[rows, :], g_ref[...], b_ref[...]).astype(BF16)
            hb_ref[rows, :] = hb
            gate_ref[rows, :] = jnp.dot(hb, wg_ref[...], preferred_element_type=F32)
            return carry

        lax.fori_loop(0, x_ref.shape[0] // _LN_ROWS, rows_body, 0)

    proj_ref[...] = jnp.dot(hb_ref[...], w_ref[...], preferred_element_type=F32).astype(BF16)


def _ln_proj(x, g, b, w_main, w_gate, tm=1024, tn=1792):
    n = x.shape[0]
    tm = min(tm, n)
    return pl.pallas_call(
        _ln_proj_kernel,
        grid=(n // tm, D_MAIN // tn),
        in_specs=[
            pl.BlockSpec((tm, D_MODEL), lambda i, j: (i, 0)),
            pl.BlockSpec((1, D_MODEL), lambda i, j: (0, 0)),
            pl.BlockSpec((1, D_MODEL), lambda i, j: (0, 0)),
            pl.BlockSpec((D_MODEL, tn), lambda i, j: (0, j)),
            pl.BlockSpec((D_MODEL, LANES), lambda i, j: (0, 0)),
        ],
        out_specs=[
            pl.BlockSpec((tm, tn), lambda i, j: (i, j)),
            pl.BlockSpec((tm, LANES), lambda i, j: (i, 0)),
        ],
        out_shape=[
            jax.ShapeDtypeStruct((n, D_MAIN), BF16),
            jax.ShapeDtypeStruct((n, LANES), F32),
        ],
        scratch_shapes=[pltpu.VMEM((tm, D_MODEL), BF16)],
        compiler_params=_cparams(("parallel", "arbitrary")),
        name="ln_proj",
    )(x, g, b, w_main, w_gate)


def _gate_scan_kernel(gt_ref, b_ref, out_ref):
    z = gt_ref[...] + b_ref[...]
    li = z[:M_HEADS]
    lf = jax.nn.log_sigmoid(z[M_HEADS:])
    pos = lax.broadcasted_iota(jnp.int32, lf.shape, 1) % CHUNK
    bcum = lf
    shift = 1
    while shift < CHUNK:
        bcum = bcum + jnp.where(pos >= shift, pltpu.roll(bcum, shift, axis=1), 0.0)
        shift *= 2
    r = li - bcum
    cm = r
    shift = 1
    while shift < CHUNK:
        cm = jnp.maximum(cm, jnp.where(pos >= shift, pltpu.roll(cm, shift, axis=1), -jnp.inf))
        shift *= 2
    out_ref[0:8, :] = bcum
    out_ref[8:16, :] = r
    out_ref[16:24, :] = cm


def _gate_scan(gt, bias, tl=2048):
    n = gt.shape[1]
    tl = min(tl, n)
    return pl.pallas_call(
        _gate_scan_kernel,
        grid=(n // tl,),
        in_specs=[pl.BlockSpec((2 * M_HEADS, tl), lambda i: (0, i)),
                  pl.BlockSpec((2 * M_HEADS, 1), lambda i: (0, 0))],
        out_specs=pl.BlockSpec((3 * M_HEADS, tl), lambda i: (0, i)),
        out_shape=jax.ShapeDtypeStruct((3 * M_HEADS, n), F32),
        compiler_params=_cparams(("parallel",)),
        name="gate_scan",
    )(gt, bias)


def _mlstm_kernel(q_ref, k_ref, v_ref, o_ref, row_ref, col_ref, ng_ref, y_ref,
                  ct_ref, n_ref, m_ref, *, ts):
    @pl.when(pl.program_id(1) == 0)
    def _():
        ct_ref[...] = jnp.zeros_like(ct_ref)
        n_ref[...] = jnp.zeros_like(n_ref)
        m_ref[...] = jnp.zeros_like(m_ref)

    L = CHUNK
    dh = M_HEAD_DIM
    scale = dh ** -0.5
    li = lax.broadcasted_iota(jnp.int32, (L, L), 0)
    lj = lax.broadcasted_iota(jnp.int32, (L, L), 1)
    causal = lj <= li

    def chunk_body(c, carry):
        t0 = pl.multiple_of(c * L, L)
        rows = pl.ds(t0, L)
        colg = col_ref[rows, :]
        rowg = row_ref[c]
        bc8 = colg[:, 0:M_HEADS]
        rc8 = colg[:, M_HEADS:2 * M_HEADS]
        cm8 = colg[:, 2 * M_HEADS:3 * M_HEADS]
        m8 = m_ref[...]
        a8 = jnp.maximum(m8, cm8)
        w_inter8 = jnp.exp(m8 - a8)
        floor8 = jnp.exp(-(bc8 + a8))
        a_last8 = a8[L - 1:L, :]
        wk8 = jnp.exp(rc8 - a_last8)
        decay8 = jnp.exp(m8 - a_last8)
        m_ref[...] = bc8[L - 1:L, :] + a_last8
        for h in range(M_HEADS):
            hs = slice(h * dh, (h + 1) * dh)
            q = q_ref[rows, hs]
            k = k_ref[rows, hs]
            v = v_ref[rows, hs]
            rr = rowg[h:h + 1, :]
            a = a8[:, h:h + 1]
            w_inter = w_inter8[:, h:h + 1]
            w_intra = jnp.exp(jnp.where(causal, rr - a, -jnp.inf))
            s = lax.dot_general(q, k, (((1,), (1,)), ((), ())),
                                preferred_element_type=F32) * scale * w_intra
            ct = ct_ref[h]
            nvec = n_ref[h]
            num = (jnp.dot(s.astype(BF16), v, preferred_element_type=F32)
                   + w_inter * jnp.dot(q, ct.astype(BF16), preferred_element_type=F32))
            qf = q.astype(F32)
            den = (jnp.sum(s, axis=-1, keepdims=True)
                   + w_inter * jnp.sum(qf * nvec, axis=-1, keepdims=True))
            hval = num / jnp.maximum(jnp.abs(den), floor8[:, h:h + 1])
            wk = wk8[:, h:h + 1]
            decay = decay8[:, h:h + 1]
            kf = k.astype(F32)
            wkv = (wk * v.astype(F32)).astype(BF16)
            upd = lax.dot_general(k, wkv, (((0,), (0,)), ((), ())),
                                  preferred_element_type=F32)
            ct_ref[h] = decay * ct + scale * upd
            n_ref[h] = decay * nvec + scale * jnp.sum(wk * kf, axis=0, keepdims=True)
            mu = jnp.mean(hval, axis=-1, keepdims=True)
            hc = hval - mu
            var = jnp.mean(hc * hc, axis=-1, keepdims=True)
            hn = hc * lax.rsqrt(var + LN_EPS) * ng_ref[:, hs]
            og = jax.nn.sigmoid(o_ref[rows, hs].astype(F32))
            y_ref[rows, hs] = (og * hn).astype(BF16)
        return carry

    lax.fori_loop(0, ts // L, chunk_body, 0, unroll=2)


def _mlstm(proj, rowg, colg, norm_g, batch, seq, ts=1024):
    n = proj.shape[0]
    ts = min(ts, seq)
    nb = seq // ts
    blk = lambda col: pl.BlockSpec((ts, D_MLSTM), lambda b, i, col=col: (b * nb + i, col))
    return pl.pallas_call(
        functools.partial(_mlstm_kernel, ts=ts),
        grid=(batch, nb),
        in_specs=[
            blk(3), blk(4), blk(5), blk(6),
            pl.BlockSpec((ts // CHUNK, M_HEADS, CHUNK), lambda b, i: (b * nb + i, 0, 0)),
            pl.BlockSpec((ts, 4 * M_HEADS), lambda b, i: (b * nb + i, 0)),
            pl.BlockSpec((1, D_MLSTM), lambda b, i: (0, 0)),
        ],
        out_specs=pl.BlockSpec((ts, D_MLSTM), lambda b, i: (b * nb + i, 0)),
        out_shape=jax.ShapeDtypeStruct((n, D_MLSTM), BF16),
        scratch_shapes=[
            pltpu.VMEM((M_HEADS, M_HEAD_DIM, M_HEAD_DIM), F32),
            pltpu.VMEM((M_HEADS, 1, M_HEAD_DIM), F32),
            pltpu.VMEM((1, M_HEADS), F32),
        ],
        compiler_params=_cparams(("parallel", "arbitrary")),
        name="mlstm",
    )(proj, proj, proj, proj, rowg, colg, norm_g)


def _mix_kernel(cb_ref, cc_ref, ch_ref, pc_ref, ph_ref, ym_ref, x_ref, g0_ref, b0_ref, cw_ref,
                cbias_ref, wo_ref, g_ref, b_ref, h1_ref, h1b_ref, *, blocks_per_seq):
    tm = cb_ref.shape[0]
    z = cc_ref[...].astype(F32) * ch_ref[...].astype(F32)
    halo = pc_ref[...].astype(F32) * ph_ref[...].astype(F32)
    first = (pl.program_id(0) % blocks_per_seq) == 0
    halo = jnp.where(first, 0.0, halo)
    row = lax.broadcasted_iota(jnp.int32, z.shape, 0)
    hr = halo.shape[0]
    z1 = jnp.where(row == 0, halo[hr - 1:hr, :], pltpu.roll(z, 1, axis=0))
    z2 = pltpu.roll(z, 2, axis=0)
    z2 = jnp.where(row == 0, halo[hr - 2:hr - 1, :], z2)
    z2 = jnp.where(row == 1, halo[hr - 1:hr, :], z2)
    conv = cbias_ref[...] + cw_ref[0:1, :] * z2 + cw_ref[1:2, :] * z1 + cw_ref[2:3, :] * z
    y_conv = (cb_ref[...].astype(F32) * conv).astype(BF16)
    mix = (jnp.dot(y_conv, wo_ref[0:D_CONV, :], preferred_element_type=F32)
           + jnp.dot(ym_ref[...], wo_ref[D_CONV:, :], preferred_element_type=F32))
    h = _layernorm(x_ref[...], g0_ref[...], b0_ref[...])
    h1 = _layernorm(ALPHA * h + mix, g_ref[...], b_ref[...])
    h1_ref[...] = h1
    h1b_ref[...] = h1.astype(BF16)


def _mix(proj, ym, x, g0, b0, conv_w, conv_b, w_out, g, b, seq, tm=512):
    n = proj.shape[0]
    tm = min(tm, seq)
    halo = 16
    hb = tm // halo
    cur = lambda col: pl.BlockSpec((tm, D_CONV), lambda i, col=col: (i, col))
    prev = lambda col: pl.BlockSpec((halo, D_CONV),
                                    lambda i, col=col: (jnp.maximum(i * hb - 1, 0), col))
    const = lambda shape: pl.BlockSpec(shape, lambda i: (0, 0))
    return pl.pallas_call(
        functools.partial(_mix_kernel, blocks_per_seq=seq // tm),
        grid=(n // tm,),
        in_specs=[
            cur(0), cur(1), cur(2), prev(1), prev(2),
            pl.BlockSpec((tm, D_MLSTM), lambda i: (i, 0)),
            pl.BlockSpec((tm, D_MODEL), lambda i: (i, 0)),
            const((1, D_MODEL)), const((1, D_MODEL)),
            const((8, D_CONV)), const((1, D_CONV)),
            pl.BlockSpec((D_MODEL, D_MODEL), lambda i: (0, 0), pipeline_mode=pl.Buffered(1)),
            const((1, D_MODEL)), const((1, D_MODEL)),
        ],
        out_specs=[pl.BlockSpec((tm, D_MODEL), lambda i: (i, 0)),
                   pl.BlockSpec((tm, D_MODEL), lambda i: (i, 0))],
        out_shape=[jax.ShapeDtypeStruct((n, D_MODEL), F32),
                   jax.ShapeDtypeStruct((n, D_MODEL), BF16)],
        compiler_params=_cparams(("parallel",)),
        name="mix",
    )(proj, proj, proj, proj, proj, ym, x, g0, b0, conv_w, conv_b, w_out, g, b)


def _qproj_kernel(x_ref, wq_ref, q_ref):
    q_ref[...] = jnp.dot(x_ref[...], wq_ref[...], preferred_element_type=F32).astype(BF16)


def _qproj(h1b, wq, tm=1024):
    n = h1b.shape[0]
    tm = min(tm, n)
    dq = wq.shape[1]
    return pl.pallas_call(
        _qproj_kernel,
        grid=(n // tm,),
        in_specs=[pl.BlockSpec((tm, D_MODEL), lambda i: (i, 0)),
                  pl.BlockSpec((D_MODEL, dq), lambda i: (0, 0), pipeline_mode=pl.Buffered(1))],
        out_specs=pl.BlockSpec((tm, dq), lambda i: (i, 0)),
        out_shape=jax.ShapeDtypeStruct((n, dq), BF16),
        compiler_params=_cparams(("parallel",)),
        name="peer_q",
    )(h1b, wq)


def _oddeven_sort_pairs(n):
    pairs = []
    p = 1
    while p < n:
        k = p
        while k >= 1:
            for j in range(k % p, n - k, 2 * k):
                for i in range(min(k, n - j - k)):
                    if (i + j) // (2 * p) == (i + j + k) // (2 * p):
                        pairs.append((i + j, i + j + k))
            k //= 2
        p *= 2
    return pairs


_SORT16 = _oddeven_sort_pairs(PEER_TOPK)
_BITONIC16 = [(i, i + d) for d in (8, 4, 2, 1) for i in range(PEER_TOPK) if not i & d]


def _compare_exchange(v, ix, pairs):
    for i, j in pairs:
        up = v[j] > v[i]
        v[i], v[j] = jnp.maximum(v[i], v[j]), jnp.minimum(v[i], v[j])
        ix[i], ix[j] = jnp.where(up, ix[j], ix[i]), jnp.where(up, ix[i], ix[j])


def _network_top16(cur_scr, srt_v, srt_i, sv_scr, si_scr, p, lanes):
    K = PEER_TOPK
    for g in range(N_KEYS // K):
        v = [cur_scr[g * K + i, :, lanes] for i in range(K)]
        ix = [jnp.int32(g * K + i) for i in range(K)]
        _compare_exchange(v, ix, _SORT16)
        for i in range(K):
            srt_v[g * K + i] = v[i]
            srt_i[g * K + i] = ix[i]
    dropped = jnp.full(v[0].shape, -jnp.inf, F32)
    span = K
    while span < N_KEYS:
        for a in range(0, N_KEYS, 2 * span):
            b = a + span
            va = [srt_v[a + i] for i in range(K)]
            ia = [srt_i[a + i] for i in range(K)]
            v, ix = [], []
            for i in range(K):
                vb = srt_v[b + K - 1 - i]
                up = vb > va[i]
                v.append(jnp.maximum(va[i], vb))
                ix.append(jnp.where(up, srt_i[b + K - 1 - i], ia[i]))
                dropped = jnp.maximum(dropped, jnp.minimum(va[i], vb))
            _compare_exchange(v, ix, _BITONIC16)
            for i in range(K):
                srt_v[a + i] = v[i]
                srt_i[a + i] = ix[i]
        span *= 2
    distinct = v[K - 1] > dropped
    for i in range(K - 1):
        distinct = distinct & (v[i] > v[i + 1])
    for i in range(K):
        sv_scr[p, i, :, lanes] = v[i]
        si_scr[p, i, :, lanes] = ix[i]
    return jnp.where(distinct, 0, 1)


def _retrieve_kernel(q_ref, kb_ref, eid_ref, gate_ref, cur_scr, sv_scr, si_scr, srt_v, srt_i):
    tb = q_ref.shape[0]
    q = q_ref[...]

    for p in range(2):
        s = lax.dot_general(kb_ref[p], q, (((1,), (1,)), ((), ())), preferred_element_type=F32)
        cur_scr[...] = s.reshape(N_KEYS, PEER_HEADS, tb)

        def lane_body(lh, tied, p=p):
            lanes = pl.ds(pl.multiple_of(lh * LANES, LANES), LANES)
            return jnp.maximum(tied, _network_top16(cur_scr, srt_v, srt_i, sv_scr, si_scr, p, lanes))

        tied = lax.fori_loop(0, tb // LANES, lane_body, jnp.zeros((PEER_HEADS, LANES), jnp.int32))

        def round_body(k, taken, p=p):
            vals, ids = [], list(range(N_KEYS))
            for n in range(N_KEYS):
                c = jnp.where(taken == n, -jnp.inf, cur_scr[n])
                cur_scr[n] = c
                vals.append(c)
            while len(vals) > 1:
                right = [vals[i + 1] > vals[i] for i in range(0, len(vals), 2)]
                ids = [jnp.where(g, ids[2 * i + 1], ids[2 * i]) for i, g in enumerate(right)]
                vals = [jnp.maximum(vals[2 * i], vals[2 * i + 1]) for i in range(len(right))]
            sv_scr[p, k] = vals[0]
            si_scr[p, k] = ids[0]
            return ids[0]

        @pl.when(jnp.max(tied) > 0)
        def _(round_body=round_body):
            lax.fori_loop(0, PEER_TOPK, round_body, jnp.full((PEER_HEADS, tb), N_KEYS, jnp.int32))

    sv1 = sv_scr[0]
    sv2 = sv_scr[1]
    e1 = si_scr[0] * N_KEYS
    e2 = si_scr[1]
    a_iota = lax.broadcasted_iota(jnp.int32, (PEER_TOPK, PEER_HEADS, tb), 0)
    front = sv1 + sv2[0][None]
    ptr = jnp.zeros_like(a_iota)
    tops, eids = [], []
    for k in range(PEER_TOPK):
        mx = jnp.max(front, axis=0)
        awin = jnp.min(jnp.where(front == mx[None], a_iota, PEER_TOPK), axis=0)
        hit = a_iota == awin[None]
        bwin = jnp.sum(jnp.where(hit, ptr, 0), axis=0)
        eid = (jnp.sum(jnp.where(hit, e1, 0), axis=0)
               + jnp.sum(jnp.where(a_iota == bwin[None], e2, 0), axis=0))
        nxt = jnp.max(jnp.where(a_iota == (bwin + 1)[None], sv2, -jnp.inf), axis=0)
        front = jnp.where(hit, sv1 + nxt[None], front)
        ptr = jnp.where(hit, ptr + 1, ptr)
        tops.append(mx)
        eids.append(eid)
    top_s = jnp.stack(tops)
    e = jnp.exp(top_s - tops[0][None])
    gate = e / jnp.sum(e, axis=0)[None]
    nj = PEER_TOPK * PEER_HEADS
    eid_ref[...] = jnp.stack(eids).reshape(nj, tb).T
    gate_ref[...] = gate.reshape(nj, tb).T


def _retrieve(q, kb, tb=256):
    n = q.shape[0]
    tb = min(tb, n)
    nj = PEER_HEADS * PEER_TOPK
    dq = 2 * PEER_HEADS * PEER_HALF
    return pl.pallas_call(
        _retrieve_kernel,
        grid=(n // tb,),
        in_specs=[
            pl.BlockSpec((tb, dq), lambda i: (i, 0)),
            pl.BlockSpec((2, N_KEYS * PEER_HEADS, dq), lambda i: (0, 0, 0),
                         pipeline_mode=pl.Buffered(1)),
        ],
        out_specs=[pl.BlockSpec((tb, nj), lambda i: (i, 0)),
                   pl.BlockSpec((tb, nj), lambda i: (i, 0))],
        out_shape=[jax.ShapeDtypeStruct((n, nj), jnp.int32),
                   jax.ShapeDtypeStruct((n, nj), F32)],
        scratch_shapes=[
            pltpu.VMEM((N_KEYS, PEER_HEADS, tb), F32),
            pltpu.VMEM((2, PEER_TOPK, PEER_HEADS, tb), F32),
            pltpu.VMEM((2, PEER_TOPK, PEER_HEADS, tb), jnp.int32),
            pltpu.VMEM((N_KEYS, PEER_HEADS, LANES), F32),
            pltpu.VMEM((N_KEYS, PEER_HEADS, LANES), jnp.int32),
        ],
        compiler_params=_cparams(("parallel",)),
        name="retrieve",
    )(q, kb)


def _block_diag_keys(keys):
    eye_h = jnp.eye(PEER_HEADS, dtype=keys.dtype)
    eye_p = jnp.eye(2, dtype=keys.dtype)
    kb = jnp.einsum("hpnc,hg,pr->pnhgrc", keys, eye_h, eye_p)
    return kb.reshape(2, N_KEYS * PEER_HEADS, PEER_HEADS * 2 * PEER_HALF)


_W_PITCH = N_KEYS + 8
_W_GROUP = 16


def _scatter_w_kernel(eid_ref, gate_ref, w_ref, scr_a, scr_b):
    tb = eid_ref.shape[0]
    nj = eid_ref.shape[1]
    ngrp = tb // _W_GROUP
    sub = lax.broadcasted_iota(jnp.int32, (N_KEYS, nj), 0)

    rows_per_token = N_KEYS // _W_GROUP

    def build_token(grp, tl, scr):
        t = pl.multiple_of(grp * _W_GROUP, _W_GROUP) + tl
        e = eid_ref[pl.ds(t, 1), :]
        g = gate_ref[pl.ds(t, 1), :]
        p = jnp.where(sub == (e >> 7), 1.0, 0.0).astype(BF16)
        q = jnp.where(sub == (e & (N_KEYS - 1)), g, 0.0).astype(BF16)
        wt = lax.dot_general(p, q, (((1,), (1,)), ((), ())), preferred_element_type=F32)
        scr[tl * _W_PITCH:tl * _W_PITCH + N_KEYS, :] = wt

    def emit_rows(grp, part, scr):
        base = pl.multiple_of(grp * _W_GROUP, _W_GROUP)
        for r in range(part * rows_per_token, (part + 1) * rows_per_token):
            w_ref[pl.ds(base, _W_GROUP), r * N_KEYS:(r + 1) * N_KEYS] = (
                scr[pl.ds(r, _W_GROUP, stride=_W_PITCH), :].astype(BF16))

    def step(build_grp, build_scr, emit_grp, emit_scr):
        for tl in range(_W_GROUP):
            if build_grp is not None:
                build_token(build_grp, tl, build_scr)
            if emit_grp is not None:
                emit_rows(emit_grp, tl, emit_scr)

    step(0, scr_a, None, None)

    def pair_body(i, carry):
        step(2 * i + 1, scr_b, 2 * i, scr_a)
        step(2 * i + 2, scr_a, 2 * i + 1, scr_b)
        return carry

    lax.fori_loop(0, ngrp // 2 - 1, pair_body, 0)
    step(ngrp - 1, scr_b, ngrp - 2, scr_a)
    step(None, None, ngrp - 1, scr_b)


def _scatter_w(eid, gate, tb=128):
    n, nj = eid.shape
    tb = min(tb, n)
    assert tb % (2 * _W_GROUP) == 0
    return pl.pallas_call(
        _scatter_w_kernel,
        grid=(n // tb,),
        in_specs=[pl.BlockSpec((tb, nj), lambda i: (i, 0)),
                  pl.BlockSpec((tb, nj), lambda i: (i, 0))],
        out_specs=pl.BlockSpec((tb, N_EXPERTS), lambda i: (i, 0)),
        out_shape=jax.ShapeDtypeStruct((n, N_EXPERTS), BF16),
        scratch_shapes=[pltpu.VMEM((_W_GROUP * _W_PITCH, N_KEYS), F32)] * 2,
        compiler_params=_cparams(("parallel",)),
        name="scatter_w",
    )(eid, gate)


def _experts_kernel(xb_ref, u_ref, v_ref, w_ref, h1_ref, g_ref, b_ref, out_ref):
    k = pl.program_id(1)

    @pl.when(k == 0)
    def _():
        out_ref[...] = ALPHA * h1_ref[...]

    a = lax.dot_general(xb_ref[...], u_ref[...], (((1,), (1,)), ((), ())),
                        preferred_element_type=F32)
    act = 0.5 * a * (1.0 + lax.erf(a * (0.5 ** 0.5)))
    wa = (w_ref[...].astype(F32) * act).astype(BF16)
    out_ref[...] += jnp.dot(wa, v_ref[...], preferred_element_type=F32)

    @pl.when(k == pl.num_programs(1) - 1)
    def _():
        out_ref[...] = _layernorm(out_ref[...], g_ref[...], b_ref[...])


def _experts(h1b, u, v, w, h1, g, b, tm=1024, ec=1024):
    n = h1b.shape[0]
    tm = min(tm, n)
    once = pl.Buffered(1)
    return pl.pallas_call(
        _experts_kernel,
        grid=(n // tm, N_EXPERTS // ec),
        in_specs=[
            pl.BlockSpec((tm, D_MODEL), lambda i, k: (i, 0), pipeline_mode=once),
            pl.BlockSpec((ec, D_MODEL), lambda i, k: (k, 0)),
            pl.BlockSpec((ec, D_MODEL), lambda i, k: (k, 0)),
            pl.BlockSpec((tm, ec), lambda i, k: (i, k)),
            pl.BlockSpec((tm, D_MODEL), lambda i, k: (i, 0), pipeline_mode=once),
            pl.BlockSpec((1, D_MODEL), lambda i, k: (0, 0)),
            pl.BlockSpec((1, D_MODEL), lambda i, k: (0, 0)),
        ],
        out_specs=pl.BlockSpec((tm, D_MODEL), lambda i, k: (i, 0), pipeline_mode=once),
        out_shape=jax.ShapeDtypeStruct((n, D_MODEL), F32),
        compiler_params=_cparams(("parallel", "arbitrary")),
        name="experts",
    )(h1b, u, v, w, h1, g, b)


def kernel(x, ln_in_g, ln_in_b, w_in, b_gate, conv_w, conv_b, mh_norm_g, w_out, ln1_g, ln1_b,
           peer_wq, peer_keys, peer_u, peer_v, ln2_g, ln2_b):
    batch, seq, d = x.shape
    n = batch * seq
    row = lambda t: t.reshape(1, -1).astype(F32)
    l = 0
    w_main = w_in[l, :, :D_MAIN].astype(BF16)
    w_gate = jnp.pad(w_in[l, :, D_MAIN:], ((0, 0), (0, LANES - 2 * M_HEADS))).astype(BF16)

    x2 = x.reshape(n, d)
    proj, gates = _ln_proj(x2, row(ln_in_g), row(ln_in_b), w_main, w_gate, tm=_TILES["proj_tm"])

    scans = _gate_scan(gates[:, :2 * M_HEADS].T, b_gate[l].reshape(2 * M_HEADS, 1),
                       tl=_TILES["scan_tl"])
    rowg = scans[M_HEADS:2 * M_HEADS].reshape(M_HEADS, n // CHUNK, CHUNK).transpose(1, 0, 2)
    colg = jnp.pad(scans.T, ((0, 0), (0, M_HEADS)))
    ym = _mlstm(proj, rowg, colg, row(mh_norm_g[l]), batch, seq, ts=_TILES["mlstm_ts"])

    cw = jnp.pad(conv_w[l], ((0, 8 - conv_w.shape[1]), (0, 0)))
    h1, h1b = _mix(proj, ym, x2, row(ln_in_g), row(ln_in_b), cw, row(conv_b[l]), w_out[l].astype(BF16),
                   row(ln1_g[l]), row(ln1_b[l]), seq, tm=_TILES["mix_tm"])

    kb = _block_diag_keys(peer_keys[l]).astype(BF16)
    q = _qproj(h1b, peer_wq[l].astype(BF16))
    eid, gate = _retrieve(q, kb, tb=_TILES["retrieve_tb"])
    w = _scatter_w(eid, gate, tb=_TILES["scatter_tb"])
    out = _experts(h1b, peer_u[l].astype(BF16), peer_v[l].astype(BF16), w, h1,
                   row(ln2_g[l]), row(ln2_b[l]), tm=_TILES["experts_tm"], ec=_TILES["experts_ec"])
    return out.reshape(batch, seq, d)
```

```python
import functools
import math

import jax
import jax.numpy as jnp
from jax import lax
from jax.experimental import pallas as pl
from jax.experimental.pallas import tpu as pltpu

F32 = jnp.float32
BF16 = jnp.bfloat16

D_MODEL = 2048
CHUNK = 64
D_CONV = 1024
D_MLSTM = 1024
M_HEADS = 8
M_HEAD_DIM = 128
D_MAIN = 3 * D_CONV + 4 * D_MLSTM
PEER_HEADS = 8
N_KEYS = 128
N_EXPERTS = N_KEYS * N_KEYS
PEER_TOPK = 16
PEER_HALF = 128
DEPTH = 1
ALPHA = (2 * DEPTH) ** 0.25
LN_EPS = 1e-5
LANES = 128
VMEM_LIMIT = 56 * 1024 * 1024

_TILES = dict(proj_tm=1024, scan_tl=2048, mlstm_ts=1024, mix_tm=512, retrieve_tb=512,
              scatter_tb=128, experts_tm=1024, experts_ec=1024)


def _cparams(sem):
    return pltpu.CompilerParams(dimension_semantics=sem, vmem_limit_bytes=VMEM_LIMIT)


def _layernorm(x, g, b):
    mu = jnp.mean(x, axis=-1, keepdims=True)
    xc = x - mu
    var = jnp.mean(xc * xc, axis=-1, keepdims=True)
    return xc * lax.rsqrt(var + LN_EPS) * g + b


_LN_ROWS = 256


def _ln_proj_kernel(x_ref, g_ref, b_ref, w_ref, wg_ref, proj_ref, gate_ref, hb_ref):
    @pl.when(pl.program_id(1) == 0)
    def _():
        def rows_body(r, carry):
            rows = pl.ds(pl.multiple_of(r * _LN_ROWS, _LN_ROWS), _LN_ROWS)
            hb = _layernorm(x_ref[rows, :], g_ref[...], b_ref[...]).astype(BF16)
            hb_ref[rows, :] = hb
            gate_ref[rows, :] = jnp.dot(hb, wg_ref[...], preferred_element_type=F32)
            return carry

        lax.fori_loop(0, x_ref.shape[0] // _LN_ROWS, rows_body, 0)

    proj_ref[...] = jnp.dot(hb_ref[...], w_ref[...], preferred_element_type=F32).astype(BF16)


def _ln_proj(x, g, b, w_main, w_gate, tm=1024, tn=1792):
    n = x.shape[0]
    tm = min(tm, n)
    return pl.pallas_call(
        _ln_proj_kernel,
        grid=(n // tm, D_MAIN // tn),
        in_specs=[
            pl.BlockSpec((tm, D_MODEL), lambda i, j: (i, 0)),
            pl.BlockSpec((1, D_MODEL), lambda i, j: (0, 0)),
            pl.BlockSpec((1, D_MODEL), lambda i, j: (0, 0)),
            pl.BlockSpec((D_MODEL, tn), lambda i, j: (0, j)),
            pl.BlockSpec((D_MODEL, LANES), lambda i, j: (0, 0)),
        ],
        out_specs=[
            pl.BlockSpec((tm, tn), lambda i, j: (i, j)),
            pl.BlockSpec((tm, LANES), lambda i, j: (i, 0)),
        ],
        out_shape=[
            jax.ShapeDtypeStruct((n, D_MAIN), BF16),
            jax.ShapeDtypeStruct((n, LANES), F32),
        ],
        scratch_shapes=[pltpu.VMEM((tm, D_MODEL), BF16)],
        compiler_params=_cparams(("parallel", "arbitrary")),
        name="ln_proj",
    )(x, g, b, w_main, w_gate)


def _gate_scan_kernel(gt_ref, b_ref, out_ref):
    z = gt_ref[...] + b_ref[...]
    li = z[:M_HEADS]
    lf = jax.nn.log_sigmoid(z[M_HEADS:])
    pos = lax.broadcasted_iota(jnp.int32, lf.shape, 1) % CHUNK
    bcum = lf
    shift = 1
    while shift < CHUNK:
        bcum = bcum + jnp.where(pos >= shift, pltpu.roll(bcum, shift, axis=1), 0.0)
        shift *= 2
    r = li - bcum
    cm = r
    shift = 1
    while shift < CHUNK:
        cm = jnp.maximum(cm, jnp.where(pos >= shift, pltpu.roll(cm, shift, axis=1), -jnp.inf))
        shift *= 2
    out_ref[0:8, :] = bcum
    out_ref[8:16, :] = r
    out_ref[16:24, :] = cm


def _gate_scan(gt, bias, tl=2048):
    n = gt.shape[1]
    tl = min(tl, n)
    return pl.pallas_call(
        _gate_scan_kernel,
        grid=(n // tl,),
        in_specs=[pl.BlockSpec((2 * M_HEADS, tl), lambda i: (0, i)),
                  pl.BlockSpec((2 * M_HEADS, 1), lambda i: (0, 0))],
        out_specs=pl.BlockSpec((3 * M_HEADS, tl), lambda i: (0, i)),
        out_shape=jax.ShapeDtypeStruct((3 * M_HEADS, n), F32),
        compiler_params=_cparams(("parallel",)),
        name="gate_scan",
    )(gt, bias)


def _mlstm_kernel(q_ref, k_ref, v_ref, o_ref, row_ref, col_ref, ng_ref, y_ref,
                  ct_ref, n_ref, m_ref, *, ts):
    @pl.when(pl.program_id(1) == 0)
    def _():
        ct_ref[...] = jnp.zeros_like(ct_ref)
        n_ref[...] = jnp.zeros_like(n_ref)
        m_ref[...] = jnp.zeros_like(m_ref)

    L = CHUNK
    dh = M_HEAD_DIM
    scale = dh ** -0.5
    li = lax.broadcasted_iota(jnp.int32, (L, L), 0)
    lj = lax.broadcasted_iota(jnp.int32, (L, L), 1)
    causal = lj <= li

    def chunk_body(c, carry):
        t0 = pl.multiple_of(c * L, L)
        rows = pl.ds(t0, L)
        colg = col_ref[rows, :]
        rowg = row_ref[c]
        bc8 = colg[:, 0:M_HEADS]
        rc8 = colg[:, M_HEADS:2 * M_HEADS]
        cm8 = colg[:, 2 * M_HEADS:3 * M_HEADS]
        m8 = m_ref[...]
        a8 = jnp.maximum(m8, cm8)
        w_inter8 = jnp.exp(m8 - a8)
        floor8 = jnp.exp(-(bc8 + a8))
        a_last8 = a8[L - 1:L, :]
        wk8 = jnp.exp(rc8 - a_last8)
        decay8 = jnp.exp(m8 - a_last8)
        m_ref[...] = bc8[L - 1:L, :] + a_last8
        heads = range(M_HEADS)
        per_head = lambda ref: jnp.stack([ref[rows, h * dh:(h + 1) * dh] for h in heads])
        column = lambda m: jnp.stack([m[:, h:h + 1] for h in heads])
        bmm = lambda x, y, cx, cy: lax.dot_general(
            x, y, (((cx,), (cy,)), ((0,), (0,))), preferred_element_type=F32)
        q = per_head(q_ref)
        k = per_head(k_ref)
        v = per_head(v_ref)
        a = column(a8)
        w_inter = column(w_inter8)
        wk = column(wk8)
        decay = jnp.stack([decay8[:, h:h + 1] for h in heads])
        w_intra = jnp.exp(jnp.where(causal[None], rowg[:, None, :] - a, -jnp.inf))
        s = bmm(q, k, 2, 2) * scale * w_intra
        ct = ct_ref[...]
        nvec = n_ref[...]
        num = bmm(s.astype(BF16), v, 2, 1) + w_inter * bmm(q, ct.astype(BF16), 2, 1)
        den = (jnp.sum(s, axis=-1, keepdims=True)
               + w_inter * jnp.sum(q.astype(F32) * nvec, axis=-1, keepdims=True))
        hval = num / jnp.maximum(jnp.abs(den), column(floor8))
        wkv = (wk * v.astype(F32)).astype(BF16)
        ct_ref[...] = decay * ct + scale * bmm(k, wkv, 1, 1)
        n_ref[...] = decay * nvec + scale * jnp.sum(wk * k.astype(F32), axis=1, keepdims=True)
        mu = jnp.mean(hval, axis=-1, keepdims=True)
        hc = hval - mu
        var = jnp.mean(hc * hc, axis=-1, keepdims=True)
        hn = hc * lax.rsqrt(var + LN_EPS)
        for h in heads:
            hs = slice(h * dh, (h + 1) * dh)
            og = jax.nn.sigmoid(o_ref[rows, hs].astype(F32))
            y_ref[rows, hs] = (og * hn[h] * ng_ref[:, hs]).astype(BF16)
        return carry

    lax.fori_loop(0, ts // L, chunk_body, 0)


def _mlstm(proj, rowg, colg, norm_g, batch, seq, ts=1024):
    n = proj.shape[0]
    ts = min(ts, seq)
    nb = seq // ts
    blk = lambda col: pl.BlockSpec((ts, D_MLSTM), lambda b, i, col=col: (b * nb + i, col))
    return pl.pallas_call(
        functools.partial(_mlstm_kernel, ts=ts),
        grid=(batch, nb),
        in_specs=[
            blk(3), blk(4), blk(5), blk(6),
            pl.BlockSpec((ts // CHUNK, M_HEADS, CHUNK), lambda b, i: (b * nb + i, 0, 0)),
            pl.BlockSpec((ts, 4 * M_HEADS), lambda b, i: (b * nb + i, 0)),
            pl.BlockSpec((1, D_MLSTM), lambda b, i: (0, 0)),
        ],
        out_specs=pl.BlockSpec((ts, D_MLSTM), lambda b, i: (b * nb + i, 0)),
        out_shape=jax.ShapeDtypeStruct((n, D_MLSTM), BF16),
        scratch_shapes=[
            pltpu.VMEM((M_HEADS, M_HEAD_DIM, M_HEAD_DIM), F32),
            pltpu.VMEM((M_HEADS, 1, M_HEAD_DIM), F32),
            pltpu.VMEM((1, M_HEADS), F32),
        ],
        compiler_params=_cparams(("parallel", "arbitrary")),
        name="mlstm",
    )(proj, proj, proj, proj, rowg, colg, norm_g)


def _mix_kernel(cb_ref, cc_ref, ch_ref, pc_ref, ph_ref, ym_ref, x_ref, g0_ref, b0_ref, cw_ref,
                cbias_ref, wo_ref, g_ref, b_ref, h1_ref, h1b_ref, *, blocks_per_seq):
    tm = cb_ref.shape[0]
    z = cc_ref[...].astype(F32) * ch_ref[...].astype(F32)
    halo = pc_ref[...].astype(F32) * ph_ref[...].astype(F32)
    first = (pl.program_id(0) % blocks_per_seq) == 0
    halo = jnp.where(first, 0.0, halo)
    row = lax.broadcasted_iota(jnp.int32, z.shape, 0)
    hr = halo.shape[0]
    z1 = jnp.where(row == 0, halo[hr - 1:hr, :], pltpu.roll(z, 1, axis=0))
    z2 = pltpu.roll(z, 2, axis=0)
    z2 = jnp.where(row == 0, halo[hr - 2:hr - 1, :], z2)
    z2 = jnp.where(row == 1, halo[hr - 1:hr, :], z2)
    conv = cbias_ref[...] + cw_ref[0:1, :] * z2 + cw_ref[1:2, :] * z1 + cw_ref[2:3, :] * z
    y_conv = (cb_ref[...].astype(F32) * conv).astype(BF16)
    mix = (jnp.dot(y_conv, wo_ref[0:D_CONV, :], preferred_element_type=F32)
           + jnp.dot(ym_ref[...], wo_ref[D_CONV:, :], preferred_element_type=F32))
    h = _layernorm(x_ref[...], g0_ref[...], b0_ref[...])
    h1 = _layernorm(ALPHA * h + mix, g_ref[...], b_ref[...])
    h1_ref[...] = h1
    h1b_ref[...] = h1.astype(BF16)


def _mix(proj, ym, x, g0, b0, conv_w, conv_b, w_out, g, b, seq, tm=512):
    n = proj.shape[0]
    tm = min(tm, seq)
    halo = 16
    hb = tm // halo
    cur = lambda col: pl.BlockSpec((tm, D_CONV), lambda i, col=col: (i, col))
    prev = lambda col: pl.BlockSpec((halo, D_CONV),
                                    lambda i, col=col: (jnp.maximum(i * hb - 1, 0), col))
    const = lambda shape: pl.BlockSpec(shape, lambda i: (0, 0))
    return pl.pallas_call(
        functools.partial(_mix_kernel, blocks_per_seq=seq // tm),
        grid=(n // tm,),
        in_specs=[
            cur(0), cur(1), cur(2), prev(1), prev(2),
            pl.BlockSpec((tm, D_MLSTM), lambda i: (i, 0)),
            pl.BlockSpec((tm, D_MODEL), lambda i: (i, 0)),
            const((1, D_MODEL)), const((1, D_MODEL)),
            const((8, D_CONV)), const((1, D_CONV)),
            pl.BlockSpec((D_MODEL, D_MODEL), lambda i: (0, 0), pipeline_mode=pl.Buffered(1)),
            const((1, D_MODEL)), const((1, D_MODEL)),
        ],
        out_specs=[pl.BlockSpec((tm, D_MODEL), lambda i: (i, 0)),
                   pl.BlockSpec((tm, D_MODEL), lambda i: (i, 0))],
        out_shape=[jax.ShapeDtypeStruct((n, D_MODEL), F32),
                   jax.ShapeDtypeStruct((n, D_MODEL), BF16)],
        compiler_params=_cparams(("parallel",)),
        name="mix",
    )(proj, proj, proj, proj, proj, ym, x, g0, b0, conv_w, conv_b, w_out, g, b)


def _qproj_kernel(x_ref, wqt_ref, qt_ref):
    qt_ref[...] = lax.dot_general(wqt_ref[...], x_ref[...], (((1,), (1,)), ((), ())),
                                  preferred_element_type=F32).astype(BF16)


def _qproj(h1b, wqt, tm=1024):
    n = h1b.shape[0]
    tm = min(tm, n)
    dq = wqt.shape[0]
    return pl.pallas_call(
        _qproj_kernel,
        grid=(n // tm,),
        in_specs=[pl.BlockSpec((tm, D_MODEL), lambda i: (i, 0)),
                  pl.BlockSpec((dq, D_MODEL), lambda i: (0, 0), pipeline_mode=pl.Buffered(1))],
        out_specs=pl.BlockSpec((dq, tm), lambda i: (0, i)),
        out_shape=jax.ShapeDtypeStruct((dq, n), BF16),
        compiler_params=_cparams(("parallel",)),
        name="peer_q",
    )(h1b, wqt)


def _oddeven_sort_pairs(n):
    pairs = []
    p = 1
    while p < n:
        k = p
        while k >= 1:
            for j in range(k % p, n - k, 2 * k):
                for i in range(min(k, n - j - k)):
                    if (i + j) // (2 * p) == (i + j + k) // (2 * p):
                        pairs.append((i + j, i + j + k))
            k //= 2
        p *= 2
    return pairs


_SORT16 = _oddeven_sort_pairs(PEER_TOPK)
_BITONIC16 = [(i, i + d) for d in (8, 4, 2, 1) for i in range(PEER_TOPK) if not i & d]


def _compare_exchange(v, ix, pairs):
    for i, j in pairs:
        up = v[j] > v[i]
        v[i], v[j] = jnp.maximum(v[i], v[j]), jnp.minimum(v[i], v[j])
        ix[i], ix[j] = jnp.where(up, ix[j], ix[i]), jnp.where(up, ix[i], ix[j])


def _network_top16(cur_scr, srt_v, srt_i, sv_scr, si_scr, p, lanes):
    K = PEER_TOPK
    for g in range(N_KEYS // K):
        v = [cur_scr[g * K + i, :, lanes] for i in range(K)]
        ix = [jnp.int32(g * K + i) for i in range(K)]
        _compare_exchange(v, ix, _SORT16)
        for i in range(K):
            srt_v[g * K + i] = v[i]
            srt_i[g * K + i] = ix[i]
    dropped = jnp.full(v[0].shape, -jnp.inf, F32)
    span = K
    while span < N_KEYS:
        for a in range(0, N_KEYS, 2 * span):
            b = a + span
            va = [srt_v[a + i] for i in range(K)]
            ia = [srt_i[a + i] for i in range(K)]
            v, ix = [], []
            for i in range(K):
                vb = srt_v[b + K - 1 - i]
                up = vb > va[i]
                v.append(jnp.maximum(va[i], vb))
                ix.append(jnp.where(up, srt_i[b + K - 1 - i], ia[i]))
                dropped = jnp.maximum(dropped, jnp.minimum(va[i], vb))
            _compare_exchange(v, ix, _BITONIC16)
            for i in range(K):
                srt_v[a + i] = v[i]
                srt_i[a + i] = ix[i]
        span *= 2
    distinct = v[K - 1] > dropped
    for i in range(K - 1):
        distinct = distinct & (v[i] > v[i + 1])
    for i in range(K):
        sv_scr[p, i, :, lanes] = v[i]
        si_scr[p, i, :, lanes] = ix[i]
    return jnp.where(distinct, 0, 1)


def _retrieve_kernel(q_ref, kb_ref, eid_ref, gate_ref, cur_scr, sv_scr, si_scr, srt_v, srt_i):
    tb = q_ref.shape[1]
    qt = q_ref[...]

    for p in range(2):
        s = jnp.dot(kb_ref[p], qt, preferred_element_type=F32)
        cur_scr[...] = s.reshape(N_KEYS, PEER_HEADS, tb)

        def lane_body(lh, tied, p=p):
            lanes = pl.ds(pl.multiple_of(lh * LANES, LANES), LANES)
            return jnp.maximum(tied, _network_top16(cur_scr, srt_v, srt_i, sv_scr, si_scr, p, lanes))

        tied = lax.fori_loop(0, tb // LANES, lane_body, jnp.zeros((PEER_HEADS, LANES), jnp.int32))

        def round_body(k, taken, p=p):
            vals, ids = [], list(range(N_KEYS))
            for n in range(N_KEYS):
                c = jnp.where(taken == n, -jnp.inf, cur_scr[n])
                cur_scr[n] = c
                vals.append(c)
            while len(vals) > 1:
                right = [vals[i + 1] > vals[i] for i in range(0, len(vals), 2)]
                ids = [jnp.where(g, ids[2 * i + 1], ids[2 * i]) for i, g in enumerate(right)]
                vals = [jnp.maximum(vals[2 * i], vals[2 * i + 1]) for i in range(len(right))]
            sv_scr[p, k] = vals[0]
            si_scr[p, k] = ids[0]
            return ids[0]

        @pl.when(jnp.max(tied) > 0)
        def _(round_body=round_body):
            lax.fori_loop(0, PEER_TOPK, round_body, jnp.full((PEER_HEADS, tb), N_KEYS, jnp.int32))

    sv1 = sv_scr[0]
    sv2 = sv_scr[1]
    e1 = si_scr[0] * N_KEYS
    e2 = si_scr[1]
    a_iota = lax.broadcasted_iota(jnp.int32, (PEER_TOPK, PEER_HEADS, tb), 0)
    front = sv1 + sv2[0][None]
    ptr = jnp.zeros_like(a_iota)
    tops, eids = [], []
    for k in range(PEER_TOPK):
        mx = jnp.max(front, axis=0)
        awin = jnp.min(jnp.where(front == mx[None], a_iota, PEER_TOPK), axis=0)
        hit = a_iota == awin[None]
        bwin = jnp.sum(jnp.where(hit, ptr, 0), axis=0)
        eid = (jnp.sum(jnp.where(hit, e1, 0), axis=0)
               + jnp.sum(jnp.where(a_iota == bwin[None], e2, 0), axis=0))
        nxt = jnp.max(jnp.where(a_iota == (bwin + 1)[None], sv2, -jnp.inf), axis=0)
        front = jnp.where(hit, sv1 + nxt[None], front)
        ptr = jnp.where(hit, ptr + 1, ptr)
        tops.append(mx)
        eids.append(eid)
    top_s = jnp.stack(tops)
    e = jnp.exp(top_s - tops[0][None])
    gate = e / jnp.sum(e, axis=0)[None]
    nj = PEER_TOPK * PEER_HEADS
    eid_ref[...] = jnp.stack(eids).reshape(nj, tb).T
    gate_ref[...] = gate.reshape(nj, tb).T


def _retrieve(q, kb, tb=256):
    n = q.shape[1]
    tb = min(tb, n)
    nj = PEER_HEADS * PEER_TOPK
    dq = 2 * PEER_HEADS * PEER_HALF
    return pl.pallas_call(
        _retrieve_kernel,
        grid=(n // tb,),
        in_specs=[
            pl.BlockSpec((dq, tb), lambda i: (0, i)),
            pl.BlockSpec((2, N_KEYS * PEER_HEADS, dq), lambda i: (0, 0, 0),
                         pipeline_mode=pl.Buffered(1)),
        ],
        out_specs=[pl.BlockSpec((tb, nj), lambda i: (i, 0)),
                   pl.BlockSpec((tb, nj), lambda i: (i, 0))],
        out_shape=[jax.ShapeDtypeStruct((n, nj), jnp.int32),
                   jax.ShapeDtypeStruct((n, nj), F32)],
        scratch_shapes=[
            pltpu.VMEM((N_KEYS, PEER_HEADS, tb), F32),
            pltpu.VMEM((2, PEER_TOPK, PEER_HEADS, tb), F32),
            pltpu.VMEM((2, PEER_TOPK, PEER_HEADS, tb), jnp.int32),
            pltpu.VMEM((N_KEYS, PEER_HEADS, LANES), F32),
            pltpu.VMEM((N_KEYS, PEER_HEADS, LANES), jnp.int32),
        ],
        compiler_params=_cparams(("parallel",)),
        name="retrieve",
    )(q, kb)


def _block_diag_keys(keys):
    eye_h = jnp.eye(PEER_HEADS, dtype=keys.dtype)
    eye_p = jnp.eye(2, dtype=keys.dtype)
    kb = jnp.einsum("hpnc,hg,pr->pnhgrc", keys, eye_h, eye_p)
    return kb.reshape(2, N_KEYS * PEER_HEADS, PEER_HEADS * 2 * PEER_HALF)


_W_PITCH = N_KEYS + 8
_W_GROUP = 16


def _scatter_w_kernel(eid_ref, gate_ref, w_ref, scr_a, scr_b):
    tb = eid_ref.shape[0]
    nj = eid_ref.shape[1]
    ngrp = tb // _W_GROUP
    sub = lax.broadcasted_iota(jnp.int32, (N_KEYS, nj), 0)

    rows_per_token = N_KEYS // _W_GROUP

    def build_token(grp, tl, scr):
        t = pl.multiple_of(grp * _W_GROUP, _W_GROUP) + tl
        e = eid_ref[pl.ds(t, 1), :]
        g = gate_ref[pl.ds(t, 1), :]
        p = jnp.where(sub == (e >> 7), 1.0, 0.0).astype(BF16)
        q = jnp.where(sub == (e & (N_KEYS - 1)), g, 0.0).astype(BF16)
        wt = lax.dot_general(p, q, (((1,), (1,)), ((), ())), preferred_element_type=F32)
        scr[tl * _W_PITCH:tl * _W_PITCH + N_KEYS, :] = wt

    def emit_rows(grp, part, scr):
        base = pl.multiple_of(grp * _W_GROUP, _W_GROUP)
        for r in range(part * rows_per_token, (part + 1) * rows_per_token):
            w_ref[pl.ds(base, _W_GROUP), r * N_KEYS:(r + 1) * N_KEYS] = (
                scr[pl.ds(r, _W_GROUP, stride=_W_PITCH), :].astype(BF16))

    def step(build_grp, build_scr, emit_grp, emit_scr):
        for tl in range(_W_GROUP):
            if build_grp is not None:
                build_token(build_grp, tl, build_scr)
            if emit_grp is not None:
                emit_rows(emit_grp, tl, emit_scr)

    step(0, scr_a, None, None)

    def pair_body(i, carry):
        step(2 * i + 1, scr_b, 2 * i, scr_a)
        step(2 * i + 2, scr_a, 2 * i + 1, scr_b)
        return carry

    lax.fori_loop(0, ngrp // 2 - 1, pair_body, 0)
    step(ngrp - 1, scr_b, ngrp - 2, scr_a)
    step(None, None, ngrp - 1, scr_b)


def _scatter_w(eid, gate, tb=128):
    n, nj = eid.shape
    tb = min(tb, n)
    assert tb % (2 * _W_GROUP) == 0
    return pl.pallas_call(
        _scatter_w_kernel,
        grid=(n // tb,),
        in_specs=[pl.BlockSpec((tb, nj), lambda i: (i, 0)),
                  pl.BlockSpec((tb, nj), lambda i: (i, 0))],
        out_specs=pl.BlockSpec((tb, N_EXPERTS), lambda i: (i, 0)),
        out_shape=jax.ShapeDtypeStruct((n, N_EXPERTS), BF16),
        scratch_shapes=[pltpu.VMEM((_W_GROUP * _W_PITCH, N_KEYS), F32)] * 2,
        compiler_params=_cparams(("parallel",)),
        name="scatter_w",
    )(eid, gate)


def _experts_kernel(xb_ref, u_ref, v_ref, w_ref, h1_ref, g_ref, b_ref, out_ref):
    k = pl.program_id(1)

    @pl.when(k == 0)
    def _():
        out_ref[...] = ALPHA * h1_ref[...]

    a = lax.dot_general(xb_ref[...], u_ref[...], (((1,), (1,)), ((), ())),
                        preferred_element_type=F32)
    act = 0.5 * a * (1.0 + lax.erf(a * (0.5 ** 0.5)))
    wa = (w_ref[...].astype(F32) * act).astype(BF16)
    out_ref[...] += jnp.dot(wa, v_ref[...], preferred_element_type=F32)

    @pl.when(k == pl.num_programs(1) - 1)
    def _():
        out_ref[...] = _layernorm(out_ref[...], g_ref[...], b_ref[...])


def _experts(h1b, u, v, w, h1, g, b, tm=1024, ec=1024):
    n = h1b.shape[0]
    tm = min(tm, n)
    once = pl.Buffered(1)
    return pl.pallas_call(
        _experts_kernel,
        grid=(n // tm, N_EXPERTS // ec),
        in_specs=[
            pl.BlockSpec((tm, D_MODEL), lambda i, k: (i, 0), pipeline_mode=once),
            pl.BlockSpec((ec, D_MODEL), lambda i, k: (k, 0)),
            pl.BlockSpec((ec, D_MODEL), lambda i, k: (k, 0)),
            pl.BlockSpec((tm, ec), lambda i, k: (i, k)),
            pl.BlockSpec((tm, D_MODEL), lambda i, k: (i, 0), pipeline_mode=once),
            pl.BlockSpec((1, D_MODEL), lambda i, k: (0, 0)),
            pl.BlockSpec((1, D_MODEL), lambda i, k: (0, 0)),
        ],
        out_specs=pl.BlockSpec((tm, D_MODEL), lambda i, k: (i, 0), pipeline_mode=once),
        out_shape=jax.ShapeDtypeStruct((n, D_MODEL), F32),
        compiler_params=_cparams(("parallel", "arbitrary")),
        name="experts",
    )(h1b, u, v, w, h1, g, b)


def kernel(x, ln_in_g, ln_in_b, w_in, b_gate, conv_w, conv_b, mh_norm_g, w_out, ln1_g, ln1_b,
           peer_wq, peer_keys, peer_u, peer_v, ln2_g, ln2_b):
    batch, seq, d = x.shape
    n = batch * seq
    row = lambda t: t.reshape(1, -1).astype(F32)
    l = 0
    w_main = w_in[l, :, :D_MAIN].astype(BF16)
    w_gate = jnp.pad(w_in[l, :, D_MAIN:], ((0, 0), (0, LANES - 2 * M_HEADS))).astype(BF16)

    x2 = x.reshape(n, d)
    proj, gates = _ln_proj(x2, row(ln_in_g), row(ln_in_b), w_main, w_gate, tm=_TILES["proj_tm"])

    scans = _gate_scan(gates[:, :2 * M_HEADS].T, b_gate[l].reshape(2 * M_HEADS, 1),
                       tl=_TILES["scan_tl"])
    rowg = scans[M_HEADS:2 * M_HEADS].reshape(M_HEADS, n // CHUNK, CHUNK).transpose(1, 0, 2)
    colg = jnp.pad(scans.T, ((0, 0), (0, M_HEADS)))
    ym = _mlstm(proj, rowg, colg, row(mh_norm_g[l]), batch, seq, ts=_TILES["mlstm_ts"])

    cw = jnp.pad(conv_w[l], ((0, 8 - conv_w.shape[1]), (0, 0)))
    h1, h1b = _mix(proj, ym, x2, row(ln_in_g), row(ln_in_b), cw, row(conv_b[l]), w_out[l].astype(BF16),
                   row(ln1_g[l]), row(ln1_b[l]), seq, tm=_TILES["mix_tm"])

    kb = _block_diag_keys(peer_keys[l]).astype(BF16)
    q = _qproj(h1b, peer_wq[l].T.astype(BF16))
    eid, gate = _retrieve(q, kb, tb=_TILES["retrieve_tb"])
    w = _scatter_w(eid, gate, tb=_TILES["scatter_tb"])
    out = _experts(h1b, peer_u[l].astype(BF16), peer_v[l].astype(BF16), w, h1,
                   row(ln2_g[l]), row(ln2_b[l]), tm=_TILES["experts_tm"], ec=_TILES["experts_ec"])
    return out.reshape(batch, seq, d)
```

```python
import functools
import math

import jax
import jax.numpy as jnp
from jax import lax
from jax.experimental import pallas as pl
from jax.experimental.pallas import tpu as pltpu

F32 = jnp.float32
BF16 = jnp.bfloat16

D_MODEL = 2048
CHUNK = 64
D_CONV = 1024
D_MLSTM = 1024
M_HEADS = 8
M_HEAD_DIM = 128
D_MAIN = 3 * D_CONV + 4 * D_MLSTM
PEER_HEADS = 8
N_KEYS = 128
N_EXPERTS = N_KEYS * N_KEYS
PEER_TOPK = 16
PEER_HALF = 128
DEPTH = 1
ALPHA = (2 * DEPTH) ** 0.25
LN_EPS = 1e-5
LANES = 128
VMEM_LIMIT = 56 * 1024 * 1024

_TILES = dict(proj_tm=1024, scan_tl=2048, mlstm_ts=1024, mix_tm=512, retrieve_tb=512,
              scatter_tb=128, experts_tm=1024, experts_ec=1024)


def _cparams(sem):
    return pltpu.CompilerParams(dimension_semantics=sem, vmem_limit_bytes=VMEM_LIMIT)


def _layernorm(x, g, b):
    mu = jnp.mean(x, axis=-1, keepdims=True)
    xc = x - mu
    var = jnp.mean(xc * xc, axis=-1, keepdims=True)
    return xc * lax.rsqrt(var + LN_EPS) * g + b


_LN_ROWS = 256


def _ln_proj_kernel(x_ref, g_ref, b_ref, w_ref, wg_ref, proj_ref, gate_ref, hb_ref):
    @pl.when(pl.program_id(1) == 0)
    def _():
        def rows_body(r, carry):
            rows = pl.ds(pl.multiple_of(r * _LN_ROWS, _LN_ROWS), _LN_ROWS)
            hb = _layernorm(x_ref[rows, :], g_ref[...], b_ref[...]).astype(BF16)
            hb_ref[rows, :] = hb
            gate_ref[rows, :] = jnp.dot(hb, wg_ref[...], preferred_element_type=F32)
            return carry

        lax.fori_loop(0, x_ref.shape[0] // _LN_ROWS, rows_body, 0)

    proj_ref[...] = jnp.dot(hb_ref[...], w_ref[...], preferred_element_type=F32).astype(BF16)


def _ln_proj(x, g, b, w_main, w_gate, tm=1024, tn=1792):
    n = x.shape[0]
    tm = min(tm, n)
    return pl.pallas_call(
        _ln_proj_kernel,
        grid=(n // tm, D_MAIN // tn),
        in_specs=[
            pl.BlockSpec((tm, D_MODEL), lambda i, j: (i, 0)),
            pl.BlockSpec((1, D_MODEL), lambda i, j: (0, 0)),
            pl.BlockSpec((1, D_MODEL), lambda i, j: (0, 0)),
            pl.BlockSpec((D_MODEL, tn), lambda i, j: (0, j)),
            pl.BlockSpec((D_MODEL, LANES), lambda i, j: (0, 0)),
        ],
        out_specs=[
            pl.BlockSpec((tm, tn), lambda i, j: (i, j)),
            pl.BlockSpec((tm, LANES), lambda i, j: (i, 0)),
        ],
        out_shape=[
            jax.ShapeDtypeStruct((n, D_MAIN), BF16),
            jax.ShapeDtypeStruct((n, LANES), F32),
        ],
        scratch_shapes=[pltpu.VMEM((tm, D_MODEL), BF16)],
        compiler_params=_cparams(("parallel", "arbitrary")),
        name="ln_proj",
    )(x, g, b, w_main, w_gate)


def _gate_scan_kernel(gt_ref, b_ref, out_ref):
    z = gt_ref[...] + b_ref[...]
    li = z[:M_HEADS]
    lf = jax.nn.log_sigmoid(z[M_HEADS:])
    pos = lax.broadcasted_iota(jnp.int32, lf.shape, 1) % CHUNK
    bcum = lf
    shift = 1
    while shift < CHUNK:
        bcum = bcum + jnp.where(pos >= shift, pltpu.roll(bcum, shift, axis=1), 0.0)
        shift *= 2
    r = li - bcum
    cm = r
    shift = 1
    while shift < CHUNK:
        cm = jnp.maximum(cm, jnp.where(pos >= shift, pltpu.roll(cm, shift, axis=1), -jnp.inf))
        shift *= 2
    out_ref[0:8, :] = bcum
    out_ref[8:16, :] = r
    out_ref[16:24, :] = cm


def _gate_scan(gt, bias, tl=2048):
    n = gt.shape[1]
    tl = min(tl, n)
    return pl.pallas_call(
        _gate_scan_kernel,
        grid=(n // tl,),
        in_specs=[pl.BlockSpec((2 * M_HEADS, tl), lambda i: (0, i)),
                  pl.BlockSpec((2 * M_HEADS, 1), lambda i: (0, 0))],
        out_specs=pl.BlockSpec((3 * M_HEADS, tl), lambda i: (0, i)),
        out_shape=jax.ShapeDtypeStruct((3 * M_HEADS, n), F32),
        compiler_params=_cparams(("parallel",)),
        name="gate_scan",
    )(gt, bias)


def _mlstm_kernel(q_ref, k_ref, v_ref, o_ref, row_ref, col_ref, ng_ref, y_ref,
                  cn_ref, m_ref, *, ts):
    @pl.when(pl.program_id(1) == 0)
    def _():
        cn_ref[...] = jnp.zeros_like(cn_ref)
        m_ref[...] = jnp.zeros_like(m_ref)

    L = CHUNK
    dh = M_HEAD_DIM
    scale = dh ** -0.5
    ones_v = jnp.ones((M_HEADS, L, dh), BF16)
    lane_mean = jnp.full((M_HEADS, dh, dh), 1.0 / dh, BF16)
    li = lax.broadcasted_iota(jnp.int32, (L, L), 0)
    lj = lax.broadcasted_iota(jnp.int32, (L, L), 1)
    causal = lj <= li

    def chunk_body(c, carry):
        t0 = pl.multiple_of(c * L, L)
        rows = pl.ds(t0, L)
        colg = col_ref[rows, :]
        rowg = row_ref[c]
        bc8 = colg[:, 0:M_HEADS]
        rc8 = colg[:, M_HEADS:2 * M_HEADS]
        cm8 = colg[:, 2 * M_HEADS:3 * M_HEADS]
        m8 = m_ref[...]
        a8 = jnp.maximum(m8, cm8)
        w_inter8 = jnp.exp(m8 - a8)
        floor8 = jnp.exp(-(bc8 + a8))
        a_last8 = a8[L - 1:L, :]
        wk8 = jnp.exp(rc8 - a_last8)
        decay8 = jnp.exp(m8 - a_last8)
        m_ref[...] = bc8[L - 1:L, :] + a_last8
        heads = range(M_HEADS)
        per_head = lambda ref: jnp.stack([ref[rows, h * dh:(h + 1) * dh] for h in heads])
        column = lambda m: jnp.stack([m[:, h:h + 1] for h in heads])
        bmm = lambda x, y, cx, cy: lax.dot_general(
            x, y, (((cx,), (cy,)), ((0,), (0,))), preferred_element_type=F32)
        q = per_head(q_ref)
        k = per_head(k_ref)
        v = per_head(v_ref)
        a = column(a8)
        w_inter = column(w_inter8)
        wk = column(wk8)
        decay = jnp.stack([decay8[:, h:h + 1] for h in heads])
        w_intra = jnp.exp(jnp.where(causal[None], rowg[:, None, :] - a, -jnp.inf))
        s = bmm(q, k, 2, 2) * scale * w_intra
        s_hi = s.astype(BF16)
        s_lo = (s - s_hi.astype(F32)).astype(BF16)
        sv = bmm(s_hi, jnp.concatenate([v, ones_v], axis=-1), 2, 1)
        cn = cn_ref[...]
        qc = bmm(q, cn.astype(BF16), 2, 1)
        num = sv[..., :dh] + w_inter * qc[..., :dh]
        den = sv[..., dh:] + bmm(s_lo, ones_v, 2, 1) + w_inter * qc[..., dh:]
        hval = num / jnp.maximum(jnp.abs(den), column(floor8))
        wk_wide = jnp.broadcast_to(wk, (M_HEADS, L, dh))
        wkv = jnp.concatenate([wk_wide * v.astype(F32), wk_wide], axis=-1).astype(BF16)
        cn_ref[...] = decay * cn + scale * bmm(k, wkv, 1, 1)
        mu = bmm(hval.astype(BF16), lane_mean, 2, 1)
        hc = hval - mu
        var = bmm((hc * hc).astype(BF16), lane_mean, 2, 1)
        hn = hc * lax.rsqrt(var + LN_EPS)
        for h in heads:
            hs = slice(h * dh, (h + 1) * dh)
            og = jax.nn.sigmoid(o_ref[rows, hs].astype(F32))
            y_ref[rows, hs] = (og * hn[h] * ng_ref[:, hs]).astype(BF16)
        return carry

    lax.fori_loop(0, ts // L, chunk_body, 0)


def _mlstm(proj, rowg, colg, norm_g, batch, seq, ts=1024):
    n = proj.shape[0]
    ts = min(ts, seq)
    nb = seq // ts
    blk = lambda col: pl.BlockSpec((ts, D_MLSTM), lambda b, i, col=col: (b * nb + i, col))
    return pl.pallas_call(
        functools.partial(_mlstm_kernel, ts=ts),
        grid=(batch, nb),
        in_specs=[
            blk(3), blk(4), blk(5), blk(6),
            pl.BlockSpec((ts // CHUNK, M_HEADS, CHUNK), lambda b, i: (b * nb + i, 0, 0)),
            pl.BlockSpec((ts, 4 * M_HEADS), lambda b, i: (b * nb + i, 0)),
            pl.BlockSpec((1, D_MLSTM), lambda b, i: (0, 0)),
        ],
        out_specs=pl.BlockSpec((ts, D_MLSTM), lambda b, i: (b * nb + i, 0)),
        out_shape=jax.ShapeDtypeStruct((n, D_MLSTM), BF16),
        scratch_shapes=[
            pltpu.VMEM((M_HEADS, M_HEAD_DIM, 2 * M_HEAD_DIM), F32),
            pltpu.VMEM((1, M_HEADS), F32),
        ],
        compiler_params=_cparams(("parallel", "arbitrary")),
        name="mlstm",
    )(proj, proj, proj, proj, rowg, colg, norm_g)


def _mix_kernel(cb_ref, cc_ref, ch_ref, pc_ref, ph_ref, ym_ref, x_ref, g0_ref, b0_ref, cw_ref,
                cbias_ref, wo_ref, g_ref, b_ref, h1_ref, h1b_ref, *, blocks_per_seq):
    tm = cb_ref.shape[0]
    z = cc_ref[...].astype(F32) * ch_ref[...].astype(F32)
    halo = pc_ref[...].astype(F32) * ph_ref[...].astype(F32)
    first = (pl.program_id(0) % blocks_per_seq) == 0
    halo = jnp.where(first, 0.0, halo)
    row = lax.broadcasted_iota(jnp.int32, z.shape, 0)
    hr = halo.shape[0]
    z1 = jnp.where(row == 0, halo[hr - 1:hr, :], pltpu.roll(z, 1, axis=0))
    z2 = pltpu.roll(z, 2, axis=0)
    z2 = jnp.where(row == 0, halo[hr - 2:hr - 1, :], z2)
    z2 = jnp.where(row == 1, halo[hr - 1:hr, :], z2)
    conv = cbias_ref[...] + cw_ref[0:1, :] * z2 + cw_ref[1:2, :] * z1 + cw_ref[2:3, :] * z
    y_conv = (cb_ref[...].astype(F32) * conv).astype(BF16)
    mix = (jnp.dot(y_conv, wo_ref[0:D_CONV, :], preferred_element_type=F32)
           + jnp.dot(ym_ref[...], wo_ref[D_CONV:, :], preferred_element_type=F32))
    h = _layernorm(x_ref[...], g0_ref[...], b0_ref[...])
    h1 = _layernorm(ALPHA * h + mix, g_ref[...], b_ref[...])
    h1_ref[...] = h1
    h1b_ref[...] = h1.astype(BF16)


def _mix(proj, ym, x, g0, b0, conv_w, conv_b, w_out, g, b, seq, tm=512):
    n = proj.shape[0]
    tm = min(tm, seq)
    halo = 16
    hb = tm // halo
    cur = lambda col: pl.BlockSpec((tm, D_CONV), lambda i, col=col: (i, col))
    prev = lambda col: pl.BlockSpec((halo, D_CONV),
                                    lambda i, col=col: (jnp.maximum(i * hb - 1, 0), col))
    const = lambda shape: pl.BlockSpec(shape, lambda i: (0, 0))
    return pl.pallas_call(
        functools.partial(_mix_kernel, blocks_per_seq=seq // tm),
        grid=(n // tm,),
        in_specs=[
            cur(0), cur(1), cur(2), prev(1), prev(2),
            pl.BlockSpec((tm, D_MLSTM), lambda i: (i, 0)),
            pl.BlockSpec((tm, D_MODEL), lambda i: (i, 0)),
            const((1, D_MODEL)), const((1, D_MODEL)),
            const((8, D_CONV)), const((1, D_CONV)),
            pl.BlockSpec((D_MODEL, D_MODEL), lambda i: (0, 0), pipeline_mode=pl.Buffered(1)),
            const((1, D_MODEL)), const((1, D_MODEL)),
        ],
        out_specs=[pl.BlockSpec((tm, D_MODEL), lambda i: (i, 0)),
                   pl.BlockSpec((tm, D_MODEL), lambda i: (i, 0))],
        out_shape=[jax.ShapeDtypeStruct((n, D_MODEL), F32),
                   jax.ShapeDtypeStruct((n, D_MODEL), BF16)],
        compiler_params=_cparams(("parallel",)),
        name="mix",
    )(proj, proj, proj, proj, proj, ym, x, g0, b0, conv_w, conv_b, w_out, g, b)


def _qproj_kernel(x_ref, wqt_ref, qt_ref):
    qt_ref[...] = lax.dot_general(wqt_ref[...], x_ref[...], (((1,), (1,)), ((), ())),
                                  preferred_element_type=F32).astype(BF16)


def _qproj(h1b, wqt, tm=1024):
    n = h1b.shape[0]
    tm = min(tm, n)
    dq = wqt.shape[0]
    return pl.pallas_call(
        _qproj_kernel,
        grid=(n // tm,),
        in_specs=[pl.BlockSpec((tm, D_MODEL), lambda i: (i, 0)),
                  pl.BlockSpec((dq, D_MODEL), lambda i: (0, 0), pipeline_mode=pl.Buffered(1))],
        out_specs=pl.BlockSpec((dq, tm), lambda i: (0, i)),
        out_shape=jax.ShapeDtypeStruct((dq, n), BF16),
        compiler_params=_cparams(("parallel",)),
        name="peer_q",
    )(h1b, wqt)


def _oddeven_sort_pairs(n):
    pairs = []
    p = 1
    while p < n:
        k = p
        while k >= 1:
            for j in range(k % p, n - k, 2 * k):
                for i in range(min(k, n - j - k)):
                    if (i + j) // (2 * p) == (i + j + k) // (2 * p):
                        pairs.append((i + j, i + j + k))
            k //= 2
        p *= 2
    return pairs


_SORT16 = _oddeven_sort_pairs(PEER_TOPK)
_BITONIC16 = [(i, i + d) for d in (8, 4, 2, 1) for i in range(PEER_TOPK) if not i & d]


def _compare_exchange(v, ix, pairs):
    for i, j in pairs:
        up = v[j] > v[i]
        v[i], v[j] = jnp.maximum(v[i], v[j]), jnp.minimum(v[i], v[j])
        ix[i], ix[j] = jnp.where(up, ix[j], ix[i]), jnp.where(up, ix[i], ix[j])


def _network_top16(cur_scr, srt_v, srt_i, sv_scr, si_scr, p, lanes):
    K = PEER_TOPK
    for g in range(N_KEYS // K):
        v = [cur_scr[g * K + i, :, lanes] for i in range(K)]
        ix = [jnp.int32(g * K + i) for i in range(K)]
        _compare_exchange(v, ix, _SORT16)
        for i in range(K):
            srt_v[g * K + i] = v[i]
            srt_i[g * K + i] = ix[i]
    dropped = jnp.full(v[0].shape, -jnp.inf, F32)
    span = K
    while span < N_KEYS:
        for a in range(0, N_KEYS, 2 * span):
            b = a + span
            va = [srt_v[a + i] for i in range(K)]
            ia = [srt_i[a + i] for i in range(K)]
            v, ix = [], []
            for i in range(K):
                vb = srt_v[b + K - 1 - i]
                up = vb > va[i]
                v.append(jnp.maximum(va[i], vb))
                ix.append(jnp.where(up, srt_i[b + K - 1 - i], ia[i]))
                dropped = jnp.maximum(dropped, jnp.minimum(va[i], vb))
            _compare_exchange(v, ix, _BITONIC16)
            for i in range(K):
                srt_v[a + i] = v[i]
                srt_i[a + i] = ix[i]
        span *= 2
    distinct = v[K - 1] > dropped
    for i in range(K - 1):
        distinct = distinct & (v[i] > v[i + 1])
    for i in range(K):
        sv_scr[p, i, :, lanes] = v[i]
        si_scr[p, i, :, lanes] = ix[i]
    return jnp.where(distinct, 0, 1)


def _retrieve_kernel(q_ref, kb_ref, eid_ref, gate_ref, cur_scr, sv_scr, si_scr, srt_v, srt_i):
    tb = q_ref.shape[1]
    dh = PEER_HEADS * PEER_HALF

    for p in range(2):
        s = jnp.dot(kb_ref[p], q_ref[p * dh:(p + 1) * dh, :],
                    preferred_element_type=F32)
        cur_scr[...] = s.reshape(N_KEYS, PEER_HEADS, tb)

        def lane_body(lh, tied, p=p):
            lanes = pl.ds(pl.multiple_of(lh * LANES, LANES), LANES)
            return jnp.maximum(tied, _network_top16(cur_scr, srt_v, srt_i, sv_scr, si_scr, p, lanes))

        tied = lax.fori_loop(0, tb // LANES, lane_body, jnp.zeros((PEER_HEADS, LANES), jnp.int32))

        def round_body(k, taken, p=p):
            vals, ids = [], list(range(N_KEYS))
            for n in range(N_KEYS):
                c = jnp.where(taken == n, -jnp.inf, cur_scr[n])
                cur_scr[n] = c
                vals.append(c)
            while len(vals) > 1:
                right = [vals[i + 1] > vals[i] for i in range(0, len(vals), 2)]
                ids = [jnp.where(g, ids[2 * i + 1], ids[2 * i]) for i, g in enumerate(right)]
                vals = [jnp.maximum(vals[2 * i], vals[2 * i + 1]) for i in range(len(right))]
            sv_scr[p, k] = vals[0]
            si_scr[p, k] = ids[0]
            return ids[0]

        @pl.when(jnp.max(tied) > 0)
        def _(round_body=round_body):
            lax.fori_loop(0, PEER_TOPK, round_body, jnp.full((PEER_HEADS, tb), N_KEYS, jnp.int32))

    sv1 = sv_scr[0]
    sv2 = sv_scr[1]
    e1 = si_scr[0] * N_KEYS
    e2 = si_scr[1]
    a_iota = lax.broadcasted_iota(jnp.int32, (PEER_TOPK, PEER_HEADS, tb), 0)
    front = sv1 + sv2[0][None]
    ptr = jnp.zeros_like(a_iota)
    tops, eids = [], []
    for k in range(PEER_TOPK):
        mx = jnp.max(front, axis=0)
        awin = jnp.min(jnp.where(front == mx[None], a_iota, PEER_TOPK), axis=0)
        hit = a_iota == awin[None]
        bwin = jnp.sum(jnp.where(hit, ptr, 0), axis=0)
        eid = (jnp.sum(jnp.where(hit, e1, 0), axis=0)
               + jnp.sum(jnp.where(a_iota == bwin[None], e2, 0), axis=0))
        nxt = jnp.max(jnp.where(a_iota == (bwin + 1)[None], sv2, -jnp.inf), axis=0)
        front = jnp.where(hit, sv1 + nxt[None], front)
        ptr = jnp.where(hit, ptr + 1, ptr)
        tops.append(mx)
        eids.append(eid)
    top_s = jnp.stack(tops)
    e = jnp.exp(top_s - tops[0][None])
    gate = e / jnp.sum(e, axis=0)[None]
    nj = PEER_TOPK * PEER_HEADS
    eid_ref[...] = jnp.stack(eids).reshape(nj, tb).T
    gate_ref[...] = gate.reshape(nj, tb).T


def _retrieve(q, kb, tb=256):
    n = q.shape[1]
    tb = min(tb, n)
    nj = PEER_HEADS * PEER_TOPK
    dq = 2 * PEER_HEADS * PEER_HALF
    return pl.pallas_call(
        _retrieve_kernel,
        grid=(n // tb,),
        in_specs=[
            pl.BlockSpec((dq, tb), lambda i: (0, i)),
            pl.BlockSpec((2, N_KEYS * PEER_HEADS, dq // 2), lambda i: (0, 0, 0),
                         pipeline_mode=pl.Buffered(1)),
        ],
        out_specs=[pl.BlockSpec((tb, nj), lambda i: (i, 0)),
                   pl.BlockSpec((tb, nj), lambda i: (i, 0))],
        out_shape=[jax.ShapeDtypeStruct((n, nj), jnp.int32),
                   jax.ShapeDtypeStruct((n, nj), F32)],
        scratch_shapes=[
            pltpu.VMEM((N_KEYS, PEER_HEADS, tb), F32),
            pltpu.VMEM((2, PEER_TOPK, PEER_HEADS, tb), F32),
            pltpu.VMEM((2, PEER_TOPK, PEER_HEADS, tb), jnp.int32),
            pltpu.VMEM((N_KEYS, PEER_HEADS, LANES), F32),
            pltpu.VMEM((N_KEYS, PEER_HEADS, LANES), jnp.int32),
        ],
        compiler_params=_cparams(("parallel",)),
        name="retrieve",
    )(q, kb)


def _block_diag_keys(keys):
    eye_h = jnp.eye(PEER_HEADS, dtype=keys.dtype)
    kb = jnp.einsum("hpnc,hg->pnhgc", keys, eye_h)
    return kb.reshape(2, N_KEYS * PEER_HEADS, PEER_HEADS * PEER_HALF)


def _half_major_query_weight(wq):
    d = wq.shape[0]
    return wq.reshape(d, PEER_HEADS, 2, PEER_HALF).transpose(2, 1, 3, 0).reshape(-1, d)


_W_PITCH = N_KEYS + 8
_W_GROUP = 16


def _scatter_w_kernel(eid_ref, gate_ref, w_ref, scr_a, scr_b):
    tb = eid_ref.shape[0]
    nj = eid_ref.shape[1]
    ngrp = tb // _W_GROUP
    sub = lax.broadcasted_iota(jnp.int32, (N_KEYS, nj), 0)

    rows_per_token = N_KEYS // _W_GROUP

    def build_token(grp, tl, scr):
        t = pl.multiple_of(grp * _W_GROUP, _W_GROUP) + tl
        e = eid_ref[pl.ds(t, 1), :]
        g = gate_ref[pl.ds(t, 1), :]
        p = jnp.where(sub == (e >> 7), 1.0, 0.0).astype(BF16)
        q = jnp.where(sub == (e & (N_KEYS - 1)), g, 0.0).astype(BF16)
        wt = lax.dot_general(p, q, (((1,), (1,)), ((), ())), preferred_element_type=F32)
        scr[tl * _W_PITCH:tl * _W_PITCH + N_KEYS, :] = wt

    def emit_rows(grp, part, scr):
        base = pl.multiple_of(grp * _W_GROUP, _W_GROUP)
        for r in range(part * rows_per_token, (part + 1) * rows_per_token):
            w_ref[pl.ds(base, _W_GROUP), r * N_KEYS:(r + 1) * N_KEYS] = (
                scr[pl.ds(r, _W_GROUP, stride=_W_PITCH), :].astype(BF16))

    def step(build_grp, build_scr, emit_grp, emit_scr):
        if emit_grp is not None:
            for part in range(_W_GROUP):
                emit_rows(emit_grp, part, emit_scr)
        if build_grp is not None:
            for tl in range(_W_GROUP):
                build_token(build_grp, tl, build_scr)

    step(0, scr_a, None, None)

    def pair_body(i, carry):
        step(2 * i + 1, scr_b, 2 * i, scr_a)
        step(2 * i + 2, scr_a, 2 * i + 1, scr_b)
        return carry

    lax.fori_loop(0, ngrp // 2 - 1, pair_body, 0)
    step(ngrp - 1, scr_b, ngrp - 2, scr_a)
    step(None, None, ngrp - 1, scr_b)


def _scatter_w(eid, gate, tb=128):
    n, nj = eid.shape
    tb = min(tb, n)
    assert tb % (2 * _W_GROUP) == 0
    return pl.pallas_call(
        _scatter_w_kernel,
        grid=(n // tb,),
        in_specs=[pl.BlockSpec((tb, nj), lambda i: (i, 0)),
                  pl.BlockSpec((tb, nj), lambda i: (i, 0))],
        out_specs=pl.BlockSpec((tb, N_EXPERTS), lambda i: (i, 0)),
        out_shape=jax.ShapeDtypeStruct((n, N_EXPERTS), BF16),
        scratch_shapes=[pltpu.VMEM((_W_GROUP * _W_PITCH, N_KEYS), F32)] * 2,
        compiler_params=_cparams(("parallel",)),
        name="scatter_w",
    )(eid, gate)


def _experts_kernel(xb_ref, u_ref, v_ref, w_ref, h1_ref, g_ref, b_ref, out_ref):
    k = pl.program_id(1)

    @pl.when(k == 0)
    def _():
        out_ref[...] = ALPHA * h1_ref[...]

    a = lax.dot_general(xb_ref[...], u_ref[...], (((1,), (1,)), ((), ())),
                        preferred_element_type=F32)
    act = 0.5 * a * (1.0 + lax.erf(a * (0.5 ** 0.5)))
    wa = (w_ref[...].astype(F32) * act).astype(BF16)
    out_ref[...] += jnp.dot(wa, v_ref[...], preferred_element_type=F32)

    @pl.when(k == pl.num_programs(1) - 1)
    def _():
        out_ref[...] = _layernorm(out_ref[...], g_ref[...], b_ref[...])


def _experts(h1b, u, v, w, h1, g, b, tm=1024, ec=1024):
    n = h1b.shape[0]
    tm = min(tm, n)
    once = pl.Buffered(1)
    return pl.pallas_call(
        _experts_kernel,
        grid=(n // tm, N_EXPERTS // ec),
        in_specs=[
            pl.BlockSpec((tm, D_MODEL), lambda i, k: (i, 0), pipeline_mode=once),
            pl.BlockSpec((ec, D_MODEL), lambda i, k: (k, 0)),
            pl.BlockSpec((ec, D_MODEL), lambda i, k: (k, 0)),
            pl.BlockSpec((tm, ec), lambda i, k: (i, k)),
            pl.BlockSpec((tm, D_MODEL), lambda i, k: (i, 0), pipeline_mode=once),
            pl.BlockSpec((1, D_MODEL), lambda i, k: (0, 0)),
            pl.BlockSpec((1, D_MODEL), lambda i, k: (0, 0)),
        ],
        out_specs=pl.BlockSpec((tm, D_MODEL), lambda i, k: (i, 0), pipeline_mode=once),
        out_shape=jax.ShapeDtypeStruct((n, D_MODEL), F32),
        compiler_params=_cparams(("parallel", "arbitrary")),
        name="experts",
    )(h1b, u, v, w, h1, g, b)


def kernel(x, ln_in_g, ln_in_b, w_in, b_gate, conv_w, conv_b, mh_norm_g, w_out, ln1_g, ln1_b,
           peer_wq, peer_keys, peer_u, peer_v, ln2_g, ln2_b):
    batch, seq, d = x.shape
    n = batch * seq
    row = lambda t: t.reshape(1, -1).astype(F32)
    l = 0
    w_main = w_in[l, :, :D_MAIN].astype(BF16)
    w_gate = jnp.pad(w_in[l, :, D_MAIN:], ((0, 0), (0, LANES - 2 * M_HEADS))).astype(BF16)

    x2 = x.reshape(n, d)
    proj, gates = _ln_proj(x2, row(ln_in_g), row(ln_in_b), w_main, w_gate, tm=_TILES["proj_tm"])

    scans = _gate_scan(gates[:, :2 * M_HEADS].T, b_gate[l].reshape(2 * M_HEADS, 1),
                       tl=_TILES["scan_tl"])
    rowg = scans[M_HEADS:2 * M_HEADS].reshape(M_HEADS, n // CHUNK, CHUNK).transpose(1, 0, 2)
    colg = jnp.pad(scans.T, ((0, 0), (0, M_HEADS)))
    ym = _mlstm(proj, rowg, colg, row(mh_norm_g[l]), batch, seq, ts=_TILES["mlstm_ts"])

    cw = jnp.pad(conv_w[l], ((0, 8 - conv_w.shape[1]), (0, 0)))
    h1, h1b = _mix(proj, ym, x2, row(ln_in_g), row(ln_in_b), cw, row(conv_b[l]), w_out[l].astype(BF16),
                   row(ln1_g[l]), row(ln1_b[l]), seq, tm=_TILES["mix_tm"])

    kb = _block_diag_keys(peer_keys[l]).astype(BF16)
    q = _qproj(h1b, _half_major_query_weight(peer_wq[l]).astype(BF16))
    eid, gate = _retrieve(q, kb, tb=_TILES["retrieve_tb"])
    w = _scatter_w(eid, gate, tb=_TILES["scatter_tb"])
    out = _experts(h1b, peer_u[l].astype(BF16), peer_v[l].astype(BF16), w, h1,
                   row(ln2_g[l]), row(ln2_b[l]), tm=_TILES["experts_tm"], ec=_TILES["experts_ec"])
    return out.reshape(batch, seq, d)
```

```python
import functools
import math

import jax
import jax.numpy as jnp
from jax import lax
from jax.experimental import pallas as pl
from jax.experimental.pallas import tpu as pltpu

F32 = jnp.float32
BF16 = jnp.bfloat16

D_MODEL = 2048
CHUNK = 64
D_CONV = 1024
D_MLSTM = 1024
M_HEADS = 8
M_HEAD_DIM = 128
D_MAIN = 3 * D_CONV + 4 * D_MLSTM
PEER_HEADS = 8
N_KEYS = 128
N_EXPERTS = N_KEYS * N_KEYS
PEER_TOPK = 16
PEER_HALF = 128
DEPTH = 1
ALPHA = (2 * DEPTH) ** 0.25
LN_EPS = 1e-5
LANES = 128
VMEM_LIMIT = 56 * 1024 * 1024

_TILES = dict(proj_tm=1024, scan_tl=2048, mlstm_ts=1024, mix_tm=512, retrieve_tb=512,
              scatter_tb=128, experts_tm=1024, experts_ec=1024)


def _cparams(sem):
    return pltpu.CompilerParams(dimension_semantics=sem, vmem_limit_bytes=VMEM_LIMIT)


def _layernorm(x, g, b):
    mu = jnp.mean(x, axis=-1, keepdims=True)
    xc = x - mu
    var = jnp.mean(xc * xc, axis=-1, keepdims=True)
    return xc * lax.rsqrt(var + LN_EPS) * g + b


_LN_ROWS = 256


def _ln_proj_kernel(x_ref, g_ref, b_ref, w_ref, wg_ref, proj_ref, gate_ref, hb_ref):
    @pl.when(pl.program_id(1) == 0)
    def _():
        def rows_body(r, carry):
            rows = pl.ds(pl.multiple_of(r * _LN_ROWS, _LN_ROWS), _LN_ROWS)
            hb = _layernorm(x_ref[rows, :], g_ref[...], b_ref[...]).astype(BF16)
            hb_ref[rows, :] = hb
            gate_ref[rows, :] = jnp.dot(hb, wg_ref[...], preferred_element_type=F32)
            return carry

        lax.fori_loop(0, x_ref.shape[0] // _LN_ROWS, rows_body, 0)

    proj_ref[...] = jnp.dot(hb_ref[...], w_ref[...], preferred_element_type=F32).astype(BF16)


def _ln_proj(x, g, b, w_main, w_gate, tm=1024, tn=1792):
    n = x.shape[0]
    tm = min(tm, n)
    return pl.pallas_call(
        _ln_proj_kernel,
        grid=(n // tm, D_MAIN // tn),
        in_specs=[
            pl.BlockSpec((tm, D_MODEL), lambda i, j: (i, 0)),
            pl.BlockSpec((1, D_MODEL), lambda i, j: (0, 0)),
            pl.BlockSpec((1, D_MODEL), lambda i, j: (0, 0)),
            pl.BlockSpec((D_MODEL, tn), lambda i, j: (0, j)),
            pl.BlockSpec((D_MODEL, LANES), lambda i, j: (0, 0)),
        ],
        out_specs=[
            pl.BlockSpec((tm, tn), lambda i, j: (i, j)),
            pl.BlockSpec((tm, LANES), lambda i, j: (i, 0)),
        ],
        out_shape=[
            jax.ShapeDtypeStruct((n, D_MAIN), BF16),
            jax.ShapeDtypeStruct((n, LANES), F32),
        ],
        scratch_shapes=[pltpu.VMEM((tm, D_MODEL), BF16)],
        compiler_params=_cparams(("parallel", "arbitrary")),
        name="ln_proj",
    )(x, g, b, w_main, w_gate)


def _gate_scan_kernel(gt_ref, b_ref, out_ref):
    z = gt_ref[...] + b_ref[...]
    li = z[:M_HEADS]
    lf = jax.nn.log_sigmoid(z[M_HEADS:])
    pos = lax.broadcasted_iota(jnp.int32, lf.shape, 1) % CHUNK
    bcum = lf
    shift = 1
    while shift < CHUNK:
        bcum = bcum + jnp.where(pos >= shift, pltpu.roll(bcum, shift, axis=1), 0.0)
        shift *= 2
    r = li - bcum
    cm = r
    shift = 1
    while shift < CHUNK:
        cm = jnp.maximum(cm, jnp.where(pos >= shift, pltpu.roll(cm, shift, axis=1), -jnp.inf))
        shift *= 2
    out_ref[0:8, :] = bcum
    out_ref[8:16, :] = r
    out_ref[16:24, :] = cm


def _gate_scan(gt, bias, tl=2048):
    n = gt.shape[1]
    tl = min(tl, n)
    return pl.pallas_call(
        _gate_scan_kernel,
        grid=(n // tl,),
        in_specs=[pl.BlockSpec((2 * M_HEADS, tl), lambda i: (0, i)),
                  pl.BlockSpec((2 * M_HEADS, 1), lambda i: (0, 0))],
        out_specs=pl.BlockSpec((3 * M_HEADS, tl), lambda i: (0, i)),
        out_shape=jax.ShapeDtypeStruct((3 * M_HEADS, n), F32),
        compiler_params=_cparams(("parallel",)),
        name="gate_scan",
    )(gt, bias)


def _mlstm_kernel(q_ref, k_ref, v_ref, o_ref, row_ref, col_ref, ng_ref, y_ref,
                  cn_ref, m_ref, *, ts):
    @pl.when(pl.program_id(1) == 0)
    def _():
        cn_ref[...] = jnp.zeros_like(cn_ref)
        m_ref[...] = jnp.zeros_like(m_ref)

    L = CHUNK
    dh = M_HEAD_DIM
    scale = dh ** -0.5
    ones_v = jnp.ones((M_HEADS, L, dh), BF16)
    lane_mean = jnp.full((M_HEADS, dh, dh), 1.0 / dh, BF16)
    li = lax.broadcasted_iota(jnp.int32, (L, L), 0)
    lj = lax.broadcasted_iota(jnp.int32, (L, L), 1)
    causal = lj <= li

    def chunk_body(c, carry):
        t0 = pl.multiple_of(c * L, L)
        rows = pl.ds(t0, L)
        colg = col_ref[rows, :]
        rowg = row_ref[c]
        bc8 = colg[:, 0:M_HEADS]
        rc8 = colg[:, M_HEADS:2 * M_HEADS]
        cm8 = colg[:, 2 * M_HEADS:3 * M_HEADS]
        m8 = m_ref[...]
        a8 = jnp.maximum(m8, cm8)
        w_inter8 = jnp.exp(m8 - a8)
        floor8 = jnp.exp(-(bc8 + a8))
        a_last8 = a8[L - 1:L, :]
        wk8 = jnp.exp(rc8 - a_last8)
        decay8 = jnp.exp(m8 - a_last8)
        m_ref[...] = bc8[L - 1:L, :] + a_last8
        heads = range(M_HEADS)
        per_head = lambda ref: jnp.stack([ref[rows, h * dh:(h + 1) * dh] for h in heads])
        column = lambda m: jnp.stack([m[:, h:h + 1] for h in heads])
        bmm = lambda x, y, cx, cy: lax.dot_general(
            x, y, (((cx,), (cy,)), ((0,), (0,))), preferred_element_type=F32)
        q = per_head(q_ref)
        k = per_head(k_ref)
        v = per_head(v_ref)
        a = column(a8)
        w_inter = column(w_inter8)
        wk = column(wk8)
        decay = jnp.stack([decay8[:, h:h + 1] for h in heads])
        w_intra = jnp.exp(jnp.where(causal[None], rowg[:, None, :] - a, -jnp.inf))
        s = bmm(q, k, 2, 2) * scale * w_intra
        s_hi = s.astype(BF16)
        s_lo = (s - s_hi.astype(F32)).astype(BF16)
        sv = bmm(s_hi, jnp.concatenate([v, ones_v], axis=-1), 2, 1)
        cn = cn_ref[...]
        qc = bmm(q, cn.astype(BF16), 2, 1)
        num = sv[..., :dh] + w_inter * qc[..., :dh]
        den = sv[..., dh:] + bmm(s_lo, ones_v, 2, 1) + w_inter * qc[..., dh:]
        hval = num / jnp.maximum(jnp.abs(den), column(floor8))
        wk_wide = jnp.broadcast_to(wk, (M_HEADS, L, dh))
        wkv = jnp.concatenate([wk_wide * v.astype(F32), wk_wide], axis=-1).astype(BF16)
        cn_ref[...] = decay * cn + scale * bmm(k, wkv, 1, 1)
        mu = bmm(hval.astype(BF16), lane_mean, 2, 1)
        hc = hval - mu
        var = bmm((hc * hc).astype(BF16), lane_mean, 2, 1)
        hn = hc * lax.rsqrt(var + LN_EPS)
        for h in heads:
            hs = slice(h * dh, (h + 1) * dh)
            og = jax.nn.sigmoid(o_ref[rows, hs].astype(F32))
            y_ref[rows, hs] = (og * hn[h] * ng_ref[:, hs]).astype(BF16)
        return carry

    lax.fori_loop(0, ts // L, chunk_body, 0)


def _mlstm(proj, rowg, colg, norm_g, batch, seq, ts=1024):
    n = proj.shape[0]
    ts = min(ts, seq)
    nb = seq // ts
    blk = lambda col: pl.BlockSpec((ts, D_MLSTM), lambda b, i, col=col: (b * nb + i, col))
    return pl.pallas_call(
        functools.partial(_mlstm_kernel, ts=ts),
        grid=(batch, nb),
        in_specs=[
            blk(3), blk(4), blk(5), blk(6),
            pl.BlockSpec((ts // CHUNK, M_HEADS, CHUNK), lambda b, i: (b * nb + i, 0, 0)),
            pl.BlockSpec((ts, 4 * M_HEADS), lambda b, i: (b * nb + i, 0)),
            pl.BlockSpec((1, D_MLSTM), lambda b, i: (0, 0)),
        ],
        out_specs=pl.BlockSpec((ts, D_MLSTM), lambda b, i: (b * nb + i, 0)),
        out_shape=jax.ShapeDtypeStruct((n, D_MLSTM), BF16),
        scratch_shapes=[
            pltpu.VMEM((M_HEADS, M_HEAD_DIM, 2 * M_HEAD_DIM), F32),
            pltpu.VMEM((1, M_HEADS), F32),
        ],
        compiler_params=_cparams(("parallel", "arbitrary")),
        name="mlstm",
    )(proj, proj, proj, proj, rowg, colg, norm_g)


def _mix_kernel(cb_ref, cc_ref, ch_ref, pc_ref, ph_ref, ym_ref, x_ref, g0_ref, b0_ref, cw_ref,
                cbias_ref, wo_ref, g_ref, b_ref, h1_ref, h1b_ref, *, blocks_per_seq):
    tm = cb_ref.shape[0]
    z = cc_ref[...].astype(F32) * ch_ref[...].astype(F32)
    halo = pc_ref[...].astype(F32) * ph_ref[...].astype(F32)
    first = (pl.program_id(0) % blocks_per_seq) == 0
    halo = jnp.where(first, 0.0, halo)
    row = lax.broadcasted_iota(jnp.int32, z.shape, 0)
    hr = halo.shape[0]
    z1 = jnp.where(row == 0, halo[hr - 1:hr, :], pltpu.roll(z, 1, axis=0))
    z2 = pltpu.roll(z, 2, axis=0)
    z2 = jnp.where(row == 0, halo[hr - 2:hr - 1, :], z2)
    z2 = jnp.where(row == 1, halo[hr - 1:hr, :], z2)
    conv = cbias_ref[...] + cw_ref[0:1, :] * z2 + cw_ref[1:2, :] * z1 + cw_ref[2:3, :] * z
    y_conv = (cb_ref[...].astype(F32) * conv).astype(BF16)
    mix = (jnp.dot(y_conv, wo_ref[0:D_CONV, :], preferred_element_type=F32)
           + jnp.dot(ym_ref[...], wo_ref[D_CONV:, :], preferred_element_type=F32))
    h = _layernorm(x_ref[...], g0_ref[...], b0_ref[...])
    h1 = _layernorm(ALPHA * h + mix, g_ref[...], b_ref[...])
    h1_ref[...] = h1
    h1b_ref[...] = h1.astype(BF16)


def _mix(proj, ym, x, g0, b0, conv_w, conv_b, w_out, g, b, seq, tm=512):
    n = proj.shape[0]
    tm = min(tm, seq)
    halo = 16
    hb = tm // halo
    cur = lambda col: pl.BlockSpec((tm, D_CONV), lambda i, col=col: (i, col))
    prev = lambda col: pl.BlockSpec((halo, D_CONV),
                                    lambda i, col=col: (jnp.maximum(i * hb - 1, 0), col))
    const = lambda shape: pl.BlockSpec(shape, lambda i: (0, 0))
    return pl.pallas_call(
        functools.partial(_mix_kernel, blocks_per_seq=seq // tm),
        grid=(n // tm,),
        in_specs=[
            cur(0), cur(1), cur(2), prev(1), prev(2),
            pl.BlockSpec((tm, D_MLSTM), lambda i: (i, 0)),
            pl.BlockSpec((tm, D_MODEL), lambda i: (i, 0)),
            const((1, D_MODEL)), const((1, D_MODEL)),
            const((8, D_CONV)), const((1, D_CONV)),
            pl.BlockSpec((D_MODEL, D_MODEL), lambda i: (0, 0), pipeline_mode=pl.Buffered(1)),
            const((1, D_MODEL)), const((1, D_MODEL)),
        ],
        out_specs=[pl.BlockSpec((tm, D_MODEL), lambda i: (i, 0)),
                   pl.BlockSpec((tm, D_MODEL), lambda i: (i, 0))],
        out_shape=[jax.ShapeDtypeStruct((n, D_MODEL), F32),
                   jax.ShapeDtypeStruct((n, D_MODEL), BF16)],
        compiler_params=_cparams(("parallel",)),
        name="mix",
    )(proj, proj, proj, proj, proj, ym, x, g0, b0, conv_w, conv_b, w_out, g, b)


def _qproj_kernel(x_ref, wqt_ref, qt_ref):
    qt_ref[...] = lax.dot_general(wqt_ref[...], x_ref[...], (((1,), (1,)), ((), ())),
                                  preferred_element_type=F32).astype(BF16)


def _qproj(h1b, wqt, tm=1024):
    n = h1b.shape[0]
    tm = min(tm, n)
    dq = wqt.shape[0]
    return pl.pallas_call(
        _qproj_kernel,
        grid=(n // tm,),
        in_specs=[pl.BlockSpec((tm, D_MODEL), lambda i: (i, 0)),
                  pl.BlockSpec((dq, D_MODEL), lambda i: (0, 0), pipeline_mode=pl.Buffered(1))],
        out_specs=pl.BlockSpec((dq, tm), lambda i: (0, i)),
        out_shape=jax.ShapeDtypeStruct((dq, n), BF16),
        compiler_params=_cparams(("parallel",)),
        name="peer_q",
    )(h1b, wqt)


def _oddeven_sort_pairs(n):
    pairs = []
    p = 1
    while p < n:
        k = p
        while k >= 1:
            for j in range(k % p, n - k, 2 * k):
                for i in range(min(k, n - j - k)):
                    if (i + j) // (2 * p) == (i + j + k) // (2 * p):
                        pairs.append((i + j, i + j + k))
            k //= 2
        p *= 2
    return pairs


_SORT16 = _oddeven_sort_pairs(PEER_TOPK)
_BITONIC16 = [(i, i + d) for d in (8, 4, 2, 1) for i in range(PEER_TOPK) if not i & d]


def _compare_exchange(v, ix, pairs):
    for i, j in pairs:
        up = v[j] > v[i]
        v[i], v[j] = jnp.maximum(v[i], v[j]), jnp.minimum(v[i], v[j])
        ix[i], ix[j] = jnp.where(up, ix[j], ix[i]), jnp.where(up, ix[i], ix[j])


def _network_top16(cur_scr, srt_v, srt_i, sv_scr, si_scr, p, lanes):
    K = PEER_TOPK
    for g in range(N_KEYS // K):
        v = [cur_scr[g * K + i, :, lanes] for i in range(K)]
        ix = [jnp.int32(g * K + i) for i in range(K)]
        _compare_exchange(v, ix, _SORT16)
        for i in range(K):
            srt_v[g * K + i] = v[i]
            srt_i[g * K + i] = ix[i]
    dropped = jnp.full(v[0].shape, -jnp.inf, F32)
    span = K
    while span < N_KEYS:
        for a in range(0, N_KEYS, 2 * span):
            b = a + span
            va = [srt_v[a + i] for i in range(K)]
            ia = [srt_i[a + i] for i in range(K)]
            v, ix = [], []
            for i in range(K):
                vb = srt_v[b + K - 1 - i]
                up = vb > va[i]
                v.append(jnp.maximum(va[i], vb))
                ix.append(jnp.where(up, srt_i[b + K - 1 - i], ia[i]))
                dropped = jnp.maximum(dropped, jnp.minimum(va[i], vb))
            _compare_exchange(v, ix, _BITONIC16)
            for i in range(K):
                srt_v[a + i] = v[i]
                srt_i[a + i] = ix[i]
        span *= 2
    distinct = v[K - 1] > dropped
    for i in range(K - 1):
        distinct = distinct & (v[i] > v[i + 1])
    for i in range(K):
        sv_scr[p, i, :, lanes] = v[i]
        si_scr[p, i, :, lanes] = ix[i]
    return jnp.where(distinct, 0, 1)


def _retrieve_kernel(q_ref, kb_ref, eid_ref, gate_ref, cur_scr, sv_scr, si_scr, srt_v, srt_i):
    tb = q_ref.shape[1]
    dh = PEER_HEADS * PEER_HALF

    for p in range(2):
        s = jnp.dot(kb_ref[p], q_ref[p * dh:(p + 1) * dh, :],
                    preferred_element_type=F32)
        cur_scr[...] = s.reshape(N_KEYS, PEER_HEADS, tb)

        def lane_body(lh, tied, p=p):
            lanes = pl.ds(pl.multiple_of(lh * LANES, LANES), LANES)
            return jnp.maximum(tied, _network_top16(cur_scr, srt_v, srt_i, sv_scr, si_scr, p, lanes))

        tied = lax.fori_loop(0, tb // LANES, lane_body, jnp.zeros((PEER_HEADS, LANES), jnp.int32))

        def round_body(k, taken, p=p):
            vals, ids = [], list(range(N_KEYS))
            for n in range(N_KEYS):
                c = jnp.where(taken == n, -jnp.inf, cur_scr[n])
                cur_scr[n] = c
                vals.append(c)
            while len(vals) > 1:
                right = [vals[i + 1] > vals[i] for i in range(0, len(vals), 2)]
                ids = [jnp.where(g, ids[2 * i + 1], ids[2 * i]) for i, g in enumerate(right)]
                vals = [jnp.maximum(vals[2 * i], vals[2 * i + 1]) for i in range(len(right))]
            sv_scr[p, k] = vals[0]
            si_scr[p, k] = ids[0]
            return ids[0]

        @pl.when(jnp.max(tied) > 0)
        def _(round_body=round_body):
            lax.fori_loop(0, PEER_TOPK, round_body, jnp.full((PEER_HEADS, tb), N_KEYS, jnp.int32))

    sv1 = sv_scr[0]
    sv2 = sv_scr[1]
    e1 = si_scr[0] * N_KEYS
    e2 = si_scr[1]
    a_iota = lax.broadcasted_iota(jnp.int32, (PEER_TOPK, PEER_HEADS, tb), 0)
    front = sv1 + sv2[0][None]
    ptr = jnp.zeros_like(a_iota)
    tops, eids = [], []
    for k in range(PEER_TOPK):
        mx = jnp.max(front, axis=0)
        awin = jnp.min(jnp.where(front == mx[None], a_iota, PEER_TOPK), axis=0)
        hit = a_iota == awin[None]
        bwin = jnp.sum(jnp.where(hit, ptr, 0), axis=0)
        eid = (jnp.sum(jnp.where(hit, e1, 0), axis=0)
               + jnp.sum(jnp.where(a_iota == bwin[None], e2, 0), axis=0))
        nxt = jnp.max(jnp.where(a_iota == (bwin + 1)[None], sv2, -jnp.inf), axis=0)
        front = jnp.where(hit, sv1 + nxt[None], front)
        ptr = jnp.where(hit, ptr + 1, ptr)
        tops.append(mx)
        eids.append(eid)
    top_s = jnp.stack(tops)
    e = jnp.exp(top_s - tops[0][None])
    gate = e / jnp.sum(e, axis=0)[None]
    nj = PEER_TOPK * PEER_HEADS
    eid_ref[...] = jnp.stack(eids).reshape(nj, tb).T
    gate_ref[...] = gate.reshape(nj, tb).T


def _retrieve(q, kb, tb=256):
    n = q.shape[1]
    tb = min(tb, n)
    nj = PEER_HEADS * PEER_TOPK
    dq = 2 * PEER_HEADS * PEER_HALF
    return pl.pallas_call(
        _retrieve_kernel,
        grid=(n // tb,),
        in_specs=[
            pl.BlockSpec((dq, tb), lambda i: (0, i)),
            pl.BlockSpec((2, N_KEYS * PEER_HEADS, dq // 2), lambda i: (0, 0, 0),
                         pipeline_mode=pl.Buffered(1)),
        ],
        out_specs=[pl.BlockSpec((tb, nj), lambda i: (i, 0)),
                   pl.BlockSpec((tb, nj), lambda i: (i, 0))],
        out_shape=[jax.ShapeDtypeStruct((n, nj), jnp.int32),
                   jax.ShapeDtypeStruct((n, nj), F32)],
        scratch_shapes=[
            pltpu.VMEM((N_KEYS, PEER_HEADS, tb), F32),
            pltpu.VMEM((2, PEER_TOPK, PEER_HEADS, tb), F32),
            pltpu.VMEM((2, PEER_TOPK, PEER_HEADS, tb), jnp.int32),
            pltpu.VMEM((N_KEYS, PEER_HEADS, LANES), F32),
            pltpu.VMEM((N_KEYS, PEER_HEADS, LANES), jnp.int32),
        ],
        compiler_params=_cparams(("parallel",)),
        name="retrieve",
    )(q, kb)


def _block_diag_keys(keys):
    eye_h = jnp.eye(PEER_HEADS, dtype=keys.dtype)
    kb = jnp.einsum("hpnc,hg->pnhgc", keys, eye_h)
    return kb.reshape(2, N_KEYS * PEER_HEADS, PEER_HEADS * PEER_HALF)


def _half_major_query_weight(wq):
    d = wq.shape[0]
    return wq.reshape(d, PEER_HEADS, 2, PEER_HALF).transpose(2, 1, 3, 0).reshape(-1, d)


_W_GROUP = 16
_W_PITCH = _W_GROUP + 4


def _scatter_w_kernel(eid_ref, gate_ref, w_ref, scr_a, scr_b):
    tb = eid_ref.shape[0]
    nj = eid_ref.shape[1]
    ngrp = tb // _W_GROUP
    sub = lax.broadcasted_iota(jnp.int32, (N_KEYS, nj), 0)

    def build(grp, scr):
        base = pl.multiple_of(grp * _W_GROUP, _W_GROUP)
        for tl in range(_W_GROUP):
            e = eid_ref[pl.ds(base + tl, 1), :]
            g = gate_ref[pl.ds(base + tl, 1), :]
            p = jnp.where(sub == (e >> 7), 1.0, 0.0).astype(BF16)
            q = jnp.where(sub == (e & (N_KEYS - 1)), g, 0.0).astype(BF16)
            wt = lax.dot_general(p, q, (((1,), (1,)), ((), ())), preferred_element_type=F32)
            scr[pl.ds(tl, N_KEYS, stride=_W_PITCH), :] = wt

    def emit(grp, scr):
        base = pl.multiple_of(grp * _W_GROUP, _W_GROUP)
        for r in range(N_KEYS):
            tile = scr[r * _W_PITCH:r * _W_PITCH + _W_GROUP, :]
            w_ref[pl.ds(base, _W_GROUP), r * N_KEYS:(r + 1) * N_KEYS] = tile.astype(BF16)

    build(0, scr_a)

    def pair_body(i, carry):
        build(2 * i + 1, scr_b)
        emit(2 * i, scr_a)
        build(2 * i + 2, scr_a)
        emit(2 * i + 1, scr_b)
        return carry

    lax.fori_loop(0, ngrp // 2 - 1, pair_body, 0)
    build(ngrp - 1, scr_b)
    emit(ngrp - 2, scr_a)
    emit(ngrp - 1, scr_b)


def _scatter_w(eid, gate, tb=128):
    n, nj = eid.shape
    tb = min(tb, n)
    assert tb % (2 * _W_GROUP) == 0
    return pl.pallas_call(
        _scatter_w_kernel,
        grid=(n // tb,),
        in_specs=[pl.BlockSpec((tb, nj), lambda i: (i, 0)),
                  pl.BlockSpec((tb, nj), lambda i: (i, 0))],
        out_specs=pl.BlockSpec((tb, N_EXPERTS), lambda i: (i, 0)),
        out_shape=jax.ShapeDtypeStruct((n, N_EXPERTS), BF16),
        scratch_shapes=[pltpu.VMEM((N_KEYS * _W_PITCH, N_KEYS), F32)] * 2,
        compiler_params=_cparams(("parallel",)),
        name="scatter_w",
    )(eid, gate)


def _experts_kernel(xb_ref, u_ref, v_ref, w_ref, h1_ref, g_ref, b_ref, out_ref):
    k = pl.program_id(1)

    @pl.when(k == 0)
    def _():
        out_ref[...] = ALPHA * h1_ref[...]

    a = lax.dot_general(xb_ref[...], u_ref[...], (((1,), (1,)), ((), ())),
                        preferred_element_type=F32)
    act = 0.5 * a * (1.0 + lax.erf(a * (0.5 ** 0.5)))
    wa = (w_ref[...].astype(F32) * act).astype(BF16)
    out_ref[...] += jnp.dot(wa, v_ref[...], preferred_element_type=F32)

    @pl.when(k == pl.num_programs(1) - 1)
    def _():
        out_ref[...] = _layernorm(out_ref[...], g_ref[...], b_ref[...])


def _experts(h1b, u, v, w, h1, g, b, tm=1024, ec=1024):
    n = h1b.shape[0]
    tm = min(tm, n)
    once = pl.Buffered(1)
    return pl.pallas_call(
        _experts_kernel,
        grid=(n // tm, N_EXPERTS // ec),
        in_specs=[
            pl.BlockSpec((tm, D_MODEL), lambda i, k: (i, 0), pipeline_mode=once),
            pl.BlockSpec((ec, D_MODEL), lambda i, k: (k, 0)),
            pl.BlockSpec((ec, D_MODEL), lambda i, k: (k, 0)),
            pl.BlockSpec((tm, ec), lambda i, k: (i, k)),
            pl.BlockSpec((tm, D_MODEL), lambda i, k: (i, 0), pipeline_mode=once),
            pl.BlockSpec((1, D_MODEL), lambda i, k: (0, 0)),
            pl.BlockSpec((1, D_MODEL), lambda i, k: (0, 0)),
        ],
        out_specs=pl.BlockSpec((tm, D_MODEL), lambda i, k: (i, 0), pipeline_mode=once),
        out_shape=jax.ShapeDtypeStruct((n, D_MODEL), F32),
        compiler_params=_cparams(("parallel", "arbitrary")),
        name="experts",
    )(h1b, u, v, w, h1, g, b)


def kernel(x, ln_in_g, ln_in_b, w_in, b_gate, conv_w, conv_b, mh_norm_g, w_out, ln1_g, ln1_b,
           peer_wq, peer_keys, peer_u, peer_v, ln2_g, ln2_b):
    batch, seq, d = x.shape
    n = batch * seq
    row = lambda t: t.reshape(1, -1).astype(F32)
    l = 0
    w_main = w_in[l, :, :D_MAIN].astype(BF16)
    w_gate = jnp.pad(w_in[l, :, D_MAIN:], ((0, 0), (0, LANES - 2 * M_HEADS))).astype(BF16)

    x2 = x.reshape(n, d)
    proj, gates = _ln_proj(x2, row(ln_in_g), row(ln_in_b), w_main, w_gate, tm=_TILES["proj_tm"])

    scans = _gate_scan(gates[:, :2 * M_HEADS].T, b_gate[l].reshape(2 * M_HEADS, 1),
                       tl=_TILES["scan_tl"])
    rowg = scans[M_HEADS:2 * M_HEADS].reshape(M_HEADS, n // CHUNK, CHUNK).transpose(1, 0, 2)
    colg = jnp.pad(scans.T, ((0, 0), (0, M_HEADS)))
    ym = _mlstm(proj, rowg, colg, row(mh_norm_g[l]), batch, seq, ts=_TILES["mlstm_ts"])

    cw = jnp.pad(conv_w[l], ((0, 8 - conv_w.shape[1]), (0, 0)))
    h1, h1b = _mix(proj, ym, x2, row(ln_in_g), row(ln_in_b), cw, row(conv_b[l]), w_out[l].astype(BF16),
                   row(ln1_g[l]), row(ln1_b[l]), seq, tm=_TILES["mix_tm"])

    kb = _block_diag_keys(peer_keys[l]).astype(BF16)
    q = _qproj(h1b, _half_major_query_weight(peer_wq[l]).astype(BF16))
    eid, gate = _retrieve(q, kb, tb=_TILES["retrieve_tb"])
    w = _scatter_w(eid, gate, tb=_TILES["scatter_tb"])
    out = _experts(h1b, peer_u[l].astype(BF16), peer_v[l].astype(BF16), w, h1,
                   row(ln2_g[l]), row(ln2_b[l]), tm=_TILES["experts_tm"], ec=_TILES["experts_ec"])
    return out.reshape(batch, seq, d)
```

```python
import functools
import math

import jax
import jax.numpy as jnp
from jax import lax
from jax.experimental import pallas as pl
from jax.experimental.pallas import tpu as pltpu

F32 = jnp.float32
BF16 = jnp.bfloat16

D_MODEL = 2048
CHUNK = 64
D_CONV = 1024
D_MLSTM = 1024
M_HEADS = 8
M_HEAD_DIM = 128
D_MAIN = 3 * D_CONV + 4 * D_MLSTM
PEER_HEADS = 8
N_KEYS = 128
N_EXPERTS = N_KEYS * N_KEYS
PEER_TOPK = 16
PEER_HALF = 128
DEPTH = 1
ALPHA = (2 * DEPTH) ** 0.25
LN_EPS = 1e-5
LANES = 128
VMEM_LIMIT = 56 * 1024 * 1024

_TILES = dict(proj_tm=1024, scan_tl=2048, mlstm_ts=1024, mix_tm=512, retrieve_tb=512,
              scatter_tb=128, experts_tm=1024, experts_ec=1024)


def _cparams(sem):
    return pltpu.CompilerParams(dimension_semantics=sem, vmem_limit_bytes=VMEM_LIMIT)


def _layernorm(x, g, b):
    mu = jnp.mean(x, axis=-1, keepdims=True)
    xc = x - mu
    var = jnp.mean(xc * xc, axis=-1, keepdims=True)
    return xc * lax.rsqrt(var + LN_EPS) * g + b


_LN_ROWS = 256


def _ln_proj_kernel(x_ref, g_ref, b_ref, w_ref, wg_ref, proj_ref, gate_ref, hb_ref):
    @pl.when(pl.program_id(1) == 0)
    def _():
        def rows_body(r, carry):
            rows = pl.ds(pl.multiple_of(r * _LN_ROWS, _LN_ROWS), _LN_ROWS)
            hb = _layernorm(x_ref[rows, :], g_ref[...], b_ref[...]).astype(BF16)
            hb_ref[rows, :] = hb
            gate_ref[rows, :] = jnp.dot(hb, wg_ref[...], preferred_element_type=F32)
            return carry

        lax.fori_loop(0, x_ref.shape[0] // _LN_ROWS, rows_body, 0)

    proj_ref[...] = jnp.dot(hb_ref[...], w_ref[...], preferred_element_type=F32).astype(BF16)


def _ln_proj(x, g, b, w_main, w_gate, tm=1024, tn=1792):
    n = x.shape[0]
    tm = min(tm, n)
    return pl.pallas_call(
        _ln_proj_kernel,
        grid=(n // tm, D_MAIN // tn),
        in_specs=[
            pl.BlockSpec((tm, D_MODEL), lambda i, j: (i, 0)),
            pl.BlockSpec((1, D_MODEL), lambda i, j: (0, 0)),
            pl.BlockSpec((1, D_MODEL), lambda i, j: (0, 0)),
            pl.BlockSpec((D_MODEL, tn), lambda i, j: (0, j)),
            pl.BlockSpec((D_MODEL, LANES), lambda i, j: (0, 0)),
        ],
        out_specs=[
            pl.BlockSpec((tm, tn), lambda i, j: (i, j)),
            pl.BlockSpec((tm, LANES), lambda i, j: (i, 0)),
        ],
        out_shape=[
            jax.ShapeDtypeStruct((n, D_MAIN), BF16),
            jax.ShapeDtypeStruct((n, LANES), F32),
        ],
        scratch_shapes=[pltpu.VMEM((tm, D_MODEL), BF16)],
        compiler_params=_cparams(("parallel", "arbitrary")),
        name="ln_proj",
    )(x, g, b, w_main, w_gate)


def _gate_scan_kernel(gt_ref, b_ref, out_ref):
    z = gt_ref[...] + b_ref[...]
    li = z[:M_HEADS]
    lf = jax.nn.log_sigmoid(z[M_HEADS:])
    pos = lax.broadcasted_iota(jnp.int32, lf.shape, 1) % CHUNK
    bcum = lf
    shift = 1
    while shift < CHUNK:
        bcum = bcum + jnp.where(pos >= shift, pltpu.roll(bcum, shift, axis=1), 0.0)
        shift *= 2
    r = li - bcum
    cm = r
    shift = 1
    while shift < CHUNK:
        cm = jnp.maximum(cm, jnp.where(pos >= shift, pltpu.roll(cm, shift, axis=1), -jnp.inf))
        shift *= 2
    out_ref[0:8, :] = bcum
    out_ref[8:16, :] = r
    out_ref[16:24, :] = cm


def _gate_scan(gt, bias, tl=2048):
    n = gt.shape[1]
    tl = min(tl, n)
    return pl.pallas_call(
        _gate_scan_kernel,
        grid=(n // tl,),
        in_specs=[pl.BlockSpec((2 * M_HEADS, tl), lambda i: (0, i)),
                  pl.BlockSpec((2 * M_HEADS, 1), lambda i: (0, 0))],
        out_specs=pl.BlockSpec((3 * M_HEADS, tl), lambda i: (0, i)),
        out_shape=jax.ShapeDtypeStruct((3 * M_HEADS, n), F32),
        compiler_params=_cparams(("parallel",)),
        name="gate_scan",
    )(gt, bias)


def _mlstm_kernel(q_ref, k_ref, v_ref, o_ref, row_ref, cola_ref, colb_ref, spread_ref, ng_ref,
                  y_ref, cn_ref, m_ref, *, ts):
    @pl.when(pl.program_id(1) == 0)
    def _():
        cn_ref[...] = jnp.zeros_like(cn_ref)
        m_ref[...] = jnp.zeros_like(m_ref)

    L = CHUNK
    dh = M_HEAD_DIM
    scale = dh ** -0.5
    ones_v = jnp.ones((M_HEADS, L, dh), BF16)
    lane_mean = jnp.full((M_HEADS, dh, dh), 1.0 / dh, BF16)
    li = lax.broadcasted_iota(jnp.int32, (L, L), 0)
    lj = lax.broadcasted_iota(jnp.int32, (L, L), 1)
    causal = lj <= li
    lane = lax.broadcasted_iota(jnp.int32, (L, LANES), 1)
    lane_grp = lane // _COL_GROUP
    lane_in = lane % _COL_GROUP
    is_wk = (lane_grp < 3) & (lane_in >= M_HEADS) & (lane_in < 2 * M_HEADS)
    is_a = (lane_grp < 3) & (lane_in >= 2 * M_HEADS) & (lane_in < 3 * M_HEADS)
    is_wi = (lane_grp < 3) & (lane_in >= 3 * M_HEADS)

    def chunk_body(c, carry):
        t0 = pl.multiple_of(c * L, L)
        rows = pl.ds(t0, L)
        rowg = row_ref[c]
        ca = cola_ref[rows, :]
        cb = colb_ref[rows, :]
        m_row = m_ref[...]
        a_l = jnp.maximum(m_row, ca)
        w_inter_l = jnp.exp(m_row - a_l)
        floor_l = jnp.exp(-(cb + a_l))
        a_last = a_l[L - 1:L, :]
        wk_l = jnp.exp(ca - pltpu.roll(a_last, LANES - M_HEADS, axis=1))
        decay_row = jnp.exp(m_row - a_last)
        m_ref[...] = cb[L - 1:L, :] + a_last
        heads = range(M_HEADS)
        per_head = lambda ref: jnp.stack([ref[rows, h * dh:(h + 1) * dh] for h in heads])
        bmm = lambda x, y, cx, cy: lax.dot_general(
            x, y, (((cx,), (cy,)), ((0,), (0,))), preferred_element_type=F32)
        x = jnp.where(is_wk, wk_l, jnp.where(is_a, a_l, jnp.where(is_wi, w_inter_l, 0.0)))
        x_hi = x.astype(BF16)
        x_r1 = x - x_hi.astype(F32)
        x_mid = x_r1.astype(BF16)
        x_lo = (x_r1 - x_mid.astype(F32)).astype(BF16)
        parts = jnp.where(lane_grp == 0, x_hi, jnp.where(lane_grp == 1, x_mid, x_lo))
        rep = jnp.dot(parts, spread_ref[...], preferred_element_type=F32)
        wide = lambda qi: jnp.stack([rep[:, (qi * M_HEADS + h) * dh:(qi * M_HEADS + h + 1) * dh]
                                     for h in heads])
        a = wide(0)[..., :L]
        w_inter = wide(1)
        wk_wide = wide(2)
        floor = jnp.stack([floor_l[:, 3 * 32 + h:3 * 32 + h + 1] for h in heads])
        decay = jnp.stack([decay_row[:, 2 * M_HEADS + h:2 * M_HEADS + h + 1] for h in heads])
        q = per_head(q_ref)
        k = per_head(k_ref)
        v = per_head(v_ref)
        w_intra = jnp.exp(jnp.where(causal[None], rowg[:, None, :] - a, -jnp.inf))
        s = bmm(q, k, 2, 2) * scale * w_intra
        s_hi = s.astype(BF16)
        s_lo = (s - s_hi.astype(F32)).astype(BF16)
        sv = bmm(s_hi, jnp.concatenate([v, ones_v], axis=-1), 2, 1)
        cn = cn_ref[...]
        qc = bmm(q, cn.astype(BF16), 2, 1)
        num = sv[..., :dh] + w_inter * qc[..., :dh]
        den = sv[..., dh:] + bmm(s_lo, ones_v, 2, 1) + w_inter * qc[..., dh:]
        hval = num / jnp.maximum(jnp.abs(den), floor)
        wkv =jnp.concatenate([wk_wide * v.astype(F32), wk_wide], axis=-1).astype(BF16)
        cn_ref[...] = decay * cn + scale * bmm(k, wkv, 1, 1)
        mu = bmm(hval.astype(BF16), lane_mean, 2, 1)
        hc = hval - mu
        var = bmm((hc * hc).astype(BF16), lane_mean, 2, 1)
        hn = hc * lax.rsqrt(var + LN_EPS)
        for h in heads:
            hs = slice(h * dh, (h + 1) * dh)
            og = jax.nn.sigmoid(o_ref[rows, hs].astype(F32))
            y_ref[rows, hs] = (og * hn[h] * ng_ref[:, hs]).astype(BF16)
        return carry

    lax.fori_loop(0, ts // L, chunk_body, 0, unroll=2)


_COL_GROUP = 32


def _gate_columns(scans):
    st = scans.T
    bc, r, cm = st[:, :M_HEADS], st[:, M_HEADS:2 * M_HEADS], st[:, 2 * M_HEADS:]
    z = jnp.zeros_like(bc)
    cola = jnp.concatenate([z, r, cm, cm] * 3 + [cm, z, z, z], axis=1)
    colb = jnp.concatenate([z, z, bc, bc] * 3 + [bc, z, z, z], axis=1)
    return cola, colb


def _spread_matrix():
    k = jnp.arange(LANES)
    grp, slot, head = k // _COL_GROUP, (k % _COL_GROUP) // M_HEADS, k % M_HEADS
    block = ((slot + 1) % 3) * M_HEADS + head
    valid = (grp < 3) & (slot >= 1)
    cols = jnp.arange(3 * M_HEADS * M_HEAD_DIM) // M_HEAD_DIM
    return ((block[:, None] == cols[None, :]) & valid[:, None]).astype(BF16)


def _mlstm(proj, rowg, cola, colb, norm_g, batch, seq, ts=1024):
    n = proj.shape[0]
    ts = min(ts, seq)
    nb = seq // ts
    blk = lambda col: pl.BlockSpec((ts, D_MLSTM), lambda b, i, col=col: (b * nb + i, col))
    return pl.pallas_call(
        functools.partial(_mlstm_kernel, ts=ts),
        grid=(batch, nb),
        in_specs=[
            blk(3), blk(4), blk(5), blk(6),
            pl.BlockSpec((ts // CHUNK, M_HEADS, CHUNK), lambda b, i: (b * nb + i, 0, 0)),
            pl.BlockSpec((ts, LANES), lambda b, i: (b * nb + i, 0)),
            pl.BlockSpec((ts, LANES), lambda b, i: (b * nb + i, 0)),
            pl.BlockSpec((LANES, 3 * M_HEADS * M_HEAD_DIM), lambda b, i: (0, 0),
                         pipeline_mode=pl.Buffered(1)),
            pl.BlockSpec((1, D_MLSTM), lambda b, i: (0, 0)),
        ],
        out_specs=pl.BlockSpec((ts, D_MLSTM), lambda b, i: (b * nb + i, 0)),
        out_shape=jax.ShapeDtypeStruct((n, D_MLSTM), BF16),
        scratch_shapes=[
            pltpu.VMEM((M_HEADS, M_HEAD_DIM, 2 * M_HEAD_DIM), F32),
            pltpu.VMEM((1, LANES), F32),
        ],
        compiler_params=_cparams(("parallel", "arbitrary")),
        name="mlstm",
    )(proj, proj, proj, proj, rowg, cola, colb, _spread_matrix(), norm_g)


def _mix_kernel(cb_ref, cc_ref, ch_ref, pc_ref, ph_ref, ym_ref, x_ref, g0_ref, b0_ref, cw_ref,
                cbias_ref, wo_ref, g_ref, b_ref, h1_ref, h1b_ref, *, blocks_per_seq):
    tm = cb_ref.shape[0]
    z = cc_ref[...].astype(F32) * ch_ref[...].astype(F32)
    halo = pc_ref[...].astype(F32) * ph_ref[...].astype(F32)
    first = (pl.program_id(0) % blocks_per_seq) == 0
    halo = jnp.where(first, 0.0, halo)
    row = lax.broadcasted_iota(jnp.int32, z.shape, 0)
    hr = halo.shape[0]
    z1 = jnp.where(row == 0, halo[hr - 1:hr, :], pltpu.roll(z, 1, axis=0))
    z2 = pltpu.roll(z, 2, axis=0)
    z2 = jnp.where(row == 0, halo[hr - 2:hr - 1, :], z2)
    z2 = jnp.where(row == 1, halo[hr - 1:hr, :], z2)
    conv = cbias_ref[...] + cw_ref[0:1, :] * z2 + cw_ref[1:2, :] * z1 + cw_ref[2:3, :] * z
    y_conv = (cb_ref[...].astype(F32) * conv).astype(BF16)
    mix = (jnp.dot(y_conv, wo_ref[0:D_CONV, :], preferred_element_type=F32)
           + jnp.dot(ym_ref[...], wo_ref[D_CONV:, :], preferred_element_type=F32))
    h = _layernorm(x_ref[...], g0_ref[...], b0_ref[...])
    h1 = _layernorm(ALPHA * h + mix, g_ref[...], b_ref[...])
    h1_ref[...] = h1
    h1b_ref[...] = h1.astype(BF16)


def _mix(proj, ym, x, g0, b0, conv_w, conv_b, w_out, g, b, seq, tm=512):
    n = proj.shape[0]
    tm = min(tm, seq)
    halo = 16
    hb = tm // halo
    cur = lambda col: pl.BlockSpec((tm, D_CONV), lambda i, col=col: (i, col))
    prev = lambda col: pl.BlockSpec((halo, D_CONV),
                                    lambda i, col=col: (jnp.maximum(i * hb - 1, 0), col))
    const = lambda shape: pl.BlockSpec(shape, lambda i: (0, 0))
    return pl.pallas_call(
        functools.partial(_mix_kernel, blocks_per_seq=seq // tm),
        grid=(n // tm,),
        in_specs=[
            cur(0), cur(1), cur(2), prev(1), prev(2),
            pl.BlockSpec((tm, D_MLSTM), lambda i: (i, 0)),
            pl.BlockSpec((tm, D_MODEL), lambda i: (i, 0)),
            const((1, D_MODEL)), const((1, D_MODEL)),
            const((8, D_CONV)), const((1, D_CONV)),
            pl.BlockSpec((D_MODEL, D_MODEL), lambda i: (0, 0), pipeline_mode=pl.Buffered(1)),
            const((1, D_MODEL)), const((1, D_MODEL)),
        ],
        out_specs=[pl.BlockSpec((tm, D_MODEL), lambda i: (i, 0)),
                   pl.BlockSpec((tm, D_MODEL), lambda i: (i, 0))],
        out_shape=[jax.ShapeDtypeStruct((n, D_MODEL), F32),
                   jax.ShapeDtypeStruct((n, D_MODEL), BF16)],
        compiler_params=_cparams(("parallel",)),
        name="mix",
    )(proj, proj, proj, proj, proj, ym, x, g0, b0, conv_w, conv_b, w_out, g, b)


def _qproj_kernel(x_ref, wqt_ref, qt_ref):
    qt_ref[...] = lax.dot_general(wqt_ref[...], x_ref[...], (((1,), (1,)), ((), ())),
                                  preferred_element_type=F32).astype(BF16)


def _qproj(h1b, wqt, tm=1024):
    n = h1b.shape[0]
    tm = min(tm, n)
    dq = wqt.shape[0]
    return pl.pallas_call(
        _qproj_kernel,
        grid=(n // tm,),
        in_specs=[pl.BlockSpec((tm, D_MODEL), lambda i: (i, 0)),
                  pl.BlockSpec((dq, D_MODEL), lambda i: (0, 0), pipeline_mode=pl.Buffered(1))],
        out_specs=pl.BlockSpec((dq, tm), lambda i: (0, i)),
        out_shape=jax.ShapeDtypeStruct((dq, n), BF16),
        compiler_params=_cparams(("parallel",)),
        name="peer_q",
    )(h1b, wqt)


def _oddeven_sort_pairs(n):
    pairs = []
    p = 1
    while p < n:
        k = p
        while k >= 1:
            for j in range(k % p, n - k, 2 * k):
                for i in range(min(k, n - j - k)):
                    if (i + j) // (2 * p) == (i + j + k) // (2 * p):
                        pairs.append((i + j, i + j + k))
            k //= 2
        p *= 2
    return pairs


_SORT16 = _oddeven_sort_pairs(PEER_TOPK)
_BITONIC16 = [(i, i + d) for d in (8, 4, 2, 1) for i in range(PEER_TOPK) if not i & d]


def _compare_exchange(v, ix, pairs):
    for i, j in pairs:
        up = v[j] > v[i]
        v[i], v[j] = jnp.maximum(v[i], v[j]), jnp.minimum(v[i], v[j])
        ix[i], ix[j] = jnp.where(up, ix[j], ix[i]), jnp.where(up, ix[i], ix[j])


def _network_top16(cur_scr, srt_v, srt_i, sv_scr, si_scr, p, lanes):
    K = PEER_TOPK
    for g in range(N_KEYS // K):
        v = [cur_scr[g * K + i, :, lanes] for i in range(K)]
        ix = [jnp.int32(g * K + i) for i in range(K)]
        _compare_exchange(v, ix, _SORT16)
        for i in range(K):
            srt_v[g * K + i] = v[i]
            srt_i[g * K + i] = ix[i]
    dropped = jnp.full(v[0].shape, -jnp.inf, F32)
    span = K
    while span < N_KEYS:
        for a in range(0, N_KEYS, 2 * span):
            b = a + span
            va = [srt_v[a + i] for i in range(K)]
            ia = [srt_i[a + i] for i in range(K)]
            v, ix = [], []
            for i in range(K):
                vb = srt_v[b + K - 1 - i]
                up = vb > va[i]
                v.append(jnp.maximum(va[i], vb))
                ix.append(jnp.where(up, srt_i[b + K - 1 - i], ia[i]))
                dropped = jnp.maximum(dropped, jnp.minimum(va[i], vb))
            _compare_exchange(v, ix, _BITONIC16)
            for i in range(K):
                srt_v[a + i] = v[i]
                srt_i[a + i] = ix[i]
        span *= 2
    distinct = v[K - 1] > dropped
    for i in range(K - 1):
        distinct = distinct & (v[i] > v[i + 1])
    for i in range(K):
        sv_scr[p, i, :, lanes] = v[i]
        si_scr[p, i, :, lanes] = ix[i]
    return jnp.where(distinct, 0, 1)


def _retrieve_kernel(q_ref, kb_ref, eid_ref, gate_ref, cur_scr, sv_scr, si_scr, srt_v, srt_i):
    tb = q_ref.shape[1]
    dh = PEER_HEADS * PEER_HALF

    for p in range(2):
        s = jnp.dot(kb_ref[p], q_ref[p * dh:(p + 1) * dh, :],
                    preferred_element_type=F32)
        cur_scr[...] = s.reshape(N_KEYS, PEER_HEADS, tb)

        def lane_body(lh, tied, p=p):
            lanes = pl.ds(pl.multiple_of(lh * LANES, LANES), LANES)
            return jnp.maximum(tied, _network_top16(cur_scr, srt_v, srt_i, sv_scr, si_scr, p, lanes))

        tied = lax.fori_loop(0, tb // LANES, lane_body, jnp.zeros((PEER_HEADS, LANES), jnp.int32))

        def round_body(k, taken, p=p):
            vals, ids = [], list(range(N_KEYS))
            for n in range(N_KEYS):
                c = jnp.where(taken == n, -jnp.inf, cur_scr[n])
                cur_scr[n] = c
                vals.append(c)
            while len(vals) > 1:
                right = [vals[i + 1] > vals[i] for i in range(0, len(vals), 2)]
                ids = [jnp.where(g, ids[2 * i + 1], ids[2 * i]) for i, g in enumerate(right)]
                vals = [jnp.maximum(vals[2 * i], vals[2 * i + 1]) for i in range(len(right))]
            sv_scr[p, k] = vals[0]
            si_scr[p, k] = ids[0]
            return ids[0]

        @pl.when(jnp.max(tied) > 0)
        def _(round_body=round_body):
            lax.fori_loop(0, PEER_TOPK, round_body, jnp.full((PEER_HEADS, tb), N_KEYS, jnp.int32))

    sv1 = sv_scr[0]
    sv2 = sv_scr[1]
    e1 = si_scr[0] * N_KEYS
    e2 = si_scr[1]
    a_iota = lax.broadcasted_iota(jnp.int32, (PEER_TOPK, PEER_HEADS, tb), 0)
    front = sv1 + sv2[0][None]
    ptr = jnp.zeros_like(a_iota)
    tops, eids = [], []
    for k in range(PEER_TOPK):
        mx = jnp.max(front, axis=0)
        awin = jnp.min(jnp.where(front == mx[None], a_iota, PEER_TOPK), axis=0)
        hit = a_iota == awin[None]
        bwin = jnp.sum(jnp.where(hit, ptr, 0), axis=0)
        eid = (jnp.sum(jnp.where(hit, e1, 0), axis=0)
               + jnp.sum(jnp.where(a_iota == bwin[None], e2, 0), axis=0))
        nxt = jnp.max(jnp.where(a_iota == (bwin + 1)[None], sv2, -jnp.inf), axis=0)
        front = jnp.where(hit, sv1 + nxt[None], front)
        ptr = jnp.where(hit, ptr + 1, ptr)
        tops.append(mx)
        eids.append(eid)
    top_s = jnp.stack(tops)
    e = jnp.exp(top_s - tops[0][None])
    gate = e / jnp.sum(e, axis=0)[None]
    nj = PEER_TOPK * PEER_HEADS
    eid_ref[...] = jnp.stack(eids).reshape(nj, tb).T
    gate_ref[...] = gate.reshape(nj, tb).T


def _retrieve(q, kb, tb=256):
    n = q.shape[1]
    tb = min(tb, n)
    nj = PEER_HEADS * PEER_TOPK
    dq = 2 * PEER_HEADS * PEER_HALF
    return pl.pallas_call(
        _retrieve_kernel,
        grid=(n // tb,),
        in_specs=[
            pl.BlockSpec((dq, tb), lambda i: (0, i)),
            pl.BlockSpec((2, N_KEYS * PEER_HEADS, dq // 2), lambda i: (0, 0, 0),
                         pipeline_mode=pl.Buffered(1)),
        ],
        out_specs=[pl.BlockSpec((tb, nj), lambda i: (i, 0)),
                   pl.BlockSpec((tb, nj), lambda i: (i, 0))],
        out_shape=[jax.ShapeDtypeStruct((n, nj), jnp.int32),
                   jax.ShapeDtypeStruct((n, nj), F32)],
        scratch_shapes=[
            pltpu.VMEM((N_KEYS, PEER_HEADS, tb), F32),
            pltpu.VMEM((2, PEER_TOPK, PEER_HEADS, tb), F32),
            pltpu.VMEM((2, PEER_TOPK, PEER_HEADS, tb), jnp.int32),
            pltpu.VMEM((N_KEYS, PEER_HEADS, LANES), F32),
            pltpu.VMEM((N_KEYS, PEER_HEADS, LANES), jnp.int32),
        ],
        compiler_params=_cparams(("parallel",)),
        name="retrieve",
    )(q, kb)


def _block_diag_keys(keys):
    eye_h = jnp.eye(PEER_HEADS, dtype=keys.dtype)
    kb = jnp.einsum("hpnc,hg->pnhgc", keys, eye_h)
    return kb.reshape(2, N_KEYS * PEER_HEADS, PEER_HEADS * PEER_HALF)


def _half_major_query_weight(wq):
    d = wq.shape[0]
    return wq.reshape(d, PEER_HEADS, 2, PEER_HALF).transpose(2, 1, 3, 0).reshape(-1, d)


_W_GROUP = 16
_W_PITCH = _W_GROUP + 4


def _scatter_w_kernel(eid_ref, gate_ref, w_ref, scr_a, scr_b):
    tb = eid_ref.shape[0]
    nj = eid_ref.shape[1]
    ngrp = tb // _W_GROUP
    sub = lax.broadcasted_iota(jnp.int32, (N_KEYS, nj), 0)

    def build(grp, scr):
        base = pl.multiple_of(grp * _W_GROUP, _W_GROUP)
        for tl in range(_W_GROUP):
            e = eid_ref[pl.ds(base + tl, 1), :]
            g = gate_ref[pl.ds(base + tl, 1), :]
            p = jnp.where(sub == (e >> 7), 1.0, 0.0).astype(BF16)
            q = jnp.where(sub == (e & (N_KEYS - 1)), g, 0.0).astype(BF16)
            wt = lax.dot_general(p, q, (((1,), (1,)), ((), ())), preferred_element_type=F32)
            scr[pl.ds(tl, N_KEYS, stride=_W_PITCH), :] = wt

    def emit(grp, scr):
        base = pl.multiple_of(grp * _W_GROUP, _W_GROUP)
        for r in range(N_KEYS):
            tile = scr[r * _W_PITCH:r * _W_PITCH + _W_GROUP, :]
            w_ref[pl.ds(base, _W_GROUP), r * N_KEYS:(r + 1) * N_KEYS] = tile.astype(BF16)

    build(0, scr_a)

    def pair_body(i, carry):
        build(2 * i + 1, scr_b)
        emit(2 * i, scr_a)
        build(2 * i + 2, scr_a)
        emit(2 * i + 1, scr_b)
        return carry

    lax.fori_loop(0, ngrp // 2 - 1, pair_body, 0)
    build(ngrp - 1, scr_b)
    emit(ngrp - 2, scr_a)
    emit(ngrp - 1, scr_b)


def _scatter_w(eid, gate, tb=128):
    n, nj = eid.shape
    tb = min(tb, n)
    assert tb % (2 * _W_GROUP) == 0
    return pl.pallas_call(
        _scatter_w_kernel,
        grid=(n // tb,),
        in_specs=[pl.BlockSpec((tb, nj), lambda i: (i, 0)),
                  pl.BlockSpec((tb, nj), lambda i: (i, 0))],
        out_specs=pl.BlockSpec((tb, N_EXPERTS), lambda i: (i, 0)),
        out_shape=jax.ShapeDtypeStruct((n, N_EXPERTS), BF16),
        scratch_shapes=[pltpu.VMEM((N_KEYS * _W_PITCH, N_KEYS), F32)] * 2,
        compiler_params=_cparams(("parallel",)),
        name="scatter_w",
    )(eid, gate)


def _experts_kernel(xb_ref, u_ref, v_ref, w_ref, h1_ref, g_ref, b_ref, out_ref):
    k = pl.program_id(1)

    @pl.when(k == 0)
    def _():
        out_ref[...] = ALPHA * h1_ref[...]

    a = lax.dot_general(xb_ref[...], u_ref[...], (((1,), (1,)), ((), ())),
                        preferred_element_type=F32)
    act = 0.5 * a * (1.0 + lax.erf(a * (0.5 ** 0.5)))
    wa = (w_ref[...].astype(F32) * act).astype(BF16)
    out_ref[...] += jnp.dot(wa, v_ref[...], preferred_element_type=F32)

    @pl.when(k == pl.num_programs(1) - 1)
    def _():
        out_ref[...] = _layernorm(out_ref[...], g_ref[...], b_ref[...])


def _experts(h1b, u, v, w, h1, g, b, tm=1024, ec=1024):
    n = h1b.shape[0]
    tm = min(tm, n)
    once = pl.Buffered(1)
    return pl.pallas_call(
        _experts_kernel,
        grid=(n // tm, N_EXPERTS // ec),
        in_specs=[
            pl.BlockSpec((tm, D_MODEL), lambda i, k: (i, 0), pipeline_mode=once),
            pl.BlockSpec((ec, D_MODEL), lambda i, k: (k, 0)),
            pl.BlockSpec((ec, D_MODEL), lambda i, k: (k, 0)),
            pl.BlockSpec((tm, ec), lambda i, k: (i, k)),
            pl.BlockSpec((tm, D_MODEL), lambda i, k: (i, 0), pipeline_mode=once),
            pl.BlockSpec((1, D_MODEL), lambda i, k: (0, 0)),
            pl.BlockSpec((1, D_MODEL), lambda i, k: (0, 0)),
        ],
        out_specs=pl.BlockSpec((tm, D_MODEL), lambda i, k: (i, 0), pipeline_mode=once),
        out_shape=jax.ShapeDtypeStruct((n, D_MODEL), F32),
        compiler_params=_cparams(("parallel", "arbitrary")),
        name="experts",
    )(h1b, u, v, w, h1, g, b)


def kernel(x, ln_in_g, ln_in_b, w_in, b_gate, conv_w, conv_b, mh_norm_g, w_out, ln1_g, ln1_b,
           peer_wq, peer_keys, peer_u, peer_v, ln2_g, ln2_b):
    batch, seq, d = x.shape
    n = batch * seq
    row = lambda t: t.reshape(1, -1).astype(F32)
    l = 0
    w_main = w_in[l, :, :D_MAIN].astype(BF16)
    w_gate = jnp.pad(w_in[l, :, D_MAIN:], ((0, 0), (0, LANES - 2 * M_HEADS))).astype(BF16)

    x2 = x.reshape(n, d)
    proj, gates = _ln_proj(x2, row(ln_in_g), row(ln_in_b), w_main, w_gate, tm=_TILES["proj_tm"])

    scans = _gate_scan(gates[:, :2 * M_HEADS].T, b_gate[l].reshape(2 * M_HEADS, 1),
                       tl=_TILES["scan_tl"])
    rowg = scans[M_HEADS:2 * M_HEADS].reshape(M_HEADS, n // CHUNK, CHUNK).transpose(1, 0, 2)
    cola, colb = _gate_columns(scans)
    ym = _mlstm(proj, rowg, cola, colb, row(mh_norm_g[l]), batch, seq, ts=_TILES["mlstm_ts"])

    cw = jnp.pad(conv_w[l], ((0, 8 - conv_w.shape[1]), (0, 0)))
    h1, h1b = _mix(proj, ym, x2, row(ln_in_g), row(ln_in_b), cw, row(conv_b[l]), w_out[l].astype(BF16),
                   row(ln1_g[l]), row(ln1_b[l]), seq, tm=_TILES["mix_tm"])

    kb = _block_diag_keys(peer_keys[l]).astype(BF16)
    q = _qproj(h1b, _half_major_query_weight(peer_wq[l]).astype(BF16))
    eid, gate = _retrieve(q, kb, tb=_TILES["retrieve_tb"])
    w = _scatter_w(eid, gate, tb=_TILES["scatter_tb"])
    out = _experts(h1b, peer_u[l].astype(BF16), peer_v[l].astype(BF16), w, h1,
                   row(ln2_g[l]), row(ln2_b[l]), tm=_TILES["experts_tm"], ec=_TILES["experts_ec"])
    return out.reshape(batch, seq, d)
```

```python
import functools
import math

import jax
import jax.numpy as jnp
from jax import lax
from jax.experimental import pallas as pl
from jax.experimental.pallas import tpu as pltpu

F32 = jnp.float32
BF16 = jnp.bfloat16

D_MODEL = 2048
CHUNK = 64
D_CONV = 1024
D_MLSTM = 1024
M_HEADS = 8
M_HEAD_DIM = 128
D_MAIN = 3 * D_CONV + 4 * D_MLSTM
PEER_HEADS = 8
N_KEYS = 128
N_EXPERTS = N_KEYS * N_KEYS
PEER_TOPK = 16
PEER_HALF = 128
DEPTH = 1
ALPHA = (2 * DEPTH) ** 0.25
LN_EPS = 1e-5
LANES = 128
VMEM_LIMIT = 56 * 1024 * 1024

_TILES = dict(proj_tm=1024, scan_tl=2048, mlstm_ts=1024, mix_tm=512, retrieve_tb=512,
              scatter_tb=128, experts_tm=1024, experts_ec=1024)


def _cparams(sem):
    return pltpu.CompilerParams(dimension_semantics=sem, vmem_limit_bytes=VMEM_LIMIT)


def _layernorm(x, g, b):
    mu = jnp.mean(x, axis=-1, keepdims=True)
    xc = x - mu
    var = jnp.mean(xc * xc, axis=-1, keepdims=True)
    return xc * lax.rsqrt(var + LN_EPS) * g + b


_LN_ROWS = 256


def _ln_proj_kernel(x_ref, g_ref, b_ref, w_ref, wg_ref, proj_ref, gate_ref, hb_ref):
    @pl.when(pl.program_id(1) == 0)
    def _():
        def rows_body(r, carry):
            rows = pl.ds(pl.multiple_of(r * _LN_ROWS, _LN_ROWS), _LN_ROWS)
            hb = _layernorm(x_ref[rows, :], g_ref[...], b_ref[...]).astype(BF16)
            hb_ref[rows, :] = hb
            gate_ref[:, rows] = lax.dot_general(wg_ref[...], hb, (((1,), (1,)), ((), ())),
                                                preferred_element_type=F32)
            return carry

        lax.fori_loop(0, x_ref.shape[0] // _LN_ROWS, rows_body, 0)

    proj_ref[...] = jnp.dot(hb_ref[...], w_ref[...], preferred_element_type=F32).astype(BF16)


def _ln_proj(x, g, b, w_main, w_gate, tm=1024, tn=1792):
    n = x.shape[0]
    tm = min(tm, n)
    return pl.pallas_call(
        _ln_proj_kernel,
        grid=(n // tm, D_MAIN // tn),
        in_specs=[
            pl.BlockSpec((tm, D_MODEL), lambda i, j: (i, 0)),
            pl.BlockSpec((1, D_MODEL), lambda i, j: (0, 0)),
            pl.BlockSpec((1, D_MODEL), lambda i, j: (0, 0)),
            pl.BlockSpec((D_MODEL, tn), lambda i, j: (0, j)),
            pl.BlockSpec((LANES, D_MODEL), lambda i, j: (0, 0)),
        ],
        out_specs=[
            pl.BlockSpec((tm, tn), lambda i, j: (i, j)),
            pl.BlockSpec((LANES, tm), lambda i, j: (0, i)),
        ],
        out_shape=[
            jax.ShapeDtypeStruct((n, D_MAIN), BF16),
            jax.ShapeDtypeStruct((LANES, n), F32),
        ],
        scratch_shapes=[pltpu.VMEM((tm, D_MODEL), BF16)],
        compiler_params=_cparams(("parallel", "arbitrary")),
        name="ln_proj",
    )(x, g, b, w_main, w_gate)


def _gate_scan_kernel(gt_ref, b_ref, r_ref, cola_ref, colb_ref):
    z = gt_ref[...] + b_ref[...]
    li = z[:M_HEADS]
    lf = jax.nn.log_sigmoid(z[M_HEADS:])
    pos = lax.broadcasted_iota(jnp.int32, lf.shape, 1) % CHUNK
    bcum = lf
    shift = 1
    while shift < CHUNK:
        bcum = bcum + jnp.where(pos >= shift, pltpu.roll(bcum, shift, axis=1), 0.0)
        shift *= 2
    r = li - bcum
    cm = r
    shift = 1
    while shift < CHUNK:
        cm = jnp.maximum(cm, jnp.where(pos >= shift, pltpu.roll(cm, shift, axis=1), -jnp.inf))
        shift *= 2
    r_ref[...] = r
    zero = jnp.zeros_like(r)
    cola_ref[...] = jnp.concatenate([zero, r, cm, cm] * 3 + [cm, zero, zero, zero], axis=0).T
    colb_ref[...] = jnp.concatenate([zero, zero, bcum, bcum] * 3 + [bcum, zero, zero, zero], axis=0).T


def _gate_scan(gt, bias, tl=2048):
    n = gt.shape[1]
    tl = min(tl, n)
    return pl.pallas_call(
        _gate_scan_kernel,
        grid=(n // tl,),
        in_specs=[pl.BlockSpec((2 * M_HEADS, tl), lambda i: (0, i)),
                  pl.BlockSpec((2 * M_HEADS, 1), lambda i: (0, 0))],
        out_specs=[pl.BlockSpec((M_HEADS, tl), lambda i: (0, i)),
                   pl.BlockSpec((tl, LANES), lambda i: (i, 0)),
                   pl.BlockSpec((tl, LANES), lambda i: (i, 0))],
        out_shape=[jax.ShapeDtypeStruct((M_HEADS, n), F32),
                   jax.ShapeDtypeStruct((n, LANES), F32),
                   jax.ShapeDtypeStruct((n, LANES), F32)],
        compiler_params=_cparams(("parallel",)),
        name="gate_scan",
    )(gt, bias)


def _mlstm_kernel(q_ref, k_ref, v_ref, o_ref, row_ref, cola_ref, colb_ref, spread_ref, ng_ref,
                  y_ref, cn_ref, m_ref, *, ts):
    @pl.when(pl.program_id(1) == 0)
    def _():
        cn_ref[...] = jnp.zeros_like(cn_ref)
        m_ref[...] = jnp.zeros_like(m_ref)

    L = CHUNK
    dh = M_HEAD_DIM
    scale = dh ** -0.5
    ones_v = jnp.ones((M_HEADS, L, dh), BF16)
    lane_mean = jnp.full((M_HEADS, dh, dh), 1.0 / dh, BF16)
    li = lax.broadcasted_iota(jnp.int32, (L, L), 0)
    lj = lax.broadcasted_iota(jnp.int32, (L, L), 1)
    causal = lj <= li
    lane = lax.broadcasted_iota(jnp.int32, (L, LANES), 1)
    lane_grp = lane // _COL_GROUP
    lane_in = lane % _COL_GROUP
    is_wk = (lane_grp < 3) & (lane_in >= M_HEADS) & (lane_in < 2 * M_HEADS)
    is_a = (lane_grp < 3) & (lane_in >= 2 * M_HEADS) & (lane_in < 3 * M_HEADS)
    is_wi = (lane_grp < 3) & (lane_in >= 3 * M_HEADS)

    def chunk_body(c, carry):
        t0 = pl.multiple_of(c * L, L)
        rows = pl.ds(t0, L)
        rowg = row_ref[c]
        ca = cola_ref[rows, :]
        cb = colb_ref[rows, :]
        m_row = m_ref[...]
        a_l = jnp.maximum(m_row, ca)
        w_inter_l = jnp.exp(m_row - a_l)
        floor_l = jnp.exp(-(cb + a_l))
        a_last = a_l[L - 1:L, :]
        wk_l = jnp.exp(ca - pltpu.roll(a_last, LANES - M_HEADS, axis=1))
        decay_row = jnp.exp(m_row - a_last)
        m_ref[...] = cb[L - 1:L, :] + a_last
        heads = range(M_HEADS)
        per_head = lambda ref: jnp.stack([ref[rows, h * dh:(h + 1) * dh] for h in heads])
        bmm = lambda x, y, cx, cy: lax.dot_general(
            x, y, (((cx,), (cy,)), ((0,), (0,))), preferred_element_type=F32)
        x = jnp.where(is_wk, wk_l, jnp.where(is_a, a_l, jnp.where(is_wi, w_inter_l, 0.0)))
        x_hi = x.astype(BF16)
        x_r1 = x - x_hi.astype(F32)
        x_mid = x_r1.astype(BF16)
        x_lo = (x_r1 - x_mid.astype(F32)).astype(BF16)
        parts = jnp.where(lane_grp == 0, x_hi, jnp.where(lane_grp == 1, x_mid, x_lo))
        rep = jnp.dot(parts, spread_ref[...], preferred_element_type=F32)
        wide = lambda qi: jnp.stack([rep[:, (qi * M_HEADS + h) * dh:(qi * M_HEADS + h + 1) * dh]
                                     for h in heads])
        a = wide(0)[..., :L]
        w_inter = wide(1)
        wk_wide = wide(2)
        floor = jnp.stack([floor_l[:, 3 * 32 + h:3 * 32 + h + 1] for h in heads])
        decay = jnp.stack([decay_row[:, 2 * M_HEADS + h:2 * M_HEADS + h + 1] for h in heads])
        q = per_head(q_ref)
        k = per_head(k_ref)
        v = per_head(v_ref)
        w_intra = jnp.exp(jnp.where(causal[None], rowg[:, None, :] - a, -jnp.inf))
        s = bmm(q, k, 2, 2) * scale * w_intra
        s_hi = s.astype(BF16)
        s_lo = (s - s_hi.astype(F32)).astype(BF16)
        sv = bmm(s_hi, jnp.concatenate([v, ones_v], axis=-1), 2, 1)
        cn = cn_ref[...]
        qc = bmm(q, cn.astype(BF16), 2, 1)
        num = sv[..., :dh] + w_inter * qc[..., :dh]
        den = sv[..., dh:] + bmm(s_lo, ones_v, 2, 1) + w_inter * qc[..., dh:]
        hval = num / jnp.maximum(jnp.abs(den), floor)
        wkv =jnp.concatenate([wk_wide * v.astype(F32), wk_wide], axis=-1).astype(BF16)
        cn_ref[...] = decay * cn + scale * bmm(k, wkv, 1, 1)
        mu = bmm(hval.astype(BF16), lane_mean, 2, 1)
        hc = hval - mu
        var = bmm((hc * hc).astype(BF16), lane_mean, 2, 1)
        hn = hc * lax.rsqrt(var + LN_EPS)
        for h in heads:
            hs = slice(h * dh, (h + 1) * dh)
            og = jax.nn.sigmoid(o_ref[rows, hs].astype(F32))
            y_ref[rows, hs] = (og * hn[h] * ng_ref[:, hs]).astype(BF16)
        return carry

    lax.fori_loop(0, ts // L, chunk_body, 0, unroll=2)


_COL_GROUP = 32


def _spread_matrix():
    k = jnp.arange(LANES)
    grp, slot, head = k // _COL_GROUP, (k % _COL_GROUP) // M_HEADS, k % M_HEADS
    block = ((slot + 1) % 3) * M_HEADS + head
    valid = (grp < 3) & (slot >= 1)
    cols = jnp.arange(3 * M_HEADS * M_HEAD_DIM) // M_HEAD_DIM
    return ((block[:, None] == cols[None, :]) & valid[:, None]).astype(BF16)


def _mlstm(proj, rowg, cola, colb, norm_g, batch, seq, ts=1024):
    n = proj.shape[0]
    ts = min(ts, seq)
    nb = seq // ts
    blk = lambda col: pl.BlockSpec((ts, D_MLSTM), lambda b, i, col=col: (b * nb + i, col))
    return pl.pallas_call(
        functools.partial(_mlstm_kernel, ts=ts),
        grid=(batch, nb),
        in_specs=[
            blk(3), blk(4), blk(5), blk(6),
            pl.BlockSpec((ts // CHUNK, M_HEADS, CHUNK), lambda b, i: (b * nb + i, 0, 0)),
            pl.BlockSpec((ts, LANES), lambda b, i: (b * nb + i, 0)),
            pl.BlockSpec((ts, LANES), lambda b, i: (b * nb + i, 0)),
            pl.BlockSpec((LANES, 3 * M_HEADS * M_HEAD_DIM), lambda b, i: (0, 0),
                         pipeline_mode=pl.Buffered(1)),
            pl.BlockSpec((1, D_MLSTM), lambda b, i: (0, 0)),
        ],
        out_specs=pl.BlockSpec((ts, D_MLSTM), lambda b, i: (b * nb + i, 0)),
        out_shape=jax.ShapeDtypeStruct((n, D_MLSTM), BF16),
        scratch_shapes=[
            pltpu.VMEM((M_HEADS, M_HEAD_DIM, 2 * M_HEAD_DIM), F32),
            pltpu.VMEM((1, LANES), F32),
        ],
        compiler_params=_cparams(("parallel", "arbitrary")),
        name="mlstm",
    )(proj, proj, proj, proj, rowg, cola, colb, _spread_matrix(), norm_g)


def _mix_kernel(cb_ref, cc_ref, ch_ref, pc_ref, ph_ref, ym_ref, x_ref, g0_ref, b0_ref, cw_ref,
                cbias_ref, wo_ref, g_ref, b_ref, h1_ref, h1b_ref, *, blocks_per_seq):
    tm = cb_ref.shape[0]
    z = cc_ref[...].astype(F32) * ch_ref[...].astype(F32)
    halo = pc_ref[...].astype(F32) * ph_ref[...].astype(F32)
    first = (pl.program_id(0) % blocks_per_seq) == 0
    halo = jnp.where(first, 0.0, halo)
    row = lax.broadcasted_iota(jnp.int32, z.shape, 0)
    hr = halo.shape[0]
    z1 = jnp.where(row == 0, halo[hr - 1:hr, :], pltpu.roll(z, 1, axis=0))
    z2 = pltpu.roll(z, 2, axis=0)
    z2 = jnp.where(row == 0, halo[hr - 2:hr - 1, :], z2)
    z2 = jnp.where(row == 1, halo[hr - 1:hr, :], z2)
    conv = cbias_ref[...] + cw_ref[0:1, :] * z2 + cw_ref[1:2, :] * z1 + cw_ref[2:3, :] * z
    y_conv = (cb_ref[...].astype(F32) * conv).astype(BF16)
    mix = (jnp.dot(y_conv, wo_ref[0:D_CONV, :], preferred_element_type=F32)
           + jnp.dot(ym_ref[...], wo_ref[D_CONV:, :], preferred_element_type=F32))
    h = _layernorm(x_ref[...], g0_ref[...], b0_ref[...])
    h1 = _layernorm(ALPHA * h + mix, g_ref[...], b_ref[...])
    h1_ref[...] = h1
    h1b_ref[...] = h1.astype(BF16)


def _mix(proj, ym, x, g0, b0, conv_w, conv_b, w_out, g, b, seq, tm=512):
    n = proj.shape[0]
    tm = min(tm, seq)
    halo = 16
    hb = tm // halo
    cur = lambda col: pl.BlockSpec((tm, D_CONV), lambda i, col=col: (i, col))
    prev = lambda col: pl.BlockSpec((halo, D_CONV),
                                    lambda i, col=col: (jnp.maximum(i * hb - 1, 0), col))
    const = lambda shape: pl.BlockSpec(shape, lambda i: (0, 0))
    return pl.pallas_call(
        functools.partial(_mix_kernel, blocks_per_seq=seq // tm),
        grid=(n // tm,),
        in_specs=[
            cur(0), cur(1), cur(2), prev(1), prev(2),
            pl.BlockSpec((tm, D_MLSTM), lambda i: (i, 0)),
            pl.BlockSpec((tm, D_MODEL), lambda i: (i, 0)),
            const((1, D_MODEL)), const((1, D_MODEL)),
            const((8, D_CONV)), const((1, D_CONV)),
            pl.BlockSpec((D_MODEL, D_MODEL), lambda i: (0, 0), pipeline_mode=pl.Buffered(1)),
            const((1, D_MODEL)), const((1, D_MODEL)),
        ],
        out_specs=[pl.BlockSpec((tm, D_MODEL), lambda i: (i, 0)),
                   pl.BlockSpec((tm, D_MODEL), lambda i: (i, 0))],
        out_shape=[jax.ShapeDtypeStruct((n, D_MODEL), F32),
                   jax.ShapeDtypeStruct((n, D_MODEL), BF16)],
        compiler_params=_cparams(("parallel",)),
        name="mix",
    )(proj, proj, proj, proj, proj, ym, x, g0, b0, conv_w, conv_b, w_out, g, b)


def _qproj_kernel(x_ref, wqt_ref, qt_ref):
    qt_ref[...] = lax.dot_general(wqt_ref[...], x_ref[...], (((1,), (1,)), ((), ())),
                                  preferred_element_type=F32).astype(BF16)


def _qproj(h1b, wqt, tm=1024):
    n = h1b.shape[0]
    tm = min(tm, n)
    dq = wqt.shape[0]
    return pl.pallas_call(
        _qproj_kernel,
        grid=(n // tm,),
        in_specs=[pl.BlockSpec((tm, D_MODEL), lambda i: (i, 0)),
                  pl.BlockSpec((dq, D_MODEL), lambda i: (0, 0), pipeline_mode=pl.Buffered(1))],
        out_specs=pl.BlockSpec((dq, tm), lambda i: (0, i)),
        out_shape=jax.ShapeDtypeStruct((dq, n), BF16),
        compiler_params=_cparams(("parallel",)),
        name="peer_q",
    )(h1b, wqt)


def _oddeven_sort_pairs(n):
    pairs = []
    p = 1
    while p < n:
        k = p
        while k >= 1:
            for j in range(k % p, n - k, 2 * k):
                for i in range(min(k, n - j - k)):
                    if (i + j) // (2 * p) == (i + j + k) // (2 * p):
                        pairs.append((i + j, i + j + k))
            k //= 2
        p *= 2
    return pairs


_SORT16 = _oddeven_sort_pairs(PEER_TOPK)
_BITONIC16 = [(i, i + d) for d in (8, 4, 2, 1) for i in range(PEER_TOPK) if not i & d]


def _compare_exchange(v, ix, pairs):
    for i, j in pairs:
        up = v[j] > v[i]
        v[i], v[j] = jnp.maximum(v[i], v[j]), jnp.minimum(v[i], v[j])
        ix[i], ix[j] = jnp.where(up, ix[j], ix[i]), jnp.where(up, ix[i], ix[j])


def _network_top16(cur_scr, srt_v, srt_i, sv_scr, si_scr, p, lanes):
    K = PEER_TOPK
    for g in range(N_KEYS // K):
        v = [cur_scr[g * K + i, :, lanes] for i in range(K)]
        ix = [jnp.int32(g * K + i) for i in range(K)]
        _compare_exchange(v, ix, _SORT16)
        for i in range(K):
            srt_v[g * K + i] = v[i]
            srt_i[g * K + i] = ix[i]
    dropped = jnp.full(v[0].shape, -jnp.inf, F32)
    span = K
    while span < N_KEYS:
        for a in range(0, N_KEYS, 2 * span):
            b = a + span
            va = [srt_v[a + i] for i in range(K)]
            ia = [srt_i[a + i] for i in range(K)]
            v, ix = [], []
            for i in range(K):
                vb = srt_v[b + K - 1 - i]
                up = vb > va[i]
                v.append(jnp.maximum(va[i], vb))
                ix.append(jnp.where(up, srt_i[b + K - 1 - i], ia[i]))
                dropped = jnp.maximum(dropped, jnp.minimum(va[i], vb))
            _compare_exchange(v, ix, _BITONIC16)
            for i in range(K):
                srt_v[a + i] = v[i]
                srt_i[a + i] = ix[i]
        span *= 2
    distinct = v[K - 1] > dropped
    for i in range(K - 1):
        distinct = distinct & (v[i] > v[i + 1])
    for i in range(K):
        sv_scr[p, i, :, lanes] = v[i]
        si_scr[p, i, :, lanes] = ix[i]
    return jnp.where(distinct, 0, 1)


def _retrieve_kernel(q_ref, kb_ref, eid_ref, gate_ref, cur_scr, sv_scr, si_scr, srt_v, srt_i):
    tb = q_ref.shape[1]
    dh = PEER_HEADS * PEER_HALF

    for p in range(2):
        s = jnp.dot(kb_ref[p], q_ref[p * dh:(p + 1) * dh, :],
                    preferred_element_type=F32)
        cur_scr[...] = s.reshape(N_KEYS, PEER_HEADS, tb)

        def lane_body(lh, tied, p=p):
            lanes = pl.ds(pl.multiple_of(lh * LANES, LANES), LANES)
            return jnp.maximum(tied, _network_top16(cur_scr, srt_v, srt_i, sv_scr, si_scr, p, lanes))

        tied = lax.fori_loop(0, tb // LANES, lane_body, jnp.zeros((PEER_HEADS, LANES), jnp.int32))

        def round_body(k, taken, p=p):
            vals, ids = [], list(range(N_KEYS))
            for n in range(N_KEYS):
                c = jnp.where(taken == n, -jnp.inf, cur_scr[n])
                cur_scr[n] = c
                vals.append(c)
            while len(vals) > 1:
                right = [vals[i + 1] > vals[i] for i in range(0, len(vals), 2)]
                ids = [jnp.where(g, ids[2 * i + 1], ids[2 * i]) for i, g in enumerate(right)]
                vals = [jnp.maximum(vals[2 * i], vals[2 * i + 1]) for i in range(len(right))]
            sv_scr[p, k] = vals[0]
            si_scr[p, k] = ids[0]
            return ids[0]

        @pl.when(jnp.max(tied) > 0)
        def _(round_body=round_body):
            lax.fori_loop(0, PEER_TOPK, round_body, jnp.full((PEER_HEADS, tb), N_KEYS, jnp.int32))

    sv1 = sv_scr[0]
    sv2 = sv_scr[1]
    e1 = si_scr[0] * N_KEYS
    e2 = si_scr[1]
    a_iota = lax.broadcasted_iota(jnp.int32, (PEER_TOPK, PEER_HEADS, tb), 0)
    front = sv1 + sv2[0][None]
    ptr = jnp.zeros_like(a_iota)
    tops, eids = [], []
    for k in range(PEER_TOPK):
        mx = jnp.max(front, axis=0)
        awin = jnp.min(jnp.where(front == mx[None], a_iota, PEER_TOPK), axis=0)
        hit = a_iota == awin[None]
        bwin = jnp.sum(jnp.where(hit, ptr, 0), axis=0)
        eid = (jnp.sum(jnp.where(hit, e1, 0), axis=0)
               + jnp.sum(jnp.where(a_iota == bwin[None], e2, 0), axis=0))
        nxt = jnp.max(jnp.where(a_iota == (bwin + 1)[None], sv2, -jnp.inf), axis=0)
        front = jnp.where(hit, sv1 + nxt[None], front)
        ptr = jnp.where(hit, ptr + 1, ptr)
        tops.append(mx)
        eids.append(eid)
    top_s = jnp.stack(tops)
    e = jnp.exp(top_s - tops[0][None])
    gate = e / jnp.sum(e, axis=0)[None]
    nj = PEER_TOPK * PEER_HEADS
    eid_ref[...] = jnp.stack(eids).reshape(nj, tb).T
    gate_ref[...] = gate.reshape(nj, tb).T


def _retrieve(q, kb, tb=256):
    n = q.shape[1]
    tb = min(tb, n)
    nj = PEER_HEADS * PEER_TOPK
    dq = 2 * PEER_HEADS * PEER_HALF
    return pl.pallas_call(
        _retrieve_kernel,
        grid=(n // tb,),
        in_specs=[
            pl.BlockSpec((dq, tb), lambda i: (0, i)),
            pl.BlockSpec((2, N_KEYS * PEER_HEADS, dq // 2), lambda i: (0, 0, 0),
                         pipeline_mode=pl.Buffered(1)),
        ],
        out_specs=[pl.BlockSpec((tb, nj), lambda i: (i, 0)),
                   pl.BlockSpec((tb, nj), lambda i: (i, 0))],
        out_shape=[jax.ShapeDtypeStruct((n, nj), jnp.int32),
                   jax.ShapeDtypeStruct((n, nj), F32)],
        scratch_shapes=[
            pltpu.VMEM((N_KEYS, PEER_HEADS, tb), F32),
            pltpu.VMEM((2, PEER_TOPK, PEER_HEADS, tb), F32),
            pltpu.VMEM((2, PEER_TOPK, PEER_HEADS, tb), jnp.int32),
            pltpu.VMEM((N_KEYS, PEER_HEADS, LANES), F32),
            pltpu.VMEM((N_KEYS, PEER_HEADS, LANES), jnp.int32),
        ],
        compiler_params=_cparams(("parallel",)),
        name="retrieve",
    )(q, kb)


def _block_diag_keys(keys):
    eye_h = jnp.eye(PEER_HEADS, dtype=keys.dtype)
    kb = jnp.einsum("hpnc,hg->pnhgc", keys, eye_h)
    return kb.reshape(2, N_KEYS * PEER_HEADS, PEER_HEADS * PEER_HALF)


def _half_major_query_weight(wq):
    d = wq.shape[0]
    return wq.reshape(d, PEER_HEADS, 2, PEER_HALF).transpose(2, 1, 3, 0).reshape(-1, d)


_W_GROUP = 16
_W_PITCH = _W_GROUP + 4


def _scatter_w_kernel(eid_ref, gate_ref, w_ref, scr_a, scr_b):
    tb = eid_ref.shape[0]
    nj = eid_ref.shape[1]
    ngrp = tb // _W_GROUP
    sub = lax.broadcasted_iota(jnp.int32, (N_KEYS, nj), 0)

    def build(grp, scr):
        base = pl.multiple_of(grp * _W_GROUP, _W_GROUP)
        for tl in range(_W_GROUP):
            e = eid_ref[pl.ds(base + tl, 1), :]
            g = gate_ref[pl.ds(base + tl, 1), :]
            p = jnp.where(sub == (e >> 7), 1.0, 0.0).astype(BF16)
            q = jnp.where(sub == (e & (N_KEYS - 1)), g, 0.0).astype(BF16)
            wt = lax.dot_general(p, q, (((1,), (1,)), ((), ())), preferred_element_type=F32)
            scr[pl.ds(tl, N_KEYS, stride=_W_PITCH), :] = wt

    def emit(grp, scr):
        base = pl.multiple_of(grp * _W_GROUP, _W_GROUP)
        for r in range(N_KEYS):
            tile = scr[r * _W_PITCH:r * _W_PITCH + _W_GROUP, :]
            w_ref[pl.ds(base, _W_GROUP), r * N_KEYS:(r + 1) * N_KEYS] = tile.astype(BF16)

    build(0, scr_a)

    def pair_body(i, carry):
        build(2 * i + 1, scr_b)
        emit(2 * i, scr_a)
        build(2 * i + 2, scr_a)
        emit(2 * i + 1, scr_b)
        return carry

    lax.fori_loop(0, ngrp // 2 - 1, pair_body, 0)
    build(ngrp - 1, scr_b)
    emit(ngrp - 2, scr_a)
    emit(ngrp - 1, scr_b)


def _scatter_w(eid, gate, tb=128):
    n, nj = eid.shape
    tb = min(tb, n)
    assert tb % (2 * _W_GROUP) == 0
    return pl.pallas_call(
        _scatter_w_kernel,
        grid=(n // tb,),
        in_specs=[pl.BlockSpec((tb, nj), lambda i: (i, 0)),
                  pl.BlockSpec((tb, nj), lambda i: (i, 0))],
        out_specs=pl.BlockSpec((tb, N_EXPERTS), lambda i: (i, 0)),
        out_shape=jax.ShapeDtypeStruct((n, N_EXPERTS), BF16),
        scratch_shapes=[pltpu.VMEM((N_KEYS * _W_PITCH, N_KEYS), F32)] * 2,
        compiler_params=_cparams(("parallel",)),
        name="scatter_w",
    )(eid, gate)


def _experts_kernel(xb_ref, u_ref, v_ref, w_ref, h1_ref, g_ref, b_ref, out_ref):
    k = pl.program_id(1)

    @pl.when(k == 0)
    def _():
        out_ref[...] = ALPHA * h1_ref[...]

    a = lax.dot_general(xb_ref[...], u_ref[...], (((1,), (1,)), ((), ())),
                        preferred_element_type=F32)
    act = 0.5 * a * (1.0 + lax.erf(a * (0.5 ** 0.5)))
    wa = (w_ref[...].astype(F32) * act).astype(BF16)
    out_ref[...] += jnp.dot(wa, v_ref[...], preferred_element_type=F32)

    @pl.when(k == pl.num_programs(1) - 1)
    def _():
        out_ref[...] = _layernorm(out_ref[...], g_ref[...], b_ref[...])


def _experts(h1b, u, v, w, h1, g, b, tm=1024, ec=1024):
    n = h1b.shape[0]
    tm = min(tm, n)
    once = pl.Buffered(1)
    return pl.pallas_call(
        _experts_kernel,
        grid=(n // tm, N_EXPERTS // ec),
        in_specs=[
            pl.BlockSpec((tm, D_MODEL), lambda i, k: (i, 0), pipeline_mode=once),
            pl.BlockSpec((ec, D_MODEL), lambda i, k: (k, 0)),
            pl.BlockSpec((ec, D_MODEL), lambda i, k: (k, 0)),
            pl.BlockSpec((tm, ec), lambda i, k: (i, k)),
            pl.BlockSpec((tm, D_MODEL), lambda i, k: (i, 0), pipeline_mode=once),
            pl.BlockSpec((1, D_MODEL), lambda i, k: (0, 0)),
            pl.BlockSpec((1, D_MODEL), lambda i, k: (0, 0)),
        ],
        out_specs=pl.BlockSpec((tm, D_MODEL), lambda i, k: (i, 0), pipeline_mode=once),
        out_shape=jax.ShapeDtypeStruct((n, D_MODEL), F32),
        compiler_params=_cparams(("parallel", "arbitrary")),
        name="experts",
    )(h1b, u, v, w, h1, g, b)


def kernel(x, ln_in_g, ln_in_b, w_in, b_gate, conv_w, conv_b, mh_norm_g, w_out, ln1_g, ln1_b,
           peer_wq, peer_keys, peer_u, peer_v, ln2_g, ln2_b):
    batch, seq, d = x.shape
    n = batch * seq
    row = lambda t: t.reshape(1, -1).astype(F32)
    l = 0
    w_main = w_in[l, :, :D_MAIN].astype(BF16)
    w_gate = jnp.pad(w_in[l, :, D_MAIN:].T, ((0, LANES - 2 * M_HEADS), (0, 0))).astype(BF16)

    x2 = x.reshape(n, d)
    proj, gates = _ln_proj(x2, row(ln_in_g), row(ln_in_b), w_main, w_gate, tm=_TILES["proj_tm"])

    r_rows, cola, colb = _gate_scan(gates, b_gate[l].reshape(2 * M_HEADS, 1), tl=_TILES["scan_tl"])
    rowg = r_rows.reshape(M_HEADS, n // CHUNK, CHUNK).transpose(1, 0, 2)
    ym = _mlstm(proj, rowg, cola, colb, row(mh_norm_g[l]), batch, seq, ts=_TILES["mlstm_ts"])

    cw = jnp.pad(conv_w[l], ((0, 8 - conv_w.shape[1]), (0, 0)))
    h1, h1b = _mix(proj, ym, x2, row(ln_in_g), row(ln_in_b), cw, row(conv_b[l]), w_out[l].astype(BF16),
                   row(ln1_g[l]), row(ln1_b[l]), seq, tm=_TILES["mix_tm"])

    kb = _block_diag_keys(peer_keys[l]).astype(BF16)
    q = _qproj(h1b, _half_major_query_weight(peer_wq[l]).astype(BF16))
    eid, gate = _retrieve(q, kb, tb=_TILES["retrieve_tb"])
    w = _scatter_w(eid, gate, tb=_TILES["scatter_tb"])
    out = _experts(h1b, peer_u[l].astype(BF16), peer_v[l].astype(BF16), w, h1,
                   row(ln2_g[l]), row(ln2_b[l]), tm=_TILES["experts_tm"], ec=_TILES["experts_ec"])
    return out.reshape(batch, seq, d)
```

```python
import functools

import jax
import jax.numpy as jnp
import numpy as np
from jax import lax
from jax.experimental import pallas as pl
from jax.experimental.pallas import tpu as pltpu

F32 = jnp.float32
BF16 = jnp.bfloat16

D_MODEL = 2048
CHUNK = 64
D_CONV = 1024
D_MLSTM = 1024
M_HEADS = 8
M_HEAD_DIM = 128
D_MAIN = 3 * D_CONV + 4 * D_MLSTM
PEER_HEADS = 8
N_KEYS = 128
N_EXPERTS = N_KEYS * N_KEYS
PEER_TOPK = 16
PEER_HALF = 128
DEPTH = 1
ALPHA = (2 * DEPTH) ** 0.25
LN_EPS = 1e-5
LANES = 128
VMEM_LIMIT = 56 * 1024 * 1024

_TILES = dict(proj_tm=1024, scan_tl=2048, mlstm_ts=1024, mix_tm=512, retrieve_tb=512,
              scatter_tb=128, experts_tm=1024, experts_ec=1024)


def _cparams(sem):
    return pltpu.CompilerParams(dimension_semantics=sem, vmem_limit_bytes=VMEM_LIMIT)


def _layernorm(x, g, b):
    mu = jnp.mean(x, axis=-1, keepdims=True)
    xc = x - mu
    var = jnp.mean(xc * xc, axis=-1, keepdims=True)
    return xc * lax.rsqrt(var + LN_EPS) * g + b


_LN_ROWS = 256


def _ln_proj_kernel(x_ref, g_ref, b_ref, w_ref, wg_ref, proj_ref, gate_ref, hb_ref):
    @pl.when(pl.program_id(1) == 0)
    def _():
        def rows_body(r, carry):
            rows = pl.ds(pl.multiple_of(r * _LN_ROWS, _LN_ROWS), _LN_ROWS)
            hb = _layernorm(x_ref[rows, :], g_ref[...], b_ref[...]).astype(BF16)
            hb_ref[rows, :] = hb
            gate_ref[:, rows] = lax.dot_general(wg_ref[...], hb, (((1,), (1,)), ((), ())),
                                                preferred_element_type=F32)
            return carry

        lax.fori_loop(0, x_ref.shape[0] // _LN_ROWS, rows_body, 0)

    proj_ref[...] = jnp.dot(hb_ref[...], w_ref[...], preferred_element_type=F32).astype(BF16)


def _ln_proj(x, g, b, w_main, w_gate, tm=1024, tn=1792):
    n = x.shape[0]
    tm = min(tm, n)
    return pl.pallas_call(
        _ln_proj_kernel,
        grid=(n // tm, D_MAIN // tn),
        in_specs=[
            pl.BlockSpec((tm, D_MODEL), lambda i, j: (i, 0)),
            pl.BlockSpec((1, D_MODEL), lambda i, j: (0, 0)),
            pl.BlockSpec((1, D_MODEL), lambda i, j: (0, 0)),
            pl.BlockSpec((D_MODEL, tn), lambda i, j: (0, j)),
            pl.BlockSpec((LANES, D_MODEL), lambda i, j: (0, 0)),
        ],
        out_specs=[
            pl.BlockSpec((tm, tn), lambda i, j: (i, j)),
            pl.BlockSpec((LANES, tm), lambda i, j: (0, i)),
        ],
        out_shape=[
            jax.ShapeDtypeStruct((n, D_MAIN), BF16),
            jax.ShapeDtypeStruct((LANES, n), F32),
        ],
        scratch_shapes=[pltpu.VMEM((tm, D_MODEL), BF16)],
        compiler_params=_cparams(("parallel", "arbitrary")),
        name="ln_proj",
    )(x, g, b, w_main, w_gate)


def _gate_scan_kernel(gt_ref, b_ref, r_ref, cola_ref, colb_ref):
    z = gt_ref[...] + b_ref[...]
    li = z[:M_HEADS]
    lf = jax.nn.log_sigmoid(z[M_HEADS:])
    pos = lax.broadcasted_iota(jnp.int32, lf.shape, 1) % CHUNK
    bcum = lf
    shift = 1
    while shift < CHUNK:
        bcum = bcum + jnp.where(pos >= shift, pltpu.roll(bcum, shift, axis=1), 0.0)
        shift *= 2
    r = li - bcum
    cm = r
    shift = 1
    while shift < CHUNK:
        cm = jnp.maximum(cm, jnp.where(pos >= shift, pltpu.roll(cm, shift, axis=1), -jnp.inf))
        shift *= 2
    r_ref[...] = r
    zero = jnp.zeros_like(r)
    cola_ref[...] = jnp.concatenate([zero, r, cm, cm] * 3 + [cm, zero, zero, zero], axis=0).T
    colb_ref[...] = jnp.concatenate([zero, zero, bcum, bcum] * 3 + [bcum, zero, zero, zero], axis=0).T


def _gate_scan(gt, bias, tl=2048):
    n = gt.shape[1]
    tl = min(tl, n)
    return pl.pallas_call(
        _gate_scan_kernel,
        grid=(n // tl,),
        in_specs=[pl.BlockSpec((2 * M_HEADS, tl), lambda i: (0, i)),
                  pl.BlockSpec((2 * M_HEADS, 1), lambda i: (0, 0))],
        out_specs=[pl.BlockSpec((M_HEADS, tl), lambda i: (0, i)),
                   pl.BlockSpec((tl, LANES), lambda i: (i, 0)),
                   pl.BlockSpec((tl, LANES), lambda i: (i, 0))],
        out_shape=[jax.ShapeDtypeStruct((M_HEADS, n), F32),
                   jax.ShapeDtypeStruct((n, LANES), F32),
                   jax.ShapeDtypeStruct((n, LANES), F32)],
        compiler_params=_cparams(("parallel",)),
        name="gate_scan",
    )(gt, bias)


def _mlstm_kernel(q_ref, k_ref, v_ref, o_ref, row_ref, cola_ref, colb_ref, spread_ref, ng_ref,
                  y_ref, cn_ref, m_ref, *, ts):
    @pl.when(pl.program_id(1) == 0)
    def _():
        cn_ref[...] = jnp.zeros_like(cn_ref)
        m_ref[...] = jnp.zeros_like(m_ref)

    L = CHUNK
    dh = M_HEAD_DIM
    scale = dh ** -0.5
    ones_v = jnp.ones((M_HEADS, L, dh), BF16)
    lane_mean = jnp.full((M_HEADS, dh, dh), 1.0 / dh, BF16)
    li = lax.broadcasted_iota(jnp.int32, (L, L), 0)
    lj = lax.broadcasted_iota(jnp.int32, (L, L), 1)
    causal = lj <= li
    lane = lax.broadcasted_iota(jnp.int32, (L, LANES), 1)
    lane_grp = lane // _COL_GROUP
    lane_in = lane % _COL_GROUP
    is_wk = (lane_grp < 3) & (lane_in >= M_HEADS) & (lane_in < 2 * M_HEADS)
    is_a = (lane_grp < 3) & (lane_in >= 2 * M_HEADS) & (lane_in < 3 * M_HEADS)
    is_wi = (lane_grp < 3) & (lane_in >= 3 * M_HEADS)

    def chunk_body(c, carry):
        t0 = pl.multiple_of(c * L, L)
        rows = pl.ds(t0, L)
        rowg = row_ref[c]
        ca = cola_ref[rows, :]
        cb = colb_ref[rows, :]
        m_row = m_ref[...]
        a_l = jnp.maximum(m_row, ca)
        w_inter_l = jnp.exp(m_row - a_l)
        floor_l = jnp.exp(-(cb + a_l))
        a_last = a_l[L - 1:L, :]
        wk_l = jnp.exp(ca - pltpu.roll(a_last, LANES - M_HEADS, axis=1))
        decay_row = jnp.exp(m_row - a_last)
        m_ref[...] = cb[L - 1:L, :] + a_last
        heads = range(M_HEADS)
        per_head = lambda ref: jnp.stack([ref[rows, h * dh:(h + 1) * dh] for h in heads])
        bmm = lambda x, y, cx, cy: lax.dot_general(
            x, y, (((cx,), (cy,)), ((0,), (0,))), preferred_element_type=F32)
        x = jnp.where(is_wk, wk_l, jnp.where(is_a, a_l, jnp.where(is_wi, w_inter_l, 0.0)))
        x_hi = x.astype(BF16)
        x_r1 = x - x_hi.astype(F32)
        x_mid = x_r1.astype(BF16)
        x_lo = (x_r1 - x_mid.astype(F32)).astype(BF16)
        parts = jnp.where(lane_grp == 0, x_hi, jnp.where(lane_grp == 1, x_mid, x_lo))
        rep = jnp.dot(parts, spread_ref[...], preferred_element_type=F32)
        wide = lambda qi: jnp.stack([rep[:, (qi * M_HEADS + h) * dh:(qi * M_HEADS + h + 1) * dh]
                                     for h in heads])
        a = wide(0)[..., :L]
        w_inter = wide(1)
        wk_wide = wide(2)
        floor = jnp.stack([floor_l[:, 3 * 32 + h:3 * 32 + h + 1] for h in heads])
        decay = jnp.stack([decay_row[:, 2 * M_HEADS + h:2 * M_HEADS + h + 1] for h in heads])
        q = per_head(q_ref)
        k = per_head(k_ref)
        v = per_head(v_ref)
        w_intra = jnp.exp(jnp.where(causal[None], rowg[:, None, :] - a, -jnp.inf))
        s = bmm(q, k, 2, 2) * scale * w_intra
        s_hi = s.astype(BF16)
        s_lo = (s - s_hi.astype(F32)).astype(BF16)
        sv = bmm(s_hi, jnp.concatenate([v, ones_v], axis=-1), 2, 1)
        cn = cn_ref[...]
        qc = bmm(q, cn.astype(BF16), 2, 1)
        num = sv[..., :dh] + w_inter * qc[..., :dh]
        den = sv[..., dh:] + bmm(s_lo, ones_v, 2, 1) + w_inter * qc[..., dh:]
        hval = num / jnp.maximum(jnp.abs(den), floor)
        wkv =jnp.concatenate([wk_wide * v.astype(F32), wk_wide], axis=-1).astype(BF16)
        cn_ref[...] = decay * cn + scale * bmm(k, wkv, 1, 1)
        mu = bmm(hval.astype(BF16), lane_mean, 2, 1)
        hc = hval - mu
        var = bmm((hc * hc).astype(BF16), lane_mean, 2, 1)
        hn = hc * lax.rsqrt(var + LN_EPS)
        for h in heads:
            hs = slice(h * dh, (h + 1) * dh)
            og = jax.nn.sigmoid(o_ref[rows, hs].astype(F32))
            y_ref[rows, hs] = (og * hn[h] * ng_ref[:, hs]).astype(BF16)
        return carry

    lax.fori_loop(0, ts // L, chunk_body, 0, unroll=2)


_COL_GROUP = 32


def _spread_matrix():
    k = np.arange(LANES)
    grp, slot, head = k // _COL_GROUP, (k % _COL_GROUP) // M_HEADS, k % M_HEADS
    block = ((slot + 1) % 3) * M_HEADS + head
    valid = (grp < 3) & (slot >= 1)
    cols = np.arange(3 * M_HEADS * M_HEAD_DIM) // M_HEAD_DIM
    return jnp.asarray((block[:, None] == cols[None, :]) & valid[:, None], dtype=BF16)


def _mlstm(proj, rowg, cola, colb, norm_g, batch, seq, ts=1024):
    n = proj.shape[0]
    ts = min(ts, seq)
    nb = seq // ts
    blk = lambda col: pl.BlockSpec((ts, D_MLSTM), lambda b, i, col=col: (b * nb + i, col))
    return pl.pallas_call(
        functools.partial(_mlstm_kernel, ts=ts),
        grid=(batch, nb),
        in_specs=[
            blk(3), blk(4), blk(5), blk(6),
            pl.BlockSpec((ts // CHUNK, M_HEADS, CHUNK), lambda b, i: (b * nb + i, 0, 0)),
            pl.BlockSpec((ts, LANES), lambda b, i: (b * nb + i, 0)),
            pl.BlockSpec((ts, LANES), lambda b, i: (b * nb + i, 0)),
            pl.BlockSpec((LANES, 3 * M_HEADS * M_HEAD_DIM), lambda b, i: (0, 0),
                         pipeline_mode=pl.Buffered(1)),
            pl.BlockSpec((1, D_MLSTM), lambda b, i: (0, 0)),
        ],
        out_specs=pl.BlockSpec((ts, D_MLSTM), lambda b, i: (b * nb + i, 0)),
        out_shape=jax.ShapeDtypeStruct((n, D_MLSTM), BF16),
        scratch_shapes=[
            pltpu.VMEM((M_HEADS, M_HEAD_DIM, 2 * M_HEAD_DIM), F32),
            pltpu.VMEM((1, LANES), F32),
        ],
        compiler_params=_cparams(("parallel", "arbitrary")),
        name="mlstm",
    )(proj, proj, proj, proj, rowg, cola, colb, _spread_matrix(), norm_g)


def _mix_kernel(cb_ref, cc_ref, ch_ref, pc_ref, ph_ref, ym_ref, x_ref, g0_ref, b0_ref, cw_ref,
                cbias_ref, wo_ref, g_ref, b_ref, h1_ref, h1b_ref, *, blocks_per_seq):
    tm = cb_ref.shape[0]
    z = cc_ref[...].astype(F32) * ch_ref[...].astype(F32)
    halo = pc_ref[...].astype(F32) * ph_ref[...].astype(F32)
    first = (pl.program_id(0) % blocks_per_seq) == 0
    halo = jnp.where(first, 0.0, halo)
    row = lax.broadcasted_iota(jnp.int32, z.shape, 0)
    hr = halo.shape[0]
    z1 = jnp.where(row == 0, halo[hr - 1:hr, :], pltpu.roll(z, 1, axis=0))
    z2 = pltpu.roll(z, 2, axis=0)
    z2 = jnp.where(row == 0, halo[hr - 2:hr - 1, :], z2)
    z2 = jnp.where(row == 1, halo[hr - 1:hr, :], z2)
    conv = cbias_ref[...] + cw_ref[0:1, :] * z2 + cw_ref[1:2, :] * z1 + cw_ref[2:3, :] * z
    y_conv = (cb_ref[...].astype(F32) * conv).astype(BF16)
    mix = (jnp.dot(y_conv, wo_ref[0:D_CONV, :], preferred_element_type=F32)
           + jnp.dot(ym_ref[...], wo_ref[D_CONV:, :], preferred_element_type=F32))
    h = _layernorm(x_ref[...], g0_ref[...], b0_ref[...])
    h1 = _layernorm(ALPHA * h + mix, g_ref[...], b_ref[...])
    h1_ref[...] = h1
    h1b_ref[...] = h1.astype(BF16)


def _mix(proj, ym, x, g0, b0, conv_w, conv_b, w_out, g, b, seq, tm=512):
    n = proj.shape[0]
    tm = min(tm, seq)
    halo = 16
    hb = tm // halo
    cur = lambda col: pl.BlockSpec((tm, D_CONV), lambda i, col=col: (i, col))
    prev = lambda col: pl.BlockSpec((halo, D_CONV),
                                    lambda i, col=col: (jnp.maximum(i * hb - 1, 0), col))
    const = lambda shape: pl.BlockSpec(shape, lambda i: (0, 0))
    return pl.pallas_call(
        functools.partial(_mix_kernel, blocks_per_seq=seq // tm),
        grid=(n // tm,),
        in_specs=[
            cur(0), cur(1), cur(2), prev(1), prev(2),
            pl.BlockSpec((tm, D_MLSTM), lambda i: (i, 0)),
            pl.BlockSpec((tm, D_MODEL), lambda i: (i, 0)),
            const((1, D_MODEL)), const((1, D_MODEL)),
            const((8, D_CONV)), const((1, D_CONV)),
            pl.BlockSpec((D_MODEL, D_MODEL), lambda i: (0, 0), pipeline_mode=pl.Buffered(1)),
            const((1, D_MODEL)), const((1, D_MODEL)),
        ],
        out_specs=[pl.BlockSpec((tm, D_MODEL), lambda i: (i, 0)),
                   pl.BlockSpec((tm, D_MODEL), lambda i: (i, 0))],
        out_shape=[jax.ShapeDtypeStruct((n, D_MODEL), F32),
                   jax.ShapeDtypeStruct((n, D_MODEL), BF16)],
        compiler_params=_cparams(("parallel",)),
        name="mix",
    )(proj, proj, proj, proj, proj, ym, x, g0, b0, conv_w, conv_b, w_out, g, b)


def _qproj_kernel(x_ref, wqt_ref, qt_ref):
    qt_ref[...] = lax.dot_general(wqt_ref[...], x_ref[...], (((1,), (1,)), ((), ())),
                                  preferred_element_type=F32).astype(BF16)


def _qproj(h1b, wqt, tm=1024):
    n = h1b.shape[0]
    tm = min(tm, n)
    dq = wqt.shape[0]
    return pl.pallas_call(
        _qproj_kernel,
        grid=(n // tm,),
        in_specs=[pl.BlockSpec((tm, D_MODEL), lambda i: (i, 0)),
                  pl.BlockSpec((dq, D_MODEL), lambda i: (0, 0), pipeline_mode=pl.Buffered(1))],
        out_specs=pl.BlockSpec((dq, tm), lambda i: (0, i)),
        out_shape=jax.ShapeDtypeStruct((dq, n), BF16),
        compiler_params=_cparams(("parallel",)),
        name="peer_q",
    )(h1b, wqt)


def _oddeven_sort_pairs(n):
    pairs = []
    p = 1
    while p < n:
        k = p
        while k >= 1:
            for j in range(k % p, n - k, 2 * k):
                for i in range(min(k, n - j - k)):
                    if (i + j) // (2 * p) == (i + j + k) // (2 * p):
                        pairs.append((i + j, i + j + k))
            k //= 2
        p *= 2
    return pairs


_SORT16 = _oddeven_sort_pairs(PEER_TOPK)
_BITONIC16 = [(i, i + d) for d in (8, 4, 2, 1) for i in range(PEER_TOPK) if not i & d]


def _compare_exchange(v, ix, pairs):
    for i, j in pairs:
        up = v[j] > v[i]
        v[i], v[j] = jnp.maximum(v[i], v[j]), jnp.minimum(v[i], v[j])
        ix[i], ix[j] = jnp.where(up, ix[j], ix[i]), jnp.where(up, ix[i], ix[j])


def _network_top16(cur_scr, srt_v, srt_i, sv_scr, si_scr, p, lanes):
    K = PEER_TOPK
    for g in range(N_KEYS // K):
        v = [cur_scr[g * K + i, :, lanes] for i in range(K)]
        ix = [jnp.int32(g * K + i) for i in range(K)]
        _compare_exchange(v, ix, _SORT16)
        for i in range(K):
            srt_v[g * K + i] = v[i]
            srt_i[g * K + i] = ix[i]
    dropped = jnp.full(v[0].shape, -jnp.inf, F32)
    span = K
    while span < N_KEYS:
        for a in range(0, N_KEYS, 2 * span):
            b = a + span
            va = [srt_v[a + i] for i in range(K)]
            ia = [srt_i[a + i] for i in range(K)]
            v, ix = [], []
            for i in range(K):
                vb = srt_v[b + K - 1 - i]
                up = vb > va[i]
                v.append(jnp.maximum(va[i], vb))
                ix.append(jnp.where(up, srt_i[b + K - 1 - i], ia[i]))
                dropped = jnp.maximum(dropped, jnp.minimum(va[i], vb))
            _compare_exchange(v, ix, _BITONIC16)
            for i in range(K):
                srt_v[a + i] = v[i]
                srt_i[a + i] = ix[i]
        span *= 2
    distinct = v[K - 1] > dropped
    for i in range(K - 1):
        distinct = distinct & (v[i] > v[i + 1])
    for i in range(K):
        sv_scr[p, i, :, lanes] = v[i]
        si_scr[p, i, :, lanes] = ix[i]
    return jnp.where(distinct, 0, 1)


def _retrieve_kernel(q_ref, kb_ref, eid_ref, gate_ref, cur_scr, sv_scr, si_scr, srt_v, srt_i):
    tb = q_ref.shape[1]
    dh = PEER_HEADS * PEER_HALF

    for p in range(2):
        s = jnp.dot(kb_ref[p], q_ref[p * dh:(p + 1) * dh, :],
                    preferred_element_type=F32)
        cur_scr[...] = s.reshape(N_KEYS, PEER_HEADS, tb)

        def lane_body(lh, tied, p=p):
            lanes = pl.ds(pl.multiple_of(lh * LANES, LANES), LANES)
            return jnp.maximum(tied, _network_top16(cur_scr, srt_v, srt_i, sv_scr, si_scr, p, lanes))

        tied = lax.fori_loop(0, tb // LANES, lane_body, jnp.zeros((PEER_HEADS, LANES), jnp.int32))

        def round_body(k, taken, p=p):
            vals, ids = [], list(range(N_KEYS))
            for n in range(N_KEYS):
                c = jnp.where(taken == n, -jnp.inf, cur_scr[n])
                cur_scr[n] = c
                vals.append(c)
            while len(vals) > 1:
                right = [vals[i + 1] > vals[i] for i in range(0, len(vals), 2)]
                ids = [jnp.where(g, ids[2 * i + 1], ids[2 * i]) for i, g in enumerate(right)]
                vals = [jnp.maximum(vals[2 * i], vals[2 * i + 1]) for i in range(len(right))]
            sv_scr[p, k] = vals[0]
            si_scr[p, k] = ids[0]
            return ids[0]

        @pl.when(jnp.max(tied) > 0)
        def _(round_body=round_body):
            lax.fori_loop(0, PEER_TOPK, round_body, jnp.full((PEER_HEADS, tb), N_KEYS, jnp.int32))

    sv1 = sv_scr[0]
    sv2 = sv_scr[1]
    e1 = si_scr[0] * N_KEYS
    e2 = si_scr[1]
    a_iota = lax.broadcasted_iota(jnp.int32, (PEER_TOPK, PEER_HEADS, tb), 0)
    front = sv1 + sv2[0][None]
    ptr = jnp.zeros_like(a_iota)
    tops, eids = [], []
    for k in range(PEER_TOPK):
        mx = jnp.max(front, axis=0)
        awin = jnp.min(jnp.where(front == mx[None], a_iota, PEER_TOPK), axis=0)
        hit = a_iota == awin[None]
        bwin = jnp.sum(jnp.where(hit, ptr, 0), axis=0)
        eid = (jnp.sum(jnp.where(hit, e1, 0), axis=0)
               + jnp.sum(jnp.where(a_iota == bwin[None], e2, 0), axis=0))
        nxt = jnp.max(jnp.where(a_iota == (bwin + 1)[None], sv2, -jnp.inf), axis=0)
        front = jnp.where(hit, sv1 + nxt[None], front)
        ptr = jnp.where(hit, ptr + 1, ptr)
        tops.append(mx)
        eids.append(eid)
    top_s = jnp.stack(tops)
    e = jnp.exp(top_s - tops[0][None])
    gate = e / jnp.sum(e, axis=0)[None]
    nj = PEER_TOPK * PEER_HEADS
    eid_ref[...] = jnp.stack(eids).reshape(nj, tb).T
    gate_ref[...] = gate.reshape(nj, tb).T


def _retrieve(q, kb, tb=256):
    n = q.shape[1]
    tb = min(tb, n)
    nj = PEER_HEADS * PEER_TOPK
    dq = 2 * PEER_HEADS * PEER_HALF
    return pl.pallas_call(
        _retrieve_kernel,
        grid=(n // tb,),
        in_specs=[
            pl.BlockSpec((dq, tb), lambda i: (0, i)),
            pl.BlockSpec((2, N_KEYS * PEER_HEADS, dq // 2), lambda i: (0, 0, 0),
                         pipeline_mode=pl.Buffered(1)),
        ],
        out_specs=[pl.BlockSpec((tb, nj), lambda i: (i, 0)),
                   pl.BlockSpec((tb, nj), lambda i: (i, 0))],
        out_shape=[jax.ShapeDtypeStruct((n, nj), jnp.int32),
                   jax.ShapeDtypeStruct((n, nj), F32)],
        scratch_shapes=[
            pltpu.VMEM((N_KEYS, PEER_HEADS, tb), F32),
            pltpu.VMEM((2, PEER_TOPK, PEER_HEADS, tb), F32),
            pltpu.VMEM((2, PEER_TOPK, PEER_HEADS, tb), jnp.int32),
            pltpu.VMEM((N_KEYS, PEER_HEADS, LANES), F32),
            pltpu.VMEM((N_KEYS, PEER_HEADS, LANES), jnp.int32),
        ],
        compiler_params=_cparams(("parallel",)),
        name="retrieve",
    )(q, kb)


def _block_diag_keys(keys):
    eye_h = jnp.eye(PEER_HEADS, dtype=keys.dtype)
    kb = jnp.einsum("hpnc,hg->pnhgc", keys, eye_h)
    return kb.reshape(2, N_KEYS * PEER_HEADS, PEER_HEADS * PEER_HALF)


def _half_major_query_weight(wq):
    d = wq.shape[0]
    return wq.reshape(d, PEER_HEADS, 2, PEER_HALF).transpose(2, 1, 3, 0).reshape(-1, d)


_W_GROUP = 16
_W_PITCH = _W_GROUP + 4


def _scatter_w_kernel(eid_ref, gate_ref, w_ref, scr_a, scr_b):
    tb = eid_ref.shape[0]
    nj = eid_ref.shape[1]
    ngrp = tb // _W_GROUP
    sub = lax.broadcasted_iota(jnp.int32, (N_KEYS, nj), 0)

    def build(grp, scr):
        base = pl.multiple_of(grp * _W_GROUP, _W_GROUP)
        for tl in range(_W_GROUP):
            e = eid_ref[pl.ds(base + tl, 1), :]
            g = gate_ref[pl.ds(base + tl, 1), :]
            p = jnp.where(sub == (e >> 7), 1.0, 0.0).astype(BF16)
            q = jnp.where(sub == (e & (N_KEYS - 1)), g, 0.0).astype(BF16)
            wt = lax.dot_general(p, q, (((1,), (1,)), ((), ())), preferred_element_type=F32)
            scr[pl.ds(tl, N_KEYS, stride=_W_PITCH), :] = wt

    def emit(grp, scr):
        base = pl.multiple_of(grp * _W_GROUP, _W_GROUP)
        for r in range(N_KEYS):
            tile = scr[r * _W_PITCH:r * _W_PITCH + _W_GROUP, :]
            w_ref[pl.ds(base, _W_GROUP), r * N_KEYS:(r + 1) * N_KEYS] = tile.astype(BF16)

    build(0, scr_a)

    def pair_body(i, carry):
        build(2 * i + 1, scr_b)
        emit(2 * i, scr_a)
        build(2 * i + 2, scr_a)
        emit(2 * i + 1, scr_b)
        return carry

    lax.fori_loop(0, ngrp // 2 - 1, pair_body, 0)
    build(ngrp - 1, scr_b)
    emit(ngrp - 2, scr_a)
    emit(ngrp - 1, scr_b)


def _scatter_w(eid, gate, tb=128):
    n, nj = eid.shape
    tb = min(tb, n)
    assert tb % (2 * _W_GROUP) == 0
    return pl.pallas_call(
        _scatter_w_kernel,
        grid=(n // tb,),
        in_specs=[pl.BlockSpec((tb, nj), lambda i: (i, 0)),
                  pl.BlockSpec((tb, nj), lambda i: (i, 0))],
        out_specs=pl.BlockSpec((tb, N_EXPERTS), lambda i: (i, 0)),
        out_shape=jax.ShapeDtypeStruct((n, N_EXPERTS), BF16),
        scratch_shapes=[pltpu.VMEM((N_KEYS * _W_PITCH, N_KEYS), F32)] * 2,
        compiler_params=_cparams(("parallel",)),
        name="scatter_w",
    )(eid, gate)


def _experts_kernel(xb_ref, u_ref, v_ref, w_ref, h1_ref, g_ref, b_ref, out_ref):
    k = pl.program_id(1)

    @pl.when(k == 0)
    def _():
        out_ref[...] = ALPHA * h1_ref[...]

    a = lax.dot_general(xb_ref[...], u_ref[...], (((1,), (1,)), ((), ())),
                        preferred_element_type=F32)
    act = 0.5 * a * (1.0 + lax.erf(a * (0.5 ** 0.5)))
    wa = (w_ref[...].astype(F32) * act).astype(BF16)
    out_ref[...] += jnp.dot(wa, v_ref[...], preferred_element_type=F32)

    @pl.when(k == pl.num_programs(1) - 1)
    def _():
        out_ref[...] = _layernorm(out_ref[...], g_ref[...], b_ref[...])


def _experts(h1b, u, v, w, h1, g, b, tm=1024, ec=1024):
    n = h1b.shape[0]
    tm = min(tm, n)
    once = pl.Buffered(1)
    return pl.pallas_call(
        _experts_kernel,
        grid=(n // tm, N_EXPERTS // ec),
        in_specs=[
            pl.BlockSpec((tm, D_MODEL), lambda i, k: (i, 0), pipeline_mode=once),
            pl.BlockSpec((ec, D_MODEL), lambda i, k: (k, 0)),
            pl.BlockSpec((ec, D_MODEL), lambda i, k: (k, 0)),
            pl.BlockSpec((tm, ec), lambda i, k: (i, k)),
            pl.BlockSpec((tm, D_MODEL), lambda i, k: (i, 0), pipeline_mode=once),
            pl.BlockSpec((1, D_MODEL), lambda i, k: (0, 0)),
            pl.BlockSpec((1, D_MODEL), lambda i, k: (0, 0)),
        ],
        out_specs=pl.BlockSpec((tm, D_MODEL), lambda i, k: (i, 0), pipeline_mode=once),
        out_shape=jax.ShapeDtypeStruct((n, D_MODEL), F32),
        compiler_params=_cparams(("parallel", "arbitrary")),
        name="experts",
    )(h1b, u, v, w, h1, g, b)


def kernel(x, ln_in_g, ln_in_b, w_in, b_gate, conv_w, conv_b, mh_norm_g, w_out, ln1_g, ln1_b,
           peer_wq, peer_keys, peer_u, peer_v, ln2_g, ln2_b):
    batch, seq, d = x.shape
    n = batch * seq
    row = lambda t: t.reshape(1, -1).astype(F32)
    l = 0
    w_main = w_in[l, :, :D_MAIN].astype(BF16)
    w_gate = jnp.pad(w_in[l, :, D_MAIN:].T, ((0, LANES - 2 * M_HEADS), (0, 0))).astype(BF16)

    x2 = x.reshape(n, d)
    proj, gates = _ln_proj(x2, row(ln_in_g), row(ln_in_b), w_main, w_gate, tm=_TILES["proj_tm"])

    r_rows, cola, colb = _gate_scan(gates, b_gate[l].reshape(2 * M_HEADS, 1), tl=_TILES["scan_tl"])
    rowg = r_rows.reshape(M_HEADS, n // CHUNK, CHUNK).transpose(1, 0, 2)
    ym = _mlstm(proj, rowg, cola, colb, row(mh_norm_g[l]), batch, seq, ts=_TILES["mlstm_ts"])

    cw = jnp.pad(conv_w[l], ((0, 8 - conv_w.shape[1]), (0, 0)))
    h1, h1b = _mix(proj, ym, x2, row(ln_in_g), row(ln_in_b), cw, row(conv_b[l]), w_out[l].astype(BF16),
                   row(ln1_g[l]), row(ln1_b[l]), seq, tm=_TILES["mix_tm"])

    kb = _block_diag_keys(peer_keys[l].astype(BF16))
    q = _qproj(h1b, _half_major_query_weight(peer_wq[l].astype(BF16)))
    eid, gate = _retrieve(q, kb, tb=_TILES["retrieve_tb"])
    w = _scatter_w(eid, gate, tb=_TILES["scatter_tb"])
    out = _experts(h1b, peer_u[l].astype(BF16), peer_v[l].astype(BF16), w, h1,
                   row(ln2_g[l]), row(ln2_b[l]), tm=_TILES["experts_tm"], ec=_TILES["experts_ec"])
    return out.reshape(batch, seq, d)
```

```python
import functools

import jax
import jax.numpy as jnp
import numpy as np
from jax import lax
from jax.experimental import pallas as pl
from jax.experimental.pallas import tpu as pltpu

F32 = jnp.float32
BF16 = jnp.bfloat16

D_MODEL = 2048
CHUNK = 64
D_CONV = 1024
D_MLSTM = 1024
M_HEADS = 8
M_HEAD_DIM = 128
D_MAIN = 3 * D_CONV + 4 * D_MLSTM
PEER_HEADS = 8
N_KEYS = 128
N_EXPERTS = N_KEYS * N_KEYS
PEER_TOPK = 16
PEER_HALF = 128
DEPTH = 1
ALPHA = (2 * DEPTH) ** 0.25
LN_EPS = 1e-5
LANES = 128
VMEM_LIMIT = 56 * 1024 * 1024

_TILES = dict(proj_tm=1024, scan_tl=2048, mlstm_ts=1024, mix_tm=512, retrieve_tb=512,
              scatter_tb=512, experts_tm=1024, experts_ec=1024)


def _cparams(sem):
    return pltpu.CompilerParams(dimension_semantics=sem, vmem_limit_bytes=VMEM_LIMIT)


def _layernorm(x, g, b):
    mu = jnp.mean(x, axis=-1, keepdims=True)
    xc = x - mu
    var = jnp.mean(xc * xc, axis=-1, keepdims=True)
    return xc * lax.rsqrt(var + LN_EPS) * g + b


_LN_ROWS = 256


def _ln_proj_kernel(x_ref, g_ref, b_ref, w_ref, wg_ref, proj_ref, gate_ref, hb_ref):
    @pl.when(pl.program_id(1) == 0)
    def _():
        def rows_body(r, carry):
            rows = pl.ds(pl.multiple_of(r * _LN_ROWS, _LN_ROWS), _LN_ROWS)
            hb = _layernorm(x_ref[rows, :], g_ref[...], b_ref[...]).astype(BF16)
            hb_ref[rows, :] = hb
            gate_ref[rows, :] = jnp.dot(hb, wg_ref[...], preferred_element_type=F32)
            return carry

        lax.fori_loop(0, x_ref.shape[0] // _LN_ROWS, rows_body, 0)

    proj_ref[...] = jnp.dot(hb_ref[...], w_ref[...], preferred_element_type=F32).astype(BF16)


def _ln_proj(x, g, b, w_main, w_gate, tm=1024, tn=1792):
    n = x.shape[0]
    tm = min(tm, n)
    return pl.pallas_call(
        _ln_proj_kernel,
        grid=(n // tm, D_MAIN // tn),
        in_specs=[
            pl.BlockSpec((tm, D_MODEL), lambda i, j: (i, 0)),
            pl.BlockSpec((1, D_MODEL), lambda i, j: (0, 0)),
            pl.BlockSpec((1, D_MODEL), lambda i, j: (0, 0)),
            pl.BlockSpec((D_MODEL, tn), lambda i, j: (0, j)),
            pl.BlockSpec((D_MODEL, LANES), lambda i, j: (0, 0)),
        ],
        out_specs=[
            pl.BlockSpec((tm, tn), lambda i, j: (i, j)),
            pl.BlockSpec((tm, LANES), lambda i, j: (i, 0)),
        ],
        out_shape=[
            jax.ShapeDtypeStruct((n, D_MAIN), BF16),
            jax.ShapeDtypeStruct((n, LANES), F32),
        ],
        scratch_shapes=[pltpu.VMEM((tm, D_MODEL), BF16)],
        compiler_params=_cparams(("parallel", "arbitrary")),
        name="ln_proj",
    )(x, g, b, w_main, w_gate)


def _gate_scan_kernel(gt_ref, b_ref, r_ref, cola_ref, colb_ref):
    z = gt_ref[...].T[:2 * M_HEADS] + b_ref[...]
    li = z[:M_HEADS]
    lf = jax.nn.log_sigmoid(z[M_HEADS:])
    pos = lax.broadcasted_iota(jnp.int32, lf.shape, 1) % CHUNK
    bcum = lf
    shift = 1
    while shift < CHUNK:
        bcum = bcum + jnp.where(pos >= shift, pltpu.roll(bcum, shift, axis=1), 0.0)
        shift *= 2
    r = li - bcum
    cm = r
    shift = 1
    while shift < CHUNK:
        cm = jnp.maximum(cm, jnp.where(pos >= shift, pltpu.roll(cm, shift, axis=1), -jnp.inf))
        shift *= 2
    r_ref[...] = r
    zero = jnp.zeros_like(r)
    cola_ref[...] = jnp.concatenate([zero, r, cm, cm] * 3 + [cm, zero, zero, zero], axis=0).T
    colb_ref[...] = jnp.concatenate([zero, zero, bcum, bcum] * 3 + [bcum, zero, zero, zero], axis=0).T


def _gate_scan(gt, bias, tl=2048):
    n = gt.shape[0]
    tl = min(tl, n)
    return pl.pallas_call(
        _gate_scan_kernel,
        grid=(n // tl,),
        in_specs=[pl.BlockSpec((tl, LANES), lambda i: (i, 0)),
                  pl.BlockSpec((2 * M_HEADS, 1), lambda i: (0, 0))],
        out_specs=[pl.BlockSpec((M_HEADS, tl), lambda i: (0, i)),
                   pl.BlockSpec((tl, LANES), lambda i: (i, 0)),
                   pl.BlockSpec((tl, LANES), lambda i: (i, 0))],
        out_shape=[jax.ShapeDtypeStruct((M_HEADS, n), F32),
                   jax.ShapeDtypeStruct((n, LANES), F32),
                   jax.ShapeDtypeStruct((n, LANES), F32)],
        compiler_params=_cparams(("parallel",)),
        name="gate_scan",
    )(gt, bias)


def _mlstm_kernel(q_ref, k_ref, v_ref, o_ref, row_ref, cola_ref, colb_ref, spread_ref, ng_ref,
                  y_ref, cn_ref, m_ref, *, ts):
    @pl.when(pl.program_id(1) == 0)
    def _():
        cn_ref[...] = jnp.zeros_like(cn_ref)
        m_ref[...] = jnp.zeros_like(m_ref)

    L = CHUNK
    dh = M_HEAD_DIM
    scale = dh ** -0.5
    ones_v = jnp.ones((M_HEADS, L, dh), BF16)
    lane_mean = jnp.full((M_HEADS, dh, dh), 1.0 / dh, BF16)
    li = lax.broadcasted_iota(jnp.int32, (L, L), 0)
    lj = lax.broadcasted_iota(jnp.int32, (L, L), 1)
    causal = lj <= li
    lane = lax.broadcasted_iota(jnp.int32, (L, LANES), 1)
    lane_grp = lane // _COL_GROUP
    lane_in = lane % _COL_GROUP
    is_wk = (lane_grp < 3) & (lane_in >= M_HEADS) & (lane_in < 2 * M_HEADS)
    is_a = (lane_grp < 3) & (lane_in >= 2 * M_HEADS) & (lane_in < 3 * M_HEADS)
    is_wi = (lane_grp < 3) & (lane_in >= 3 * M_HEADS)

    def chunk_body(c, carry):
        t0 = pl.multiple_of(c * L, L)
        rows = pl.ds(t0, L)
        rowg = row_ref[c]
        ca = cola_ref[rows, :]
        cb = colb_ref[rows, :]
        m_row = m_ref[...]
        a_l = jnp.maximum(m_row, ca)
        w_inter_l = jnp.exp(m_row - a_l)
        floor_l = jnp.exp(-(cb + a_l))
        a_last = a_l[L - 1:L, :]
        wk_l = jnp.exp(ca - pltpu.roll(a_last, LANES - M_HEADS, axis=1))
        decay_row = jnp.exp(m_row - a_last)
        m_ref[...] = cb[L - 1:L, :] + a_last
        heads = range(M_HEADS)
        per_head = lambda ref: jnp.stack([ref[rows, h * dh:(h + 1) * dh] for h in heads])
        bmm = lambda x, y, cx, cy: lax.dot_general(
            x, y, (((cx,), (cy,)), ((0,), (0,))), preferred_element_type=F32)
        x = jnp.where(is_wk, wk_l, jnp.where(is_a, a_l, jnp.where(is_wi, w_inter_l, 0.0)))
        x_hi = x.astype(BF16)
        x_r1 = x - x_hi.astype(F32)
        x_mid = x_r1.astype(BF16)
        x_lo = (x_r1 - x_mid.astype(F32)).astype(BF16)
        parts = jnp.where(lane_grp == 0, x_hi, jnp.where(lane_grp == 1, x_mid, x_lo))
        rep = jnp.dot(parts, spread_ref[...], preferred_element_type=F32)
        wide = lambda qi: jnp.stack([rep[:, (qi * M_HEADS + h) * dh:(qi * M_HEADS + h + 1) * dh]
                                     for h in heads])
        a = wide(0)[..., :L]
        w_inter = wide(1)
        wk_wide = wide(2)
        floor = jnp.stack([floor_l[:, 3 * 32 + h:3 * 32 + h + 1] for h in heads])
        decay = jnp.stack([decay_row[:, 2 * M_HEADS + h:2 * M_HEADS + h + 1] for h in heads])
        q = per_head(q_ref)
        k = per_head(k_ref)
        v = per_head(v_ref)
        w_intra = jnp.exp(jnp.where(causal[None], rowg[:, None, :] - a, -jnp.inf))
        s = bmm(q, k, 2, 2) * scale * w_intra
        s_hi = s.astype(BF16)
        s_lo = (s - s_hi.astype(F32)).astype(BF16)
        sv = bmm(s_hi, jnp.concatenate([v, ones_v], axis=-1), 2, 1)
        cn = cn_ref[...]
        qc = bmm(q, cn.astype(BF16), 2, 1)
        num = sv[..., :dh] + w_inter * qc[..., :dh]
        den = sv[..., dh:] + bmm(s_lo, ones_v, 2, 1) + w_inter * qc[..., dh:]
        hval = num / jnp.maximum(jnp.abs(den), floor)
        wkv =jnp.concatenate([wk_wide * v.astype(F32), wk_wide], axis=-1).astype(BF16)
        cn_ref[...] = decay * cn + scale * bmm(k, wkv, 1, 1)
        mu = bmm(hval.astype(BF16), lane_mean, 2, 1)
        hc = hval - mu
        var = bmm((hc * hc).astype(BF16), lane_mean, 2, 1)
        hn = hc * lax.rsqrt(var + LN_EPS)
        for h in heads:
            hs = slice(h * dh, (h + 1) * dh)
            og = jax.nn.sigmoid(o_ref[rows, hs].astype(F32))
            y_ref[rows, hs] = (og * hn[h] * ng_ref[:, hs]).astype(BF16)
        return carry

    lax.fori_loop(0, ts // L, chunk_body, 0, unroll=2)


_COL_GROUP = 32


def _spread_matrix():
    k = np.arange(LANES)
    grp, slot, head = k // _COL_GROUP, (k % _COL_GROUP) // M_HEADS, k % M_HEADS
    block = ((slot + 1) % 3) * M_HEADS + head
    valid = (grp < 3) & (slot >= 1)
    cols = np.arange(3 * M_HEADS * M_HEAD_DIM) // M_HEAD_DIM
    return jnp.asarray((block[:, None] == cols[None, :]) & valid[:, None], dtype=BF16)


def _mlstm(proj, rowg, cola, colb, norm_g, batch, seq, ts=1024):
    n = proj.shape[0]
    ts = min(ts, seq)
    nb = seq // ts
    blk = lambda col: pl.BlockSpec((ts, D_MLSTM), lambda b, i, col=col: (b * nb + i, col))
    return pl.pallas_call(
        functools.partial(_mlstm_kernel, ts=ts),
        grid=(batch, nb),
        in_specs=[
            blk(3), blk(4), blk(5), blk(6),
            pl.BlockSpec((ts // CHUNK, M_HEADS, CHUNK), lambda b, i: (b * nb + i, 0, 0)),
            pl.BlockSpec((ts, LANES), lambda b, i: (b * nb + i, 0)),
            pl.BlockSpec((ts, LANES), lambda b, i: (b * nb + i, 0)),
            pl.BlockSpec((LANES, 3 * M_HEADS * M_HEAD_DIM), lambda b, i: (0, 0),
                         pipeline_mode=pl.Buffered(1)),
            pl.BlockSpec((1, D_MLSTM), lambda b, i: (0, 0)),
        ],
        out_specs=pl.BlockSpec((ts, D_MLSTM), lambda b, i: (b * nb + i, 0)),
        out_shape=jax.ShapeDtypeStruct((n, D_MLSTM), BF16),
        scratch_shapes=[
            pltpu.VMEM((M_HEADS, M_HEAD_DIM, 2 * M_HEAD_DIM), F32),
            pltpu.VMEM((1, LANES), F32),
        ],
        compiler_params=_cparams(("parallel", "arbitrary")),
        name="mlstm",
    )(proj, proj, proj, proj, rowg, cola, colb, _spread_matrix(), norm_g)


def _mix_kernel(cb_ref, cc_ref, ch_ref, pc_ref, ph_ref, ym_ref, x_ref, g0_ref, b0_ref, cw_ref,
                cbias_ref, wo_ref, g_ref, b_ref, h1_ref, h1b_ref, *, blocks_per_seq):
    tm = cb_ref.shape[0]
    z = cc_ref[...].astype(F32) * ch_ref[...].astype(F32)
    halo = pc_ref[...].astype(F32) * ph_ref[...].astype(F32)
    first = (pl.program_id(0) % blocks_per_seq) == 0
    halo = jnp.where(first, 0.0, halo)
    row = lax.broadcasted_iota(jnp.int32, z.shape, 0)
    hr = halo.shape[0]
    z1 = jnp.where(row == 0, halo[hr - 1:hr, :], pltpu.roll(z, 1, axis=0))
    z2 = pltpu.roll(z, 2, axis=0)
    z2 = jnp.where(row == 0, halo[hr - 2:hr - 1, :], z2)
    z2 = jnp.where(row == 1, halo[hr - 1:hr, :], z2)
    conv = cbias_ref[...] + cw_ref[0:1, :] * z2 + cw_ref[1:2, :] * z1 + cw_ref[2:3, :] * z
    y_conv = (cb_ref[...].astype(F32) * conv).astype(BF16)
    mix = (jnp.dot(y_conv, wo_ref[0:D_CONV, :], preferred_element_type=F32)
           + jnp.dot(ym_ref[...], wo_ref[D_CONV:, :], preferred_element_type=F32))
    h = _layernorm(x_ref[...], g0_ref[...], b0_ref[...])
    h1 = _layernorm(ALPHA * h + mix, g_ref[...], b_ref[...])
    h1_ref[...] = h1
    h1b_ref[...] = h1.astype(BF16)


def _mix(proj, ym, x, g0, b0, conv_w, conv_b, w_out, g, b, seq, tm=512):
    n = proj.shape[0]
    tm = min(tm, seq)
    halo = 16
    hb = tm // halo
    cur = lambda col: pl.BlockSpec((tm, D_CONV), lambda i, col=col: (i, col))
    prev = lambda col: pl.BlockSpec((halo, D_CONV),
                                    lambda i, col=col: (jnp.maximum(i * hb - 1, 0), col))
    const = lambda shape: pl.BlockSpec(shape, lambda i: (0, 0))
    return pl.pallas_call(
        functools.partial(_mix_kernel, blocks_per_seq=seq // tm),
        grid=(n // tm,),
        in_specs=[
            cur(0), cur(1), cur(2), prev(1), prev(2),
            pl.BlockSpec((tm, D_MLSTM), lambda i: (i, 0)),
            pl.BlockSpec((tm, D_MODEL), lambda i: (i, 0)),
            const((1, D_MODEL)), const((1, D_MODEL)),
            const((8, D_CONV)), const((1, D_CONV)),
            pl.BlockSpec((D_MODEL, D_MODEL), lambda i: (0, 0), pipeline_mode=pl.Buffered(1)),
            const((1, D_MODEL)), const((1, D_MODEL)),
        ],
        out_specs=[pl.BlockSpec((tm, D_MODEL), lambda i: (i, 0)),
                   pl.BlockSpec((tm, D_MODEL), lambda i: (i, 0))],
        out_shape=[jax.ShapeDtypeStruct((n, D_MODEL), F32),
                   jax.ShapeDtypeStruct((n, D_MODEL), BF16)],
        compiler_params=_cparams(("parallel",)),
        name="mix",
    )(proj, proj, proj, proj, proj, ym, x, g0, b0, conv_w, conv_b, w_out, g, b)


def _qproj_kernel(x_ref, wqt_ref, qt_ref):
    qt_ref[...] = lax.dot_general(wqt_ref[...], x_ref[...], (((1,), (1,)), ((), ())),
                                  preferred_element_type=F32).astype(BF16)


def _qproj(h1b, wqt, tm=1024):
    n = h1b.shape[0]
    tm = min(tm, n)
    dq = wqt.shape[0]
    return pl.pallas_call(
        _qproj_kernel,
        grid=(n // tm,),
        in_specs=[pl.BlockSpec((tm, D_MODEL), lambda i: (i, 0)),
                  pl.BlockSpec((dq, D_MODEL), lambda i: (0, 0), pipeline_mode=pl.Buffered(1))],
        out_specs=pl.BlockSpec((dq, tm), lambda i: (0, i)),
        out_shape=jax.ShapeDtypeStruct((dq, n), BF16),
        compiler_params=_cparams(("parallel",)),
        name="peer_q",
    )(h1b, wqt)


def _oddeven_sort_pairs(n):
    pairs = []
    p = 1
    while p < n:
        k = p
        while k >= 1:
            for j in range(k % p, n - k, 2 * k):
                for i in range(min(k, n - j - k)):
                    if (i + j) // (2 * p) == (i + j + k) // (2 * p):
                        pairs.append((i + j, i + j + k))
            k //= 2
        p *= 2
    return pairs


_SORT16 = _oddeven_sort_pairs(PEER_TOPK)
_BITONIC16 = [(i, i + d) for d in (8, 4, 2, 1) for i in range(PEER_TOPK) if not i & d]


def _compare_exchange(v, ix, pairs):
    for i, j in pairs:
        up = v[j] > v[i]
        v[i], v[j] = jnp.maximum(v[i], v[j]), jnp.minimum(v[i], v[j])
        ix[i], ix[j] = jnp.where(up, ix[j], ix[i]), jnp.where(up, ix[i], ix[j])


def _network_top16(cur_scr, srt_v, srt_i, sv_scr, si_scr, p, lanes):
    K = PEER_TOPK
    for g in range(N_KEYS // K):
        v = [cur_scr[g * K + i, :, lanes] for i in range(K)]
        ix = [jnp.int32(g * K + i) for i in range(K)]
        _compare_exchange(v, ix, _SORT16)
        for i in range(K):
            srt_v[g * K + i] = v[i]
            srt_i[g * K + i] = ix[i]
    dropped = jnp.full(v[0].shape, -jnp.inf, F32)
    span = K
    while span < N_KEYS:
        for a in range(0, N_KEYS, 2 * span):
            b = a + span
            va = [srt_v[a + i] for i in range(K)]
            ia = [srt_i[a + i] for i in range(K)]
            v, ix = [], []
            for i in range(K):
                vb = srt_v[b + K - 1 - i]
                up = vb > va[i]
                v.append(jnp.maximum(va[i], vb))
                ix.append(jnp.where(up, srt_i[b + K - 1 - i], ia[i]))
                dropped = jnp.maximum(dropped, jnp.minimum(va[i], vb))
            _compare_exchange(v, ix, _BITONIC16)
            for i in range(K):
                srt_v[a + i] = v[i]
                srt_i[a + i] = ix[i]
        span *= 2
    distinct = v[K - 1] > dropped
    for i in range(K - 1):
        distinct = distinct & (v[i] > v[i + 1])
    for i in range(K):
        sv_scr[p, i, :, lanes] = v[i]
        si_scr[p, i, :, lanes] = ix[i]
    return jnp.where(distinct, 0, 1)


def _retrieve_kernel(q_ref, kb_ref, eid_ref, gate_ref, cur_scr, sv_scr, si_scr, srt_v, srt_i):
    tb = q_ref.shape[1]
    dh = PEER_HEADS * PEER_HALF

    for p in range(2):
        s = jnp.dot(kb_ref[p], q_ref[p * dh:(p + 1) * dh, :],
                    preferred_element_type=F32)
        cur_scr[...] = s.reshape(N_KEYS, PEER_HEADS, tb)

        def lane_body(lh, tied, p=p):
            lanes = pl.ds(pl.multiple_of(lh * LANES, LANES), LANES)
            return jnp.maximum(tied, _network_top16(cur_scr, srt_v, srt_i, sv_scr, si_scr, p, lanes))

        tied = lax.fori_loop(0, tb // LANES, lane_body, jnp.zeros((PEER_HEADS, LANES), jnp.int32))

        def round_body(k, taken, p=p):
            vals, ids = [], list(range(N_KEYS))
            for n in range(N_KEYS):
                c = jnp.where(taken == n, -jnp.inf, cur_scr[n])
                cur_scr[n] = c
                vals.append(c)
            while len(vals) > 1:
                right = [vals[i + 1] > vals[i] for i in range(0, len(vals), 2)]
                ids = [jnp.where(g, ids[2 * i + 1], ids[2 * i]) for i, g in enumerate(right)]
                vals = [jnp.maximum(vals[2 * i], vals[2 * i + 1]) for i in range(len(right))]
            sv_scr[p, k] = vals[0]
            si_scr[p, k] = ids[0]
            return ids[0]

        @pl.when(jnp.max(tied) > 0)
        def _(round_body=round_body):
            lax.fori_loop(0, PEER_TOPK, round_body, jnp.full((PEER_HEADS, tb), N_KEYS, jnp.int32))

    sv1 = sv_scr[0]
    sv2 = sv_scr[1]
    e1 = si_scr[0] * N_KEYS
    e2 = si_scr[1]
    a_iota = lax.broadcasted_iota(jnp.int32, (PEER_TOPK, PEER_HEADS, tb), 0)
    front = sv1 + sv2[0][None]
    ptr = jnp.zeros_like(a_iota)
    tops, eids = [], []
    for k in range(PEER_TOPK):
        mx = jnp.max(front, axis=0)
        awin = jnp.min(jnp.where(front == mx[None], a_iota, PEER_TOPK), axis=0)
        hit = a_iota == awin[None]
        bwin = jnp.sum(jnp.where(hit, ptr, 0), axis=0)
        eid = (jnp.sum(jnp.where(hit, e1, 0), axis=0)
               + jnp.sum(jnp.where(a_iota == bwin[None], e2, 0), axis=0))
        nxt = jnp.max(jnp.where(a_iota == (bwin + 1)[None], sv2, -jnp.inf), axis=0)
        front = jnp.where(hit, sv1 + nxt[None], front)
        ptr = jnp.where(hit, ptr + 1, ptr)
        tops.append(mx)
        eids.append(eid)
    top_s = jnp.stack(tops)
    e = jnp.exp(top_s - tops[0][None])
    gate = e / jnp.sum(e, axis=0)[None]
    nj = PEER_TOPK * PEER_HEADS
    eid_ref[...] = jnp.stack(eids).reshape(nj, tb).T
    gate_ref[...] = gate.reshape(nj, tb).T


def _retrieve(q, kb, tb=256):
    n = q.shape[1]
    tb = min(tb, n)
    nj = PEER_HEADS * PEER_TOPK
    dq = 2 * PEER_HEADS * PEER_HALF
    return pl.pallas_call(
        _retrieve_kernel,
        grid=(n // tb,),
        in_specs=[
            pl.BlockSpec((dq, tb), lambda i: (0, i)),
            pl.BlockSpec((2, N_KEYS * PEER_HEADS, dq // 2), lambda i: (0, 0, 0),
                         pipeline_mode=pl.Buffered(1)),
        ],
        out_specs=[pl.BlockSpec((tb, nj), lambda i: (i, 0)),
                   pl.BlockSpec((tb, nj), lambda i: (i, 0))],
        out_shape=[jax.ShapeDtypeStruct((n, nj), jnp.int32),
                   jax.ShapeDtypeStruct((n, nj), F32)],
        scratch_shapes=[
            pltpu.VMEM((N_KEYS, PEER_HEADS, tb), F32),
            pltpu.VMEM((2, PEER_TOPK, PEER_HEADS, tb), F32),
            pltpu.VMEM((2, PEER_TOPK, PEER_HEADS, tb), jnp.int32),
            pltpu.VMEM((N_KEYS, PEER_HEADS, LANES), F32),
            pltpu.VMEM((N_KEYS, PEER_HEADS, LANES), jnp.int32),
        ],
        compiler_params=_cparams(("parallel",)),
        name="retrieve",
    )(q, kb)


def _block_diag_keys(keys):
    eye_h = jnp.eye(PEER_HEADS, dtype=keys.dtype)
    kb = jnp.einsum("hpnc,hg->pnhgc", keys, eye_h)
    return kb.reshape(2, N_KEYS * PEER_HEADS, PEER_HEADS * PEER_HALF)


def _half_major_query_weight(wq):
    d = wq.shape[0]
    return wq.reshape(d, PEER_HEADS, 2, PEER_HALF).transpose(2, 1, 3, 0).reshape(-1, d)


_W_GROUP = 16
_W_PITCH = _W_GROUP + 4


def _scatter_w_kernel(eid_ref, gate_ref, w_ref, scr_a, scr_b):
    tb = eid_ref.shape[0]
    nj = eid_ref.shape[1]
    ngrp = tb // _W_GROUP
    sub = lax.broadcasted_iota(jnp.int32, (N_KEYS, nj), 0)

    def build(grp, scr):
        base = pl.multiple_of(grp * _W_GROUP, _W_GROUP)
        for tl in range(_W_GROUP):
            e = eid_ref[pl.ds(base + tl, 1), :]
            g = gate_ref[pl.ds(base + tl, 1), :]
            p = jnp.where(sub == (e >> 7), 1.0, 0.0).astype(BF16)
            q = jnp.where(sub == (e & (N_KEYS - 1)), g, 0.0).astype(BF16)
            wt = lax.dot_general(p, q, (((1,), (1,)), ((), ())), preferred_element_type=F32)
            scr[pl.ds(tl, N_KEYS, stride=_W_PITCH), :] = wt

    def emit(grp, scr):
        base = pl.multiple_of(grp * _W_GROUP, _W_GROUP)
        for r in range(N_KEYS):
            tile = scr[r * _W_PITCH:r * _W_PITCH + _W_GROUP, :]
            w_ref[pl.ds(base, _W_GROUP), r * N_KEYS:(r + 1) * N_KEYS] = tile.astype(BF16)

    build(0, scr_a)

    def pair_body(i, carry):
        build(2 * i + 1, scr_b)
        emit(2 * i, scr_a)
        build(2 * i + 2, scr_a)
        emit(2 * i + 1, scr_b)
        return carry

    lax.fori_loop(0, ngrp // 2 - 1, pair_body, 0)
    build(ngrp - 1, scr_b)
    emit(ngrp - 2, scr_a)
    emit(ngrp - 1, scr_b)


def _scatter_w(eid, gate, tb=128):
    n, nj = eid.shape
    tb = min(tb, n)
    assert tb % (2 * _W_GROUP) == 0
    return pl.pallas_call(
        _scatter_w_kernel,
        grid=(n // tb,),
        in_specs=[pl.BlockSpec((tb, nj), lambda i: (i, 0)),
                  pl.BlockSpec((tb, nj), lambda i: (i, 0))],
        out_specs=pl.BlockSpec((tb, N_EXPERTS), lambda i: (i, 0)),
        out_shape=jax.ShapeDtypeStruct((n, N_EXPERTS), BF16),
        scratch_shapes=[pltpu.VMEM((N_KEYS * _W_PITCH, N_KEYS), F32)] * 2,
        compiler_params=_cparams(("parallel",)),
        name="scatter_w",
    )(eid, gate)


def _experts_kernel(xb_ref, u_ref, v_ref, w_ref, h1_ref, g_ref, b_ref, out_ref):
    k = pl.program_id(1)

    @pl.when(k == 0)
    def _():
        out_ref[...] = ALPHA * h1_ref[...]

    a = lax.dot_general(xb_ref[...], u_ref[...], (((1,), (1,)), ((), ())),
                        preferred_element_type=F32)
    act = 0.5 * a * (1.0 + lax.erf(a * (0.5 ** 0.5)))
    wa = (w_ref[...].astype(F32) * act).astype(BF16)
    out_ref[...] += jnp.dot(wa, v_ref[...], preferred_element_type=F32)

    @pl.when(k == pl.num_programs(1) - 1)
    def _():
        out_ref[...] = _layernorm(out_ref[...], g_ref[...], b_ref[...])


def _experts(h1b, u, v, w, h1, g, b, tm=1024, ec=1024):
    n = h1b.shape[0]
    tm = min(tm, n)
    once = pl.Buffered(1)
    return pl.pallas_call(
        _experts_kernel,
        grid=(n // tm, N_EXPERTS // ec),
        in_specs=[
            pl.BlockSpec((tm, D_MODEL), lambda i, k: (i, 0), pipeline_mode=once),
            pl.BlockSpec((ec, D_MODEL), lambda i, k: (k, 0)),
            pl.BlockSpec((ec, D_MODEL), lambda i, k: (k, 0)),
            pl.BlockSpec((tm, ec), lambda i, k: (i, k)),
            pl.BlockSpec((tm, D_MODEL), lambda i, k: (i, 0), pipeline_mode=once),
            pl.BlockSpec((1, D_MODEL), lambda i, k: (0, 0)),
            pl.BlockSpec((1, D_MODEL), lambda i, k: (0, 0)),
        ],
        out_specs=pl.BlockSpec((tm, D_MODEL), lambda i, k: (i, 0), pipeline_mode=once),
        out_shape=jax.ShapeDtypeStruct((n, D_MODEL), F32),
        compiler_params=_cparams(("parallel", "arbitrary")),
        name="experts",
    )(h1b, u, v, w, h1, g, b)


def kernel(x, ln_in_g, ln_in_b, w_in, b_gate, conv_w, conv_b, mh_norm_g, w_out, ln1_g, ln1_b,
           peer_wq, peer_keys, peer_u, peer_v, ln2_g, ln2_b):
    batch, seq, d = x.shape
    n = batch * seq
    row = lambda t: t.reshape(1, -1).astype(F32)
    l = 0
    w_main = w_in[l, :, :D_MAIN].astype(BF16)
    w_gate = jnp.pad(w_in[l, :, D_MAIN:], ((0, 0), (0, LANES - 2 * M_HEADS))).astype(BF16)

    x2 = x.reshape(n, d)
    proj, gates = _ln_proj(x2, row(ln_in_g), row(ln_in_b), w_main, w_gate, tm=_TILES["proj_tm"])

    r_rows, cola, colb = _gate_scan(gates, b_gate[l].reshape(2 * M_HEADS, 1), tl=_TILES["scan_tl"])
    rowg = r_rows.reshape(M_HEADS, n // CHUNK, CHUNK).transpose(1, 0, 2)
    ym = _mlstm(proj, rowg, cola, colb, row(mh_norm_g[l]), batch, seq, ts=_TILES["mlstm_ts"])

    cw = jnp.pad(conv_w[l], ((0, 8 - conv_w.shape[1]), (0, 0)))
    h1, h1b = _mix(proj, ym, x2, row(ln_in_g), row(ln_in_b), cw, row(conv_b[l]), w_out[l].astype(BF16),
                   row(ln1_g[l]), row(ln1_b[l]), seq, tm=_TILES["mix_tm"])

    kb = _block_diag_keys(peer_keys[l].astype(BF16))
    q = _qproj(h1b, _half_major_query_weight(peer_wq[l].astype(BF16)))
    eid, gate = _retrieve(q, kb, tb=_TILES["retrieve_tb"])
    w = _scatter_w(eid, gate, tb=_TILES["scatter_tb"])
    out = _experts(h1b, peer_u[l].astype(BF16), peer_v[l].astype(BF16), w, h1,
                   row(ln2_g[l]), row(ln2_b[l]), tm=_TILES["experts_tm"], ec=_TILES["experts_ec"])
    return out.reshape(batch, seq, d)
```

```python
import functools

import jax
import jax.numpy as jnp
import numpy as np
from jax import lax
from jax.experimental import pallas as pl
from jax.experimental.pallas import tpu as pltpu

F32 = jnp.float32
BF16 = jnp.bfloat16

D_MODEL = 2048
CHUNK = 64
D_CONV = 1024
D_MLSTM = 1024
M_HEADS = 8
M_HEAD_DIM = 128
D_MAIN = 3 * D_CONV + 4 * D_MLSTM
PEER_HEADS = 8
N_KEYS = 128
N_EXPERTS = N_KEYS * N_KEYS
PEER_TOPK = 16
PEER_HALF = 128
DEPTH = 1
ALPHA = (2 * DEPTH) ** 0.25
LN_EPS = 1e-5
LANES = 128
VMEM_LIMIT = 56 * 1024 * 1024

_TILES = dict(proj_tm=1024, scan_tl=2048, mlstm_ts=1024, mix_tm=512, retrieve_tb=512,
              scatter_tb=512, experts_tm=1024, experts_ec=1024)


def _cparams(sem):
    return pltpu.CompilerParams(dimension_semantics=sem, vmem_limit_bytes=VMEM_LIMIT)


def _layernorm(x, g, b):
    mu = jnp.mean(x, axis=-1, keepdims=True)
    xc = x - mu
    var = jnp.mean(xc * xc, axis=-1, keepdims=True)
    return xc * lax.rsqrt(var + LN_EPS) * g + b


_LN_ROWS = 256


def _ln_proj_kernel(x_ref, g_ref, b_ref, w_ref, wg_ref, proj_ref, gate_ref, hb_ref):
    @pl.when(pl.program_id(1) == 0)
    def _():
        def rows_body(r, carry):
            rows = pl.ds(pl.multiple_of(r * _LN_ROWS, _LN_ROWS), _LN_ROWS)
            hb = _layernorm(x_ref[rows, :], g_ref[...], b_ref[...]).astype(BF16)
            hb_ref[rows, :] = hb
            gate_ref[rows, :] = jnp.dot(hb, wg_ref[...], preferred_element_type=F32)
            return carry

        lax.fori_loop(0, x_ref.shape[0] // _LN_ROWS, rows_body, 0)

    proj_ref[...] = jnp.dot(hb_ref[...], w_ref[...], preferred_element_type=F32).astype(BF16)


def _ln_proj(x, g, b, w_main, w_gate, tm=1024, tn=1792):
    n = x.shape[0]
    tm = min(tm, n)
    return pl.pallas_call(
        _ln_proj_kernel,
        grid=(n // tm, D_MAIN // tn),
        in_specs=[
            pl.BlockSpec((tm, D_MODEL), lambda i, j: (i, 0)),
            pl.BlockSpec((1, D_MODEL), lambda i, j: (0, 0)),
            pl.BlockSpec((1, D_MODEL), lambda i, j: (0, 0)),
            pl.BlockSpec((D_MODEL, tn), lambda i, j: (0, j)),
            pl.BlockSpec((D_MODEL, LANES), lambda i, j: (0, 0)),
        ],
        out_specs=[
            pl.BlockSpec((tm, tn), lambda i, j: (i, j)),
            pl.BlockSpec((tm, LANES), lambda i, j: (i, 0)),
        ],
        out_shape=[
            jax.ShapeDtypeStruct((n, D_MAIN), BF16),
            jax.ShapeDtypeStruct((n, LANES), F32),
        ],
        scratch_shapes=[pltpu.VMEM((tm, D_MODEL), BF16)],
        compiler_params=_cparams(("parallel", "arbitrary")),
        name="ln_proj",
    )(x, g, b, w_main, w_gate)


def _gate_scan_kernel(gt_ref, b_ref, r_ref, cola_ref, colb_ref):
    z = gt_ref[...].T[:2 * M_HEADS] + b_ref[...]
    li = z[:M_HEADS]
    lf = jax.nn.log_sigmoid(z[M_HEADS:])
    pos = lax.broadcasted_iota(jnp.int32, lf.shape, 1) % CHUNK
    bcum = lf
    shift = 1
    while shift < CHUNK:
        bcum = bcum + jnp.where(pos >= shift, pltpu.roll(bcum, shift, axis=1), 0.0)
        shift *= 2
    r = li - bcum
    cm = r
    shift = 1
    while shift < CHUNK:
        cm = jnp.maximum(cm, jnp.where(pos >= shift, pltpu.roll(cm, shift, axis=1), -jnp.inf))
        shift *= 2
    r_ref[...] = r
    zero = jnp.zeros_like(r)
    cola_ref[...] = jnp.concatenate([zero, r, cm, cm] * 3 + [cm, zero, zero, zero], axis=0).T
    colb_ref[...] = jnp.concatenate([zero, zero, bcum, bcum] * 3 + [bcum, zero, zero, zero], axis=0).T


def _gate_scan(gt, bias, tl=2048):
    n = gt.shape[0]
    tl = min(tl, n)
    return pl.pallas_call(
        _gate_scan_kernel,
        grid=(n // tl,),
        in_specs=[pl.BlockSpec((tl, LANES), lambda i: (i, 0)),
                  pl.BlockSpec((2 * M_HEADS, 1), lambda i: (0, 0))],
        out_specs=[pl.BlockSpec((M_HEADS, tl), lambda i: (0, i)),
                   pl.BlockSpec((tl, LANES), lambda i: (i, 0)),
                   pl.BlockSpec((tl, LANES), lambda i: (i, 0))],
        out_shape=[jax.ShapeDtypeStruct((M_HEADS, n), F32),
                   jax.ShapeDtypeStruct((n, LANES), F32),
                   jax.ShapeDtypeStruct((n, LANES), F32)],
        compiler_params=_cparams(("parallel",)),
        name="gate_scan",
    )(gt, bias)


def _mlstm_kernel(q_ref, k_ref, v_ref, o_ref, row_ref, cola_ref, colb_ref, spread_ref, ng_ref,
                  y_ref, cn_ref, m_ref, *, ts):
    @pl.when(pl.program_id(1) == 0)
    def _():
        cn_ref[...] = jnp.zeros_like(cn_ref)
        m_ref[...] = jnp.zeros_like(m_ref)

    L = CHUNK
    dh = M_HEAD_DIM
    scale = dh ** -0.5
    ones_v = jnp.ones((M_HEADS, L, dh), BF16)
    lane_mean = jnp.full((M_HEADS, dh, dh), 1.0 / dh, BF16)
    li = lax.broadcasted_iota(jnp.int32, (L, L), 0)
    lj = lax.broadcasted_iota(jnp.int32, (L, L), 1)
    causal = lj <= li
    lane = lax.broadcasted_iota(jnp.int32, (L, LANES), 1)
    lane_grp = lane // _COL_GROUP
    lane_in = lane % _COL_GROUP
    is_wk = (lane_grp < 3) & (lane_in >= M_HEADS) & (lane_in < 2 * M_HEADS)
    is_a = (lane_grp < 3) & (lane_in >= 2 * M_HEADS) & (lane_in < 3 * M_HEADS)
    is_wi = (lane_grp < 3) & (lane_in >= 3 * M_HEADS)

    def chunk_body(c, carry):
        t0 = pl.multiple_of(c * L, L)
        rows = pl.ds(t0, L)
        rowg = row_ref[c]
        ca = cola_ref[rows, :]
        cb = colb_ref[rows, :]
        m_row = m_ref[...]
        a_l = jnp.maximum(m_row, ca)
        w_inter_l = jnp.exp(m_row - a_l)
        floor_l = jnp.exp(-(cb + a_l))
        a_last = a_l[L - 1:L, :]
        wk_l = jnp.exp(ca - pltpu.roll(a_last, LANES - M_HEADS, axis=1))
        decay_row = jnp.exp(m_row - a_last)
        m_ref[...] = cb[L - 1:L, :] + a_last
        heads = range(M_HEADS)
        per_head = lambda ref: jnp.stack([ref[rows, h * dh:(h + 1) * dh] for h in heads])
        bmm = lambda x, y, cx, cy: lax.dot_general(
            x, y, (((cx,), (cy,)), ((0,), (0,))), preferred_element_type=F32)
        x = jnp.where(is_wk, wk_l, jnp.where(is_a, a_l, jnp.where(is_wi, w_inter_l, 0.0)))
        x_hi = x.astype(BF16)
        x_r1 = x - x_hi.astype(F32)
        x_mid = x_r1.astype(BF16)
        x_lo = (x_r1 - x_mid.astype(F32)).astype(BF16)
        parts = jnp.where(lane_grp == 0, x_hi, jnp.where(lane_grp == 1, x_mid, x_lo))
        rep = jnp.dot(parts, spread_ref[...], preferred_element_type=F32)
        wide = lambda qi: jnp.stack([rep[:, (qi * M_HEADS + h) * dh:(qi * M_HEADS + h + 1) * dh]
                                     for h in heads])
        a = wide(0)[..., :L]
        w_inter = wide(1)
        wk_wide = wide(2)
        floor = jnp.stack([floor_l[:, 3 * 32 + h:3 * 32 + h + 1] for h in heads])
        decay = jnp.stack([decay_row[:, 2 * M_HEADS + h:2 * M_HEADS + h + 1] for h in heads])
        q = per_head(q_ref)
        k = per_head(k_ref)
        v = per_head(v_ref)
        w_intra = jnp.exp(jnp.where(causal[None], rowg[:, None, :] - a, -jnp.inf))
        s = bmm(q, k, 2, 2) * scale * w_intra
        s_hi = s.astype(BF16)
        s_lo = (s - s_hi.astype(F32)).astype(BF16)
        sv = bmm(s_hi, jnp.concatenate([v, ones_v], axis=-1), 2, 1)
        cn = cn_ref[...]
        qc = bmm(q, cn.astype(BF16), 2, 1)
        num = sv[..., :dh] + w_inter * qc[..., :dh]
        den = sv[..., dh:] + bmm(s_lo, ones_v, 2, 1) + w_inter * qc[..., dh:]
        hval = num / jnp.maximum(jnp.abs(den), floor)
        wkv =jnp.concatenate([wk_wide * v.astype(F32), wk_wide], axis=-1).astype(BF16)
        cn_ref[...] = decay * cn + scale * bmm(k, wkv, 1, 1)
        mu = bmm(hval.astype(BF16), lane_mean, 2, 1)
        hc = hval - mu
        var = bmm((hc * hc).astype(BF16), lane_mean, 2, 1)
        hn = hc * lax.rsqrt(var + LN_EPS)
        for h in heads:
            hs = slice(h * dh, (h + 1) * dh)
            og = jax.nn.sigmoid(o_ref[rows, hs].astype(F32))
            y_ref[rows, hs] = (og * hn[h] * ng_ref[:, hs]).astype(BF16)
        return carry

    lax.fori_loop(0, ts // L, chunk_body, 0, unroll=4)


_COL_GROUP = 32


def _spread_matrix():
    k = np.arange(LANES)
    grp, slot, head = k // _COL_GROUP, (k % _COL_GROUP) // M_HEADS, k % M_HEADS
    block = ((slot + 1) % 3) * M_HEADS + head
    valid = (grp < 3) & (slot >= 1)
    cols = np.arange(3 * M_HEADS * M_HEAD_DIM) // M_HEAD_DIM
    return jnp.asarray((block[:, None] == cols[None, :]) & valid[:, None], dtype=BF16)


def _mlstm(proj, rowg, cola, colb, norm_g, batch, seq, ts=1024):
    n = proj.shape[0]
    ts = min(ts, seq)
    nb = seq // ts
    blk = lambda col: pl.BlockSpec((ts, D_MLSTM), lambda b, i, col=col: (b * nb + i, col))
    return pl.pallas_call(
        functools.partial(_mlstm_kernel, ts=ts),
        grid=(batch, nb),
        in_specs=[
            blk(3), blk(4), blk(5), blk(6),
            pl.BlockSpec((ts // CHUNK, M_HEADS, CHUNK), lambda b, i: (b * nb + i, 0, 0)),
            pl.BlockSpec((ts, LANES), lambda b, i: (b * nb + i, 0)),
            pl.BlockSpec((ts, LANES), lambda b, i: (b * nb + i, 0)),
            pl.BlockSpec((LANES, 3 * M_HEADS * M_HEAD_DIM), lambda b, i: (0, 0),
                         pipeline_mode=pl.Buffered(1)),
            pl.BlockSpec((1, D_MLSTM), lambda b, i: (0, 0)),
        ],
        out_specs=pl.BlockSpec((ts, D_MLSTM), lambda b, i: (b * nb + i, 0)),
        out_shape=jax.ShapeDtypeStruct((n, D_MLSTM), BF16),
        scratch_shapes=[
            pltpu.VMEM((M_HEADS, M_HEAD_DIM, 2 * M_HEAD_DIM), F32),
            pltpu.VMEM((1, LANES), F32),
        ],
        compiler_params=_cparams(("parallel", "arbitrary")),
        name="mlstm",
    )(proj, proj, proj, proj, rowg, cola, colb, _spread_matrix(), norm_g)


def _mix_kernel(cb_ref, cc_ref, ch_ref, pc_ref, ph_ref, ym_ref, x_ref, g0_ref, b0_ref, cw_ref,
                cbias_ref, wo_ref, g_ref, b_ref, h1_ref, h1b_ref, *, blocks_per_seq):
    tm = cb_ref.shape[0]
    z = cc_ref[...].astype(F32) * ch_ref[...].astype(F32)
    halo = pc_ref[...].astype(F32) * ph_ref[...].astype(F32)
    first = (pl.program_id(0) % blocks_per_seq) == 0
    halo = jnp.where(first, 0.0, halo)
    row = lax.broadcasted_iota(jnp.int32, z.shape, 0)
    hr = halo.shape[0]
    z1 = jnp.where(row == 0, halo[hr - 1:hr, :], pltpu.roll(z, 1, axis=0))
    z2 = pltpu.roll(z, 2, axis=0)
    z2 = jnp.where(row == 0, halo[hr - 2:hr - 1, :], z2)
    z2 = jnp.where(row == 1, halo[hr - 1:hr, :], z2)
    conv = cbias_ref[...] + cw_ref[0:1, :] * z2 + cw_ref[1:2, :] * z1 + cw_ref[2:3, :] * z
    y_conv = (cb_ref[...].astype(F32) * conv).astype(BF16)
    mix = (jnp.dot(y_conv, wo_ref[0:D_CONV, :], preferred_element_type=F32)
           + jnp.dot(ym_ref[...], wo_ref[D_CONV:, :], preferred_element_type=F32))
    h = _layernorm(x_ref[...], g0_ref[...], b0_ref[...])
    h1 = _layernorm(ALPHA * h + mix, g_ref[...], b_ref[...])
    h1_ref[...] = h1
    h1b_ref[...] = h1.astype(BF16)


def _mix(proj, ym, x, g0, b0, conv_w, conv_b, w_out, g, b, seq, tm=512):
    n = proj.shape[0]
    tm = min(tm, seq)
    halo = 16
    hb = tm // halo
    cur = lambda col: pl.BlockSpec((tm, D_CONV), lambda i, col=col: (i, col))
    prev = lambda col: pl.BlockSpec((halo, D_CONV),
                                    lambda i, col=col: (jnp.maximum(i * hb - 1, 0), col))
    const = lambda shape: pl.BlockSpec(shape, lambda i: (0, 0))
    return pl.pallas_call(
        functools.partial(_mix_kernel, blocks_per_seq=seq // tm),
        grid=(n // tm,),
        in_specs=[
            cur(0), cur(1), cur(2), prev(1), prev(2),
            pl.BlockSpec((tm, D_MLSTM), lambda i: (i, 0)),
            pl.BlockSpec((tm, D_MODEL), lambda i: (i, 0)),
            const((1, D_MODEL)), const((1, D_MODEL)),
            const((8, D_CONV)), const((1, D_CONV)),
            pl.BlockSpec((D_MODEL, D_MODEL), lambda i: (0, 0), pipeline_mode=pl.Buffered(1)),
            const((1, D_MODEL)), const((1, D_MODEL)),
        ],
        out_specs=[pl.BlockSpec((tm, D_MODEL), lambda i: (i, 0)),
                   pl.BlockSpec((tm, D_MODEL), lambda i: (i, 0))],
        out_shape=[jax.ShapeDtypeStruct((n, D_MODEL), F32),
                   jax.ShapeDtypeStruct((n, D_MODEL), BF16)],
        compiler_params=_cparams(("parallel",)),
        name="mix",
    )(proj, proj, proj, proj, proj, ym, x, g0, b0, conv_w, conv_b, w_out, g, b)


def _qproj_kernel(x_ref, wqt_ref, qt_ref):
    qt_ref[...] = lax.dot_general(wqt_ref[...], x_ref[...], (((1,), (1,)), ((), ())),
                                  preferred_element_type=F32).astype(BF16)


def _qproj(h1b, wqt, tm=1024):
    n = h1b.shape[0]
    tm = min(tm, n)
    dq = wqt.shape[0]
    return pl.pallas_call(
        _qproj_kernel,
        grid=(n // tm,),
        in_specs=[pl.BlockSpec((tm, D_MODEL), lambda i: (i, 0)),
                  pl.BlockSpec((dq, D_MODEL), lambda i: (0, 0), pipeline_mode=pl.Buffered(1))],
        out_specs=pl.BlockSpec((dq, tm), lambda i: (0, i)),
        out_shape=jax.ShapeDtypeStruct((dq, n), BF16),
        compiler_params=_cparams(("parallel",)),
        name="peer_q",
    )(h1b, wqt)


def _oddeven_sort_pairs(n):
    pairs = []
    p = 1
    while p < n:
        k = p
        while k >= 1:
            for j in range(k % p, n - k, 2 * k):
                for i in range(min(k, n - j - k)):
                    if (i + j) // (2 * p) == (i + j + k) // (2 * p):
                        pairs.append((i + j, i + j + k))
            k //= 2
        p *= 2
    return pairs


_SORT16 = _oddeven_sort_pairs(PEER_TOPK)
_BITONIC16 = [(i, i + d) for d in (8, 4, 2, 1) for i in range(PEER_TOPK) if not i & d]


def _compare_exchange(v, ix, pairs):
    for i, j in pairs:
        up = v[j] > v[i]
        v[i], v[j] = jnp.maximum(v[i], v[j]), jnp.minimum(v[i], v[j])
        ix[i], ix[j] = jnp.where(up, ix[j], ix[i]), jnp.where(up, ix[i], ix[j])


def _network_top16(cur_scr, srt_v, srt_i, sv_scr, si_scr, p, lanes):
    K = PEER_TOPK
    for g in range(N_KEYS // K):
        v = [cur_scr[g * K + i, :, lanes] for i in range(K)]
        ix = [jnp.int32(g * K + i) for i in range(K)]
        _compare_exchange(v, ix, _SORT16)
        for i in range(K):
            srt_v[g * K + i] = v[i]
            srt_i[g * K + i] = ix[i]
    dropped = jnp.full(v[0].shape, -jnp.inf, F32)
    span = K
    while span < N_KEYS:
        for a in range(0, N_KEYS, 2 * span):
            b = a + span
            va = [srt_v[a + i] for i in range(K)]
            ia = [srt_i[a + i] for i in range(K)]
            v, ix = [], []
            for i in range(K):
                vb = srt_v[b + K - 1 - i]
                up = vb > va[i]
                v.append(jnp.maximum(va[i], vb))
                ix.append(jnp.where(up, srt_i[b + K - 1 - i], ia[i]))
                dropped = jnp.maximum(dropped, jnp.minimum(va[i], vb))
            _compare_exchange(v, ix, _BITONIC16)
            for i in range(K):
                srt_v[a + i] = v[i]
                srt_i[a + i] = ix[i]
        span *= 2
    distinct = v[K - 1] > dropped
    for i in range(K - 1):
        distinct = distinct & (v[i] > v[i + 1])
    for i in range(K):
        sv_scr[p, i, :, lanes] = v[i]
        si_scr[p, i, :, lanes] = ix[i]
    return jnp.where(distinct, 0, 1)


def _retrieve_kernel(q_ref, kb_ref, eid_ref, gate_ref, cur_scr, sv_scr, si_scr, srt_v, srt_i):
    tb = q_ref.shape[1]
    dh = PEER_HEADS * PEER_HALF

    for p in range(2):
        s = jnp.dot(kb_ref[p], q_ref[p * dh:(p + 1) * dh, :],
                    preferred_element_type=F32)
        cur_scr[...] = s.reshape(N_KEYS, PEER_HEADS, tb)

        def lane_body(lh, tied, p=p):
            lanes = pl.ds(pl.multiple_of(lh * LANES, LANES), LANES)
            return jnp.maximum(tied, _network_top16(cur_scr, srt_v, srt_i, sv_scr, si_scr, p, lanes))

        tied = lax.fori_loop(0, tb // LANES, lane_body, jnp.zeros((PEER_HEADS, LANES), jnp.int32))

        def round_body(k, taken, p=p):
            vals, ids = [], list(range(N_KEYS))
            for n in range(N_KEYS):
                c = jnp.where(taken == n, -jnp.inf, cur_scr[n])
                cur_scr[n] = c
                vals.append(c)
            while len(vals) > 1:
                right = [vals[i + 1] > vals[i] for i in range(0, len(vals), 2)]
                ids = [jnp.where(g, ids[2 * i + 1], ids[2 * i]) for i, g in enumerate(right)]
                vals = [jnp.maximum(vals[2 * i], vals[2 * i + 1]) for i in range(len(right))]
            sv_scr[p, k] = vals[0]
            si_scr[p, k] = ids[0]
            return ids[0]

        @pl.when(jnp.max(tied) > 0)
        def _(round_body=round_body):
            lax.fori_loop(0, PEER_TOPK, round_body, jnp.full((PEER_HEADS, tb), N_KEYS, jnp.int32))

    sv1 = sv_scr[0]
    sv2 = sv_scr[1]
    e1 = si_scr[0] * N_KEYS
    e2 = si_scr[1]
    a_iota = lax.broadcasted_iota(jnp.int32, (PEER_TOPK, PEER_HEADS, tb), 0)
    front = sv1 + sv2[0][None]
    ptr = jnp.zeros_like(a_iota)
    tops, eids = [], []
    for k in range(PEER_TOPK):
        mx = jnp.max(front, axis=0)
        awin = jnp.min(jnp.where(front == mx[None], a_iota, PEER_TOPK), axis=0)
        hit = a_iota == awin[None]
        bwin = jnp.sum(jnp.where(hit, ptr, 0), axis=0)
        eid = (jnp.sum(jnp.where(hit, e1, 0), axis=0)
               + jnp.sum(jnp.where(a_iota == bwin[None], e2, 0), axis=0))
        nxt = jnp.max(jnp.where(a_iota == (bwin + 1)[None], sv2, -jnp.inf), axis=0)
        front = jnp.where(hit, sv1 + nxt[None], front)
        ptr = jnp.where(hit, ptr + 1, ptr)
        tops.append(mx)
        eids.append(eid)
    top_s = jnp.stack(tops)
    e = jnp.exp(top_s - tops[0][None])
    gate = e / jnp.sum(e, axis=0)[None]
    nj = PEER_TOPK * PEER_HEADS
    eid_ref[...] = jnp.stack(eids).reshape(nj, tb).T
    gate_ref[...] = gate.reshape(nj, tb).T


def _retrieve(q, kb, tb=256):
    n = q.shape[1]
    tb = min(tb, n)
    nj = PEER_HEADS * PEER_TOPK
    dq = 2 * PEER_HEADS * PEER_HALF
    return pl.pallas_call(
        _retrieve_kernel,
        grid=(n // tb,),
        in_specs=[
            pl.BlockSpec((dq, tb), lambda i: (0, i)),
            pl.BlockSpec((2, N_KEYS * PEER_HEADS, dq // 2), lambda i: (0, 0, 0),
                         pipeline_mode=pl.Buffered(1)),
        ],
        out_specs=[pl.BlockSpec((tb, nj), lambda i: (i, 0)),
                   pl.BlockSpec((tb, nj), lambda i: (i, 0))],
        out_shape=[jax.ShapeDtypeStruct((n, nj), jnp.int32),
                   jax.ShapeDtypeStruct((n, nj), F32)],
        scratch_shapes=[
            pltpu.VMEM((N_KEYS, PEER_HEADS, tb), F32),
            pltpu.VMEM((2, PEER_TOPK, PEER_HEADS, tb), F32),
            pltpu.VMEM((2, PEER_TOPK, PEER_HEADS, tb), jnp.int32),
            pltpu.VMEM((N_KEYS, PEER_HEADS, LANES), F32),
            pltpu.VMEM((N_KEYS, PEER_HEADS, LANES), jnp.int32),
        ],
        compiler_params=_cparams(("parallel",)),
        name="retrieve",
    )(q, kb)


def _block_diag_keys(keys):
    eye_h = jnp.eye(PEER_HEADS, dtype=keys.dtype)
    kb = jnp.einsum("hpnc,hg->pnhgc", keys, eye_h)
    return kb.reshape(2, N_KEYS * PEER_HEADS, PEER_HEADS * PEER_HALF)


def _half_major_query_weight(wq):
    d = wq.shape[0]
    return wq.reshape(d, PEER_HEADS, 2, PEER_HALF).transpose(2, 1, 3, 0).reshape(-1, d)


_W_GROUP = 16
_W_PITCH = _W_GROUP + 4


def _scatter_w_kernel(eid_ref, gate_ref, w_ref, scr_a, scr_b):
    tb = eid_ref.shape[0]
    nj = eid_ref.shape[1]
    ngrp = tb // _W_GROUP
    sub = lax.broadcasted_iota(jnp.int32, (N_KEYS, nj), 0)

    def build(grp, scr):
        base = pl.multiple_of(grp * _W_GROUP, _W_GROUP)
        for tl in range(_W_GROUP):
            e = eid_ref[pl.ds(base + tl, 1), :]
            g = gate_ref[pl.ds(base + tl, 1), :]
            p = jnp.where(sub == (e >> 7), 1.0, 0.0).astype(BF16)
            q = jnp.where(sub == (e & (N_KEYS - 1)), g, 0.0).astype(BF16)
            wt = lax.dot_general(p, q, (((1,), (1,)), ((), ())), preferred_element_type=F32)
            scr[pl.ds(tl, N_KEYS, stride=_W_PITCH), :] = wt

    def emit(grp, scr):
        base = pl.multiple_of(grp * _W_GROUP, _W_GROUP)
        for r in range(N_KEYS):
            tile = scr[r * _W_PITCH:r * _W_PITCH + _W_GROUP, :]
            w_ref[pl.ds(base, _W_GROUP), r * N_KEYS:(r + 1) * N_KEYS] = tile.astype(BF16)

    build(0, scr_a)

    def pair_body(i, carry):
        build(2 * i + 1, scr_b)
        emit(2 * i, scr_a)
        build(2 * i + 2, scr_a)
        emit(2 * i + 1, scr_b)
        return carry

    lax.fori_loop(0, ngrp // 2 - 1, pair_body, 0)
    build(ngrp - 1, scr_b)
    emit(ngrp - 2, scr_a)
    emit(ngrp - 1, scr_b)


def _scatter_w(eid, gate, tb=128):
    n, nj = eid.shape
    tb = min(tb, n)
    assert tb % (2 * _W_GROUP) == 0
    return pl.pallas_call(
        _scatter_w_kernel,
        grid=(n // tb,),
        in_specs=[pl.BlockSpec((tb, nj), lambda i: (i, 0)),
                  pl.BlockSpec((tb, nj), lambda i: (i, 0))],
        out_specs=pl.BlockSpec((tb, N_EXPERTS), lambda i: (i, 0)),
        out_shape=jax.ShapeDtypeStruct((n, N_EXPERTS), BF16),
        scratch_shapes=[pltpu.VMEM((N_KEYS * _W_PITCH, N_KEYS), F32)] * 2,
        compiler_params=_cparams(("parallel",)),
        name="scatter_w",
    )(eid, gate)


def _experts_kernel(xb_ref, u_ref, v_ref, w_ref, h1_ref, g_ref, b_ref, out_ref):
    k = pl.program_id(1)

    @pl.when(k == 0)
    def _():
        out_ref[...] = ALPHA * h1_ref[...]

    a = lax.dot_general(xb_ref[...], u_ref[...], (((1,), (1,)), ((), ())),
                        preferred_element_type=F32)
    act = 0.5 * a * (1.0 + lax.erf(a * (0.5 ** 0.5)))
    wa = (w_ref[...].astype(F32) * act).astype(BF16)
    out_ref[...] += jnp.dot(wa, v_ref[...], preferred_element_type=F32)

    @pl.when(k == pl.num_programs(1) - 1)
    def _():
        out_ref[...] = _layernorm(out_ref[...], g_ref[...], b_ref[...])


def _experts(h1b, u, v, w, h1, g, b, tm=1024, ec=1024):
    n = h1b.shape[0]
    tm = min(tm, n)
    once = pl.Buffered(1)
    return pl.pallas_call(
        _experts_kernel,
        grid=(n // tm, N_EXPERTS // ec),
        in_specs=[
            pl.BlockSpec((tm, D_MODEL), lambda i, k: (i, 0), pipeline_mode=once),
            pl.BlockSpec((ec, D_MODEL), lambda i, k: (k, 0)),
            pl.BlockSpec((ec, D_MODEL), lambda i, k: (k, 0)),
            pl.BlockSpec((tm, ec), lambda i, k: (i, k)),
            pl.BlockSpec((tm, D_MODEL), lambda i, k: (i, 0), pipeline_mode=once),
            pl.BlockSpec((1, D_MODEL), lambda i, k: (0, 0)),
            pl.BlockSpec((1, D_MODEL), lambda i, k: (0, 0)),
        ],
        out_specs=pl.BlockSpec((tm, D_MODEL), lambda i, k: (i, 0), pipeline_mode=once),
        out_shape=jax.ShapeDtypeStruct((n, D_MODEL), F32),
        compiler_params=_cparams(("parallel", "arbitrary")),
        name="experts",
    )(h1b, u, v, w, h1, g, b)


def kernel(x, ln_in_g, ln_in_b, w_in, b_gate, conv_w, conv_b, mh_norm_g, w_out, ln1_g, ln1_b,
           peer_wq, peer_keys, peer_u, peer_v, ln2_g, ln2_b):
    batch, seq, d = x.shape
    n = batch * seq
    row = lambda t: t.reshape(1, -1).astype(F32)
    l = 0
    w_main = w_in[l, :, :D_MAIN].astype(BF16)
    w_gate = jnp.pad(w_in[l, :, D_MAIN:], ((0, 0), (0, LANES - 2 * M_HEADS))).astype(BF16)

    x2 = x.reshape(n, d)
    proj, gates = _ln_proj(x2, row(ln_in_g), row(ln_in_b), w_main, w_gate, tm=_TILES["proj_tm"])

    r_rows, cola, colb = _gate_scan(gates, b_gate[l].reshape(2 * M_HEADS, 1), tl=_TILES["scan_tl"])
    rowg = r_rows.reshape(M_HEADS, n // CHUNK, CHUNK).transpose(1, 0, 2)
    ym = _mlstm(proj, rowg, cola, colb, row(mh_norm_g[l]), batch, seq, ts=_TILES["mlstm_ts"])

    cw = jnp.pad(conv_w[l], ((0, 8 - conv_w.shape[1]), (0, 0)))
    h1, h1b = _mix(proj, ym, x2, row(ln_in_g), row(ln_in_b), cw, row(conv_b[l]), w_out[l].astype(BF16),
                   row(ln1_g[l]), row(ln1_b[l]), seq, tm=_TILES["mix_tm"])

    kb = _block_diag_keys(peer_keys[l].astype(BF16))
    q = _qproj(h1b, _half_major_query_weight(peer_wq[l].astype(BF16)))
    eid, gate = _retrieve(q, kb, tb=_TILES["retrieve_tb"])
    w = _scatter_w(eid, gate, tb=_TILES["scatter_tb"])
    out = _experts(h1b, peer_u[l].astype(BF16), peer_v[l].astype(BF16), w, h1,
                   row(ln2_g[l]), row(ln2_b[l]), tm=_TILES["experts_tm"], ec=_TILES["experts_ec"])
    return out.reshape(batch, seq, d)
```

```python
import functools

import jax
import jax.numpy as jnp
import numpy as np
from jax import lax
from jax.experimental import pallas as pl
from jax.experimental.pallas import tpu as pltpu

F32 = jnp.float32
BF16 = jnp.bfloat16

D_MODEL = 2048
CHUNK = 64
D_CONV = 1024
D_MLSTM = 1024
M_HEADS = 8
M_HEAD_DIM = 128
D_MAIN = 3 * D_CONV + 4 * D_MLSTM
PEER_HEADS = 8
N_KEYS = 128
N_EXPERTS = N_KEYS * N_KEYS
PEER_TOPK = 16
PEER_HALF = 128
DEPTH = 1
ALPHA = (2 * DEPTH) ** 0.25
LN_EPS = 1e-5
LANES = 128
VMEM_LIMIT = 56 * 1024 * 1024

_TILES = dict(proj_tm=1024, scan_tl=2048, mlstm_ts=1024, mix_tm=512, retrieve_tb=512,
              scatter_tb=512, experts_tm=1024, experts_ec=1024)


def _cparams(sem):
    return pltpu.CompilerParams(dimension_semantics=sem, vmem_limit_bytes=VMEM_LIMIT)


def _layernorm(x, g, b):
    mu = jnp.mean(x, axis=-1, keepdims=True)
    xc = x - mu
    var = jnp.mean(xc * xc, axis=-1, keepdims=True)
    return xc * lax.rsqrt(var + LN_EPS) * g + b


_LN_ROWS = 256


def _ln_proj_kernel(x_ref, g_ref, b_ref, w_ref, wg_ref, proj_ref, gate_ref, hb_ref):
    @pl.when(pl.program_id(1) == 0)
    def _():
        def rows_body(r, carry):
            rows = pl.ds(pl.multiple_of(r * _LN_ROWS, _LN_ROWS), _LN_ROWS)
            hb = _layernorm(x_ref[rows, :], g_ref[...], b_ref[...]).astype(BF16)
            hb_ref[rows, :] = hb
            gate_ref[rows, :] = jnp.dot(hb, wg_ref[...], preferred_element_type=F32)
            return carry

        lax.fori_loop(0, x_ref.shape[0] // _LN_ROWS, rows_body, 0)

    proj_ref[...] = jnp.dot(hb_ref[...], w_ref[...], preferred_element_type=F32).astype(BF16)


def _ln_proj(x, g, b, w_main, w_gate, tm=1024, tn=1792):
    n = x.shape[0]
    tm = min(tm, n)
    return pl.pallas_call(
        _ln_proj_kernel,
        grid=(n // tm, D_MAIN // tn),
        in_specs=[
            pl.BlockSpec((tm, D_MODEL), lambda i, j: (i, 0)),
            pl.BlockSpec((1, D_MODEL), lambda i, j: (0, 0)),
            pl.BlockSpec((1, D_MODEL), lambda i, j: (0, 0)),
            pl.BlockSpec((D_MODEL, tn), lambda i, j: (0, j)),
            pl.BlockSpec((D_MODEL, LANES), lambda i, j: (0, 0)),
        ],
        out_specs=[
            pl.BlockSpec((tm, tn), lambda i, j: (i, j)),
            pl.BlockSpec((tm, LANES), lambda i, j: (i, 0)),
        ],
        out_shape=[
            jax.ShapeDtypeStruct((n, D_MAIN), BF16),
            jax.ShapeDtypeStruct((n, LANES), F32),
        ],
        scratch_shapes=[pltpu.VMEM((tm, D_MODEL), BF16)],
        compiler_params=_cparams(("parallel", "arbitrary")),
        name="ln_proj",
    )(x, g, b, w_main, w_gate)


def _gate_scan_kernel(gt_ref, b_ref, r_ref, cola_ref, colb_ref):
    z = gt_ref[...].T[:2 * M_HEADS] + b_ref[...]
    li = z[:M_HEADS]
    lf = jax.nn.log_sigmoid(z[M_HEADS:])
    pos = lax.broadcasted_iota(jnp.int32, lf.shape, 1) % CHUNK
    bcum = lf
    shift = 1
    while shift < CHUNK:
        bcum = bcum + jnp.where(pos >= shift, pltpu.roll(bcum, shift, axis=1), 0.0)
        shift *= 2
    r = li - bcum
    cm = r
    shift = 1
    while shift < CHUNK:
        cm = jnp.maximum(cm, jnp.where(pos >= shift, pltpu.roll(cm, shift, axis=1), -jnp.inf))
        shift *= 2
    r_ref[...] = r
    zero = jnp.zeros_like(r)
    cola_ref[...] = jnp.concatenate([zero, r, cm, cm] * 3 + [cm, zero, zero, zero], axis=0).T
    colb_ref[...] = jnp.concatenate([zero, zero, bcum, bcum] * 3 + [bcum, zero, zero, zero], axis=0).T


def _gate_scan(gt, bias, tl=2048):
    n = gt.shape[0]
    tl = min(tl, n)
    return pl.pallas_call(
        _gate_scan_kernel,
        grid=(n // tl,),
        in_specs=[pl.BlockSpec((tl, LANES), lambda i: (i, 0)),
                  pl.BlockSpec((2 * M_HEADS, 1), lambda i: (0, 0))],
        out_specs=[pl.BlockSpec((M_HEADS, tl), lambda i: (0, i)),
                   pl.BlockSpec((tl, LANES), lambda i: (i, 0)),
                   pl.BlockSpec((tl, LANES), lambda i: (i, 0))],
        out_shape=[jax.ShapeDtypeStruct((M_HEADS, n), F32),
                   jax.ShapeDtypeStruct((n, LANES), F32),
                   jax.ShapeDtypeStruct((n, LANES), F32)],
        compiler_params=_cparams(("parallel",)),
        name="gate_scan",
    )(gt, bias)


def _mlstm_kernel(q_ref, k_ref, v_ref, o_ref, row_ref, cola_ref, colb_ref, spread_ref, ng_ref,
                  y_ref, cn_ref, m_ref, *, ts):
    @pl.when(pl.program_id(1) == 0)
    def _():
        cn_ref[...] = jnp.zeros_like(cn_ref)
        m_ref[...] = jnp.zeros_like(m_ref)

    L = CHUNK
    dh = M_HEAD_DIM
    scale = dh ** -0.5
    ones_v = jnp.ones((M_HEADS, L, dh), BF16)
    lane_mean = jnp.full((M_HEADS, dh, dh), 1.0 / dh, BF16)
    li = lax.broadcasted_iota(jnp.int32, (L, L), 0)
    lj = lax.broadcasted_iota(jnp.int32, (L, L), 1)
    causal = lj <= li
    lane = lax.broadcasted_iota(jnp.int32, (L, LANES), 1)
    lane_grp = lane // _COL_GROUP
    lane_in = lane % _COL_GROUP
    is_wk = (lane_grp < 3) & (lane_in >= M_HEADS) & (lane_in < 2 * M_HEADS)
    is_a = (lane_grp < 3) & (lane_in >= 2 * M_HEADS) & (lane_in < 3 * M_HEADS)
    is_wi = (lane_grp < 3) & (lane_in >= 3 * M_HEADS)

    def chunk_body(c, carry):
        t0 = pl.multiple_of(c * L, L)
        rows = pl.ds(t0, L)
        rowg = row_ref[c]
        ca = cola_ref[rows, :]
        cb = colb_ref[rows, :]
        m_row = m_ref[...]
        a_l = jnp.maximum(m_row, ca)
        w_inter_l = jnp.exp(m_row - a_l)
        floor_l = jnp.exp(-(cb + a_l))
        a_last = a_l[L - 1:L, :]
        wk_l = jnp.exp(ca - pltpu.roll(a_last, LANES - M_HEADS, axis=1))
        decay_row = jnp.exp(m_row - a_last)
        m_ref[...] = cb[L - 1:L, :] + a_last
        heads = range(M_HEADS)
        per_head = lambda ref: jnp.stack([ref[rows, h * dh:(h + 1) * dh] for h in heads])
        bmm = lambda x, y, cx, cy: lax.dot_general(
            x, y, (((cx,), (cy,)), ((0,), (0,))), preferred_element_type=F32)
        x = jnp.where(is_wk, wk_l, jnp.where(is_a, a_l, jnp.where(is_wi, w_inter_l, 0.0)))
        x_hi = x.astype(BF16)
        x_r1 = x - x_hi.astype(F32)
        x_mid = x_r1.astype(BF16)
        x_lo = (x_r1 - x_mid.astype(F32)).astype(BF16)
        parts = jnp.where(lane_grp == 0, x_hi, jnp.where(lane_grp == 1, x_mid, x_lo))
        rep = jnp.dot(parts, spread_ref[...], preferred_element_type=F32)
        wide = lambda qi: jnp.stack([rep[:, (qi * M_HEADS + h) * dh:(qi * M_HEADS + h + 1) * dh]
                                     for h in heads])
        a = wide(0)[..., :L]
        w_inter = wide(1)
        wk_wide = wide(2)
        floor = jnp.stack([floor_l[:, 3 * 32 + h:3 * 32 + h + 1] for h in heads])
        decay = jnp.stack([decay_row[:, 2 * M_HEADS + h:2 * M_HEADS + h + 1] for h in heads])
        q = per_head(q_ref)
        k = per_head(k_ref)
        v = per_head(v_ref)
        w_intra = jnp.exp(jnp.where(causal[None], rowg[:, None, :] - a, -jnp.inf))
        s = bmm(q, k, 2, 2) * scale * w_intra
        s_hi = s.astype(BF16)
        s_lo = (s - s_hi.astype(F32)).astype(BF16)
        sv = bmm(s_hi, jnp.concatenate([v, ones_v], axis=-1), 2, 1)
        cn = cn_ref[...]
        qc = bmm(q, cn.astype(BF16), 2, 1)
        num = sv[..., :dh] + w_inter * qc[..., :dh]
        den = sv[..., dh:] + bmm(s_lo, ones_v, 2, 1) + w_inter * qc[..., dh:]
        hval = num / jnp.maximum(jnp.abs(den), floor)
        wkv =jnp.concatenate([wk_wide * v.astype(F32), wk_wide], axis=-1).astype(BF16)
        cn_ref[...] = decay * cn + scale * bmm(k, wkv, 1, 1)
        mu = bmm(hval.astype(BF16), lane_mean, 2, 1)
        hc = hval - mu
        var = bmm((hc * hc).astype(BF16), lane_mean, 2, 1)
        hn = hc * lax.rsqrt(var + LN_EPS)
        for h in heads:
            hs = slice(h * dh, (h + 1) * dh)
            og = jax.nn.sigmoid(o_ref[rows, hs].astype(F32))
            y_ref[rows, hs] = (og * hn[h] * ng_ref[:, hs]).astype(BF16)
        return carry

    lax.fori_loop(0, ts // L, chunk_body, 0, unroll=4)


_COL_GROUP = 32


def _spread_matrix():
    k = np.arange(LANES)
    grp, slot, head = k // _COL_GROUP, (k % _COL_GROUP) // M_HEADS, k % M_HEADS
    block = ((slot + 1) % 3) * M_HEADS + head
    valid = (grp < 3) & (slot >= 1)
    cols = np.arange(3 * M_HEADS * M_HEAD_DIM) // M_HEAD_DIM
    return jnp.asarray((block[:, None] == cols[None, :]) & valid[:, None], dtype=BF16)


def _mlstm(proj, rowg, cola, colb, norm_g, batch, seq, ts=1024):
    n = proj.shape[0]
    ts = min(ts, seq)
    nb = seq // ts
    blk = lambda col: pl.BlockSpec((ts, D_MLSTM), lambda b, i, col=col: (b * nb + i, col))
    return pl.pallas_call(
        functools.partial(_mlstm_kernel, ts=ts),
        grid=(batch, nb),
        in_specs=[
            blk(3), blk(4), blk(5), blk(6),
            pl.BlockSpec((ts // CHUNK, M_HEADS, CHUNK), lambda b, i: (b * nb + i, 0, 0)),
            pl.BlockSpec((ts, LANES), lambda b, i: (b * nb + i, 0)),
            pl.BlockSpec((ts, LANES), lambda b, i: (b * nb + i, 0)),
            pl.BlockSpec((LANES, 3 * M_HEADS * M_HEAD_DIM), lambda b, i: (0, 0),
                         pipeline_mode=pl.Buffered(1)),
            pl.BlockSpec((1, D_MLSTM), lambda b, i: (0, 0)),
        ],
        out_specs=pl.BlockSpec((ts, D_MLSTM), lambda b, i: (b * nb + i, 0)),
        out_shape=jax.ShapeDtypeStruct((n, D_MLSTM), BF16),
        scratch_shapes=[
            pltpu.VMEM((M_HEADS, M_HEAD_DIM, 2 * M_HEAD_DIM), F32),
            pltpu.VMEM((1, LANES), F32),
        ],
        compiler_params=_cparams(("parallel", "arbitrary")),
        name="mlstm",
    )(proj, proj, proj, proj, rowg, cola, colb, _spread_matrix(), norm_g)


_MIX_CHUNKS = 2


def _mix_kernel(cb_ref, cc_ref, ch_ref, pc_ref, ph_ref, ym_ref, x_ref, g0_ref, b0_ref, cw_ref,
                cbias_ref, wo_ref, g_ref, b_ref, h1_ref, h1b_ref, *, blocks_per_seq):
    tm = cb_ref.shape[0]
    rows_per_chunk = tm // _MIX_CHUNKS
    first = (pl.program_id(0) % blocks_per_seq) == 0
    halo = pc_ref[...].astype(F32) * ph_ref[...].astype(F32)
    hr = halo.shape[0]
    tail = jnp.where(first, 0.0, halo[hr - 2:hr, :])
    row = lax.broadcasted_iota(jnp.int32, (rows_per_chunk, D_CONV), 0)
    for c in range(_MIX_CHUNKS):
        rows = slice(c * rows_per_chunk, (c + 1) * rows_per_chunk)
        z = cc_ref[rows, :].astype(F32) * ch_ref[rows, :].astype(F32)
        z1 = jnp.where(row == 0, tail[1:2, :], pltpu.roll(z, 1, axis=0))
        z2 = pltpu.roll(z, 2, axis=0)
        z2 = jnp.where(row == 0, tail[0:1, :], z2)
        z2 = jnp.where(row == 1, tail[1:2, :], z2)
        conv = cbias_ref[...] + cw_ref[0:1, :] * z2 + cw_ref[1:2, :] * z1 + cw_ref[2:3, :] * z
        y_conv = (cb_ref[rows, :].astype(F32) * conv).astype(BF16)
        mix = (jnp.dot(y_conv, wo_ref[0:D_CONV, :], preferred_element_type=F32)
               + jnp.dot(ym_ref[rows, :], wo_ref[D_CONV:, :], preferred_element_type=F32))
        h = _layernorm(x_ref[rows, :], g0_ref[...], b0_ref[...])
        h1 = _layernorm(ALPHA * h + mix, g_ref[...], b_ref[...])
        h1_ref[rows, :] = h1
        h1b_ref[rows, :] = h1.astype(BF16)
        tail = z[rows_per_chunk - 2:rows_per_chunk, :]


def _mix(proj, ym, x, g0, b0, conv_w, conv_b, w_out, g, b, seq, tm=512):
    n = proj.shape[0]
    tm = min(tm, seq)
    halo = 16
    hb = tm // halo
    cur = lambda col: pl.BlockSpec((tm, D_CONV), lambda i, col=col: (i, col))
    prev = lambda col: pl.BlockSpec((halo, D_CONV),
                                    lambda i, col=col: (jnp.maximum(i * hb - 1, 0), col))
    const = lambda shape: pl.BlockSpec(shape, lambda i: (0, 0))
    return pl.pallas_call(
        functools.partial(_mix_kernel, blocks_per_seq=seq // tm),
        grid=(n // tm,),
        in_specs=[
            cur(0), cur(1), cur(2), prev(1), prev(2),
            pl.BlockSpec((tm, D_MLSTM), lambda i: (i, 0)),
            pl.BlockSpec((tm, D_MODEL), lambda i: (i, 0)),
            const((1, D_MODEL)), const((1, D_MODEL)),
            const((8, D_CONV)), const((1, D_CONV)),
            pl.BlockSpec((D_MODEL, D_MODEL), lambda i: (0, 0), pipeline_mode=pl.Buffered(1)),
            const((1, D_MODEL)), const((1, D_MODEL)),
        ],
        out_specs=[pl.BlockSpec((tm, D_MODEL), lambda i: (i, 0)),
                   pl.BlockSpec((tm, D_MODEL), lambda i: (i, 0))],
        out_shape=[jax.ShapeDtypeStruct((n, D_MODEL), F32),
                   jax.ShapeDtypeStruct((n, D_MODEL), BF16)],
        compiler_params=_cparams(("parallel",)),
        name="mix",
    )(proj, proj, proj, proj, proj, ym, x, g0, b0, conv_w, conv_b, w_out, g, b)


def _qproj_kernel(x_ref, wqt_ref, qt_ref):
    qt_ref[...] = lax.dot_general(wqt_ref[...], x_ref[...], (((1,), (1,)), ((), ())),
                                  preferred_element_type=F32).astype(BF16)


def _qproj(h1b, wqt, tm=1024):
    n = h1b.shape[0]
    tm = min(tm, n)
    dq = wqt.shape[0]
    return pl.pallas_call(
        _qproj_kernel,
        grid=(n // tm,),
        in_specs=[pl.BlockSpec((tm, D_MODEL), lambda i: (i, 0)),
                  pl.BlockSpec((dq, D_MODEL), lambda i: (0, 0), pipeline_mode=pl.Buffered(1))],
        out_specs=pl.BlockSpec((dq, tm), lambda i: (0, i)),
        out_shape=jax.ShapeDtypeStruct((dq, n), BF16),
        compiler_params=_cparams(("parallel",)),
        name="peer_q",
    )(h1b, wqt)


def _oddeven_sort_pairs(n):
    pairs = []
    p = 1
    while p < n:
        k = p
        while k >= 1:
            for j in range(k % p, n - k, 2 * k):
                for i in range(min(k, n - j - k)):
                    if (i + j) // (2 * p) == (i + j + k) // (2 * p):
                        pairs.append((i + j, i + j + k))
            k //= 2
        p *= 2
    return pairs


_SORT16 = _oddeven_sort_pairs(PEER_TOPK)
_BITONIC16 = [(i, i + d) for d in (8, 4, 2, 1) for i in range(PEER_TOPK) if not i & d]


def _compare_exchange(v, ix, pairs):
    for i, j in pairs:
        up = v[j] > v[i]
        v[i], v[j] = jnp.maximum(v[i], v[j]), jnp.minimum(v[i], v[j])
        ix[i], ix[j] = jnp.where(up, ix[j], ix[i]), jnp.where(up, ix[i], ix[j])


def _network_top16(cur_scr, srt_v, srt_i, sv_scr, si_scr, p, lanes):
    K = PEER_TOPK
    for g in range(N_KEYS // K):
        v = [cur_scr[g * K + i, :, lanes] for i in range(K)]
        ix = [jnp.int32(g * K + i) for i in range(K)]
        _compare_exchange(v, ix, _SORT16)
        for i in range(K):
            srt_v[g * K + i] = v[i]
            srt_i[g * K + i] = ix[i]
    dropped = jnp.full(v[0].shape, -jnp.inf, F32)
    span = K
    while span < N_KEYS:
        for a in range(0, N_KEYS, 2 * span):
            b = a + span
            va = [srt_v[a + i] for i in range(K)]
            ia = [srt_i[a + i] for i in range(K)]
            v, ix = [], []
            for i in range(K):
                vb = srt_v[b + K - 1 - i]
                up = vb > va[i]
                v.append(jnp.maximum(va[i], vb))
                ix.append(jnp.where(up, srt_i[b + K - 1 - i], ia[i]))
                dropped = jnp.maximum(dropped, jnp.minimum(va[i], vb))
            _compare_exchange(v, ix, _BITONIC16)
            for i in range(K):
                srt_v[a + i] = v[i]
                srt_i[a + i] = ix[i]
        span *= 2
    distinct = v[K - 1] > dropped
    for i in range(K - 1):
        distinct = distinct & (v[i] > v[i + 1])
    for i in range(K):
        sv_scr[p, i, :, lanes] = v[i]
        si_scr[p, i, :, lanes] = ix[i]
    return jnp.where(distinct, 0, 1)


def _retrieve_kernel(q_ref, kb_ref, eid_ref, gate_ref, cur_scr, sv_scr, si_scr, srt_v, srt_i):
    tb = q_ref.shape[1]
    dh = PEER_HEADS * PEER_HALF

    for p in range(2):
        s = jnp.dot(kb_ref[p], q_ref[p * dh:(p + 1) * dh, :],
                    preferred_element_type=F32)
        cur_scr[...] = s.reshape(N_KEYS, PEER_HEADS, tb)

        def lane_body(lh, tied, p=p):
            lanes = pl.ds(pl.multiple_of(lh * LANES, LANES), LANES)
            return jnp.maximum(tied, _network_top16(cur_scr, srt_v, srt_i, sv_scr, si_scr, p, lanes))

        tied = lax.fori_loop(0, tb // LANES, lane_body, jnp.zeros((PEER_HEADS, LANES), jnp.int32))

        def round_body(k, taken, p=p):
            vals, ids = [], list(range(N_KEYS))
            for n in range(N_KEYS):
                c = jnp.where(taken == n, -jnp.inf, cur_scr[n])
                cur_scr[n] = c
                vals.append(c)
            while len(vals) > 1:
                right = [vals[i + 1] > vals[i] for i in range(0, len(vals), 2)]
                ids = [jnp.where(g, ids[2 * i + 1], ids[2 * i]) for i, g in enumerate(right)]
                vals = [jnp.maximum(vals[2 * i], vals[2 * i + 1]) for i in range(len(right))]
            sv_scr[p, k] = vals[0]
            si_scr[p, k] = ids[0]
            return ids[0]

        @pl.when(jnp.max(tied) > 0)
        def _(round_body=round_body):
            lax.fori_loop(0, PEER_TOPK, round_body, jnp.full((PEER_HEADS, tb), N_KEYS, jnp.int32))

    sv1 = sv_scr[0]
    sv2 = sv_scr[1]
    e1 = si_scr[0] * N_KEYS
    e2 = si_scr[1]
    a_iota = lax.broadcasted_iota(jnp.int32, (PEER_TOPK, PEER_HEADS, tb), 0)
    front = sv1 + sv2[0][None]
    ptr = jnp.zeros_like(a_iota)
    tops, eids = [], []
    for k in range(PEER_TOPK):
        mx = jnp.max(front, axis=0)
        awin = jnp.min(jnp.where(front == mx[None], a_iota, PEER_TOPK), axis=0)
        hit = a_iota == awin[None]
        bwin = jnp.sum(jnp.where(hit, ptr, 0), axis=0)
        eid = (jnp.sum(jnp.where(hit, e1, 0), axis=0)
               + jnp.sum(jnp.where(a_iota == bwin[None], e2, 0), axis=0))
        nxt = jnp.max(jnp.where(a_iota == (bwin + 1)[None], sv2, -jnp.inf), axis=0)
        front = jnp.where(hit, sv1 + nxt[None], front)
        ptr = jnp.where(hit, ptr + 1, ptr)
        tops.append(mx)
        eids.append(eid)
    top_s = jnp.stack(tops)
    e = jnp.exp(top_s - tops[0][None])
    gate = e / jnp.sum(e, axis=0)[None]
    nj = PEER_TOPK * PEER_HEADS
    eid_ref[...] = jnp.stack(eids).reshape(nj, tb).T
    gate_ref[...] = gate.reshape(nj, tb).T


def _retrieve(q, kb, tb=256):
    n = q.shape[1]
    tb = min(tb, n)
    nj = PEER_HEADS * PEER_TOPK
    dq = 2 * PEER_HEADS * PEER_HALF
    return pl.pallas_call(
        _retrieve_kernel,
        grid=(n // tb,),
        in_specs=[
            pl.BlockSpec((dq, tb), lambda i: (0, i)),
            pl.BlockSpec((2, N_KEYS * PEER_HEADS, dq // 2), lambda i: (0, 0, 0),
                         pipeline_mode=pl.Buffered(1)),
        ],
        out_specs=[pl.BlockSpec((tb, nj), lambda i: (i, 0)),
                   pl.BlockSpec((tb, nj), lambda i: (i, 0))],
        out_shape=[jax.ShapeDtypeStruct((n, nj), jnp.int32),
                   jax.ShapeDtypeStruct((n, nj), F32)],
        scratch_shapes=[
            pltpu.VMEM((N_KEYS, PEER_HEADS, tb), F32),
            pltpu.VMEM((2, PEER_TOPK, PEER_HEADS, tb), F32),
            pltpu.VMEM((2, PEER_TOPK, PEER_HEADS, tb), jnp.int32),
            pltpu.VMEM((N_KEYS, PEER_HEADS, LANES), F32),
            pltpu.VMEM((N_KEYS, PEER_HEADS, LANES), jnp.int32),
        ],
        compiler_params=_cparams(("parallel",)),
        name="retrieve",
    )(q, kb)


def _block_diag_keys(keys):
    eye_h = jnp.eye(PEER_HEADS, dtype=keys.dtype)
    kb = jnp.einsum("hpnc,hg->pnhgc", keys, eye_h)
    return kb.reshape(2, N_KEYS * PEER_HEADS, PEER_HEADS * PEER_HALF)


def _half_major_query_weight(wq):
    d = wq.shape[0]
    return wq.reshape(d, PEER_HEADS, 2, PEER_HALF).transpose(2, 1, 3, 0).reshape(-1, d)


_W_GROUP = 16
_W_PITCH = _W_GROUP + 4


def _scatter_w_kernel(eid_ref, gate_ref, w_ref, scr_a, scr_b):
    tb = eid_ref.shape[0]
    nj = eid_ref.shape[1]
    ngrp = tb // _W_GROUP
    sub = lax.broadcasted_iota(jnp.int32, (N_KEYS, nj), 0)

    def build(grp, scr):
        base = pl.multiple_of(grp * _W_GROUP, _W_GROUP)
        for tl in range(_W_GROUP):
            e = eid_ref[pl.ds(base + tl, 1), :]
            g = gate_ref[pl.ds(base + tl, 1), :]
            p = jnp.where(sub == (e >> 7), 1.0, 0.0).astype(BF16)
            q = jnp.where(sub == (e & (N_KEYS - 1)), g, 0.0).astype(BF16)
            wt = lax.dot_general(p, q, (((1,), (1,)), ((), ())), preferred_element_type=F32)
            scr[pl.ds(tl, N_KEYS, stride=_W_PITCH), :] = wt

    def emit(grp, scr):
        base = pl.multiple_of(grp * _W_GROUP, _W_GROUP)
        for r in range(N_KEYS):
            tile = scr[r * _W_PITCH:r * _W_PITCH + _W_GROUP, :]
            w_ref[pl.ds(base, _W_GROUP), r * N_KEYS:(r + 1) * N_KEYS] = tile.astype(BF16)

    build(0, scr_a)

    def pair_body(i, carry):
        build(2 * i + 1, scr_b)
        emit(2 * i, scr_a)
        build(2 * i + 2, scr_a)
        emit(2 * i + 1, scr_b)
        return carry

    lax.fori_loop(0, ngrp // 2 - 1, pair_body, 0)
    build(ngrp - 1, scr_b)
    emit(ngrp - 2, scr_a)
    emit(ngrp - 1, scr_b)


def _scatter_w(eid, gate, tb=128):
    n, nj = eid.shape
    tb = min(tb, n)
    assert tb % (2 * _W_GROUP) == 0
    return pl.pallas_call(
        _scatter_w_kernel,
        grid=(n // tb,),
        in_specs=[pl.BlockSpec((tb, nj), lambda i: (i, 0)),
                  pl.BlockSpec((tb, nj), lambda i: (i, 0))],
        out_specs=pl.BlockSpec((tb, N_EXPERTS), lambda i: (i, 0)),
        out_shape=jax.ShapeDtypeStruct((n, N_EXPERTS), BF16),
        scratch_shapes=[pltpu.VMEM((N_KEYS * _W_PITCH, N_KEYS), F32)] * 2,
        compiler_params=_cparams(("parallel",)),
        name="scatter_w",
    )(eid, gate)


def _experts_kernel(xb_ref, u_ref, v_ref, w_ref, h1_ref, g_ref, b_ref, out_ref):
    k = pl.program_id(1)

    @pl.when(k == 0)
    def _():
        out_ref[...] = ALPHA * h1_ref[...]

    a = lax.dot_general(xb_ref[...], u_ref[...], (((1,), (1,)), ((), ())),
                        preferred_element_type=F32)
    act = 0.5 * a * (1.0 + lax.erf(a * (0.5 ** 0.5)))
    wa = (w_ref[...].astype(F32) * act).astype(BF16)
    out_ref[...] += jnp.dot(wa, v_ref[...], preferred_element_type=F32)

    @pl.when(k == pl.num_programs(1) - 1)
    def _():
        out_ref[...] = _layernorm(out_ref[...], g_ref[...], b_ref[...])


def _experts(h1b, u, v, w, h1, g, b, tm=1024, ec=1024):
    n = h1b.shape[0]
    tm = min(tm, n)
    once = pl.Buffered(1)
    return pl.pallas_call(
        _experts_kernel,
        grid=(n // tm, N_EXPERTS // ec),
        in_specs=[
            pl.BlockSpec((tm, D_MODEL), lambda i, k: (i, 0), pipeline_mode=once),
            pl.BlockSpec((ec, D_MODEL), lambda i, k: (k, 0)),
            pl.BlockSpec((ec, D_MODEL), lambda i, k: (k, 0)),
            pl.BlockSpec((tm, ec), lambda i, k: (i, k)),
            pl.BlockSpec((tm, D_MODEL), lambda i, k: (i, 0), pipeline_mode=once),
            pl.BlockSpec((1, D_MODEL), lambda i, k: (0, 0)),
            pl.BlockSpec((1, D_MODEL), lambda i, k: (0, 0)),
        ],
        out_specs=pl.BlockSpec((tm, D_MODEL), lambda i, k: (i, 0), pipeline_mode=once),
        out_shape=jax.ShapeDtypeStruct((n, D_MODEL), F32),
        compiler_params=_cparams(("parallel", "arbitrary")),
        name="experts",
    )(h1b, u, v, w, h1, g, b)


def kernel(x, ln_in_g, ln_in_b, w_in, b_gate, conv_w, conv_b, mh_norm_g, w_out, ln1_g, ln1_b,
           peer_wq, peer_keys, peer_u, peer_v, ln2_g, ln2_b):
    batch, seq, d = x.shape
    n = batch * seq
    row = lambda t: t.reshape(1, -1).astype(F32)
    l = 0
    w_main = w_in[l, :, :D_MAIN].astype(BF16)
    w_gate = jnp.pad(w_in[l, :, D_MAIN:], ((0, 0), (0, LANES - 2 * M_HEADS))).astype(BF16)

    x2 = x.reshape(n, d)
    proj, gates = _ln_proj(x2, row(ln_in_g), row(ln_in_b), w_main, w_gate, tm=_TILES["proj_tm"])

    r_rows, cola, colb = _gate_scan(gates, b_gate[l].reshape(2 * M_HEADS, 1), tl=_TILES["scan_tl"])
    rowg = r_rows.reshape(M_HEADS, n // CHUNK, CHUNK).transpose(1, 0, 2)
    ym = _mlstm(proj, rowg, cola, colb, row(mh_norm_g[l]), batch, seq, ts=_TILES["mlstm_ts"])

    cw = jnp.pad(conv_w[l], ((0, 8 - conv_w.shape[1]), (0, 0)))
    h1, h1b = _mix(proj, ym, x2, row(ln_in_g), row(ln_in_b), cw, row(conv_b[l]), w_out[l].astype(BF16),
                   row(ln1_g[l]), row(ln1_b[l]), seq, tm=_TILES["mix_tm"])

    kb = _block_diag_keys(peer_keys[l].astype(BF16))
    q = _qproj(h1b, _half_major_query_weight(peer_wq[l].astype(BF16)))
    eid, gate = _retrieve(q, kb, tb=_TILES["retrieve_tb"])
    w = _scatter_w(eid, gate, tb=_TILES["scatter_tb"])
    out = _experts(h1b, peer_u[l].astype(BF16), peer_v[l].astype(BF16), w, h1,
                   row(ln2_g[l]), row(ln2_b[l]), tm=_TILES["experts_tm"], ec=_TILES["experts_ec"])
    return out.reshape(batch, seq, d)
```

```python
import functools

import jax
import jax.numpy as jnp
import numpy as np
from jax import lax
from jax.experimental import pallas as pl
from jax.experimental.pallas import tpu as pltpu

F32 = jnp.float32
BF16 = jnp.bfloat16

D_MODEL = 2048
CHUNK = 64
D_CONV = 1024
D_MLSTM = 1024
M_HEADS = 8
M_HEAD_DIM = 128
D_MAIN = 3 * D_CONV + 4 * D_MLSTM
PEER_HEADS = 8
N_KEYS = 128
N_EXPERTS = N_KEYS * N_KEYS
PEER_TOPK = 16
PEER_HALF = 128
DEPTH = 1
ALPHA = (2 * DEPTH) ** 0.25
LN_EPS = 1e-5
LANES = 128
VMEM_LIMIT = 56 * 1024 * 1024

_TILES = dict(proj_tm=1024, scan_tl=2048, mlstm_ts=1024, mix_tm=512, retrieve_tb=512,
              scatter_tb=512, experts_tm=1024, experts_ec=1024)


def _cparams(sem):
    return pltpu.CompilerParams(dimension_semantics=sem, vmem_limit_bytes=VMEM_LIMIT)


def _layernorm(x, g, b):
    mu = jnp.mean(x, axis=-1, keepdims=True)
    xc = x - mu
    var = jnp.mean(xc * xc, axis=-1, keepdims=True)
    return xc * lax.rsqrt(var + LN_EPS) * g + b


_LN_ROWS = 256


def _ln_proj_kernel(x_ref, g_ref, b_ref, w_ref, wg_ref, proj_ref, gate_ref, hb_ref):
    @pl.when(pl.program_id(1) == 0)
    def _():
        def rows_body(r, carry):
            rows = pl.ds(pl.multiple_of(r * _LN_ROWS, _LN_ROWS), _LN_ROWS)
            hb = _layernorm(x_ref[rows, :], g_ref[...], b_ref[...]).astype(BF16)
            hb_ref[rows, :] = hb
            gate_ref[rows, :] = jnp.dot(hb, wg_ref[...], preferred_element_type=F32)
            return carry

        lax.fori_loop(0, x_ref.shape[0] // _LN_ROWS, rows_body, 0)

    proj_ref[...] = jnp.dot(hb_ref[...], w_ref[...], preferred_element_type=F32).astype(BF16)


def _ln_proj(x, g, b, w_main, w_gate, tm=1024, tn=1792):
    n = x.shape[0]
    tm = min(tm, n)
    return pl.pallas_call(
        _ln_proj_kernel,
        grid=(n // tm, D_MAIN // tn),
        in_specs=[
            pl.BlockSpec((tm, D_MODEL), lambda i, j: (i, 0)),
            pl.BlockSpec((1, D_MODEL), lambda i, j: (0, 0)),
            pl.BlockSpec((1, D_MODEL), lambda i, j: (0, 0)),
            pl.BlockSpec((D_MODEL, tn), lambda i, j: (0, j)),
            pl.BlockSpec((D_MODEL, LANES), lambda i, j: (0, 0)),
        ],
        out_specs=[
            pl.BlockSpec((tm, tn), lambda i, j: (i, j)),
            pl.BlockSpec((tm, LANES), lambda i, j: (i, 0)),
        ],
        out_shape=[
            jax.ShapeDtypeStruct((n, D_MAIN), BF16),
            jax.ShapeDtypeStruct((n, LANES), F32),
        ],
        scratch_shapes=[pltpu.VMEM((tm, D_MODEL), BF16)],
        compiler_params=_cparams(("parallel", "arbitrary")),
        name="ln_proj",
    )(x, g, b, w_main, w_gate)


def _gate_scan_kernel(gt_ref, b_ref, r_ref, cola_ref, colb_ref):
    z = gt_ref[...].T[:2 * M_HEADS] + b_ref[...]
    li = z[:M_HEADS]
    lf = jax.nn.log_sigmoid(z[M_HEADS:])
    pos = lax.broadcasted_iota(jnp.int32, lf.shape, 1) % CHUNK
    bcum = lf
    shift = 1
    while shift < CHUNK:
        bcum = bcum + jnp.where(pos >= shift, pltpu.roll(bcum, shift, axis=1), 0.0)
        shift *= 2
    r = li - bcum
    cm = r
    shift = 1
    while shift < CHUNK:
        cm = jnp.maximum(cm, jnp.where(pos >= shift, pltpu.roll(cm, shift, axis=1), -jnp.inf))
        shift *= 2
    r_ref[...] = r
    zero = jnp.zeros_like(r)
    cola_ref[...] = jnp.concatenate([zero, r, cm, cm] * 3 + [cm, zero, zero, zero], axis=0).T
    colb_ref[...] = jnp.concatenate([zero, zero, bcum, bcum] * 3 + [bcum, zero, zero, zero], axis=0).T


def _gate_scan(gt, bias, tl=2048):
    n = gt.shape[0]
    tl = min(tl, n)
    return pl.pallas_call(
        _gate_scan_kernel,
        grid=(n // tl,),
        in_specs=[pl.BlockSpec((tl, LANES), lambda i: (i, 0)),
                  pl.BlockSpec((2 * M_HEADS, 1), lambda i: (0, 0))],
        out_specs=[pl.BlockSpec((M_HEADS, tl), lambda i: (0, i)),
                   pl.BlockSpec((tl, LANES), lambda i: (i, 0)),
                   pl.BlockSpec((tl, LANES), lambda i: (i, 0))],
        out_shape=[jax.ShapeDtypeStruct((M_HEADS, n), F32),
                   jax.ShapeDtypeStruct((n, LANES), F32),
                   jax.ShapeDtypeStruct((n, LANES), F32)],
        compiler_params=_cparams(("parallel",)),
        name="gate_scan",
    )(gt, bias)


def _mlstm_kernel(q_ref, k_ref, v_ref, o_ref, row_ref, cola_ref, colb_ref, spread_ref, ng_ref,
                  y_ref, cn_ref, m_ref, *, ts):
    @pl.when(pl.program_id(1) == 0)
    def _():
        cn_ref[...] = jnp.zeros_like(cn_ref)
        m_ref[...] = jnp.zeros_like(m_ref)

    L = CHUNK
    dh = M_HEAD_DIM
    scale = dh ** -0.5
    ones_v = jnp.ones((M_HEADS, L, dh), BF16)
    lane_mean = jnp.full((M_HEADS, dh, dh), 1.0 / dh, BF16)
    li = lax.broadcasted_iota(jnp.int32, (L, L), 0)
    lj = lax.broadcasted_iota(jnp.int32, (L, L), 1)
    causal = lj <= li
    lane = lax.broadcasted_iota(jnp.int32, (L, LANES), 1)
    lane_grp = lane // _COL_GROUP
    lane_in = lane % _COL_GROUP
    is_wk = (lane_grp < 3) & (lane_in >= M_HEADS) & (lane_in < 2 * M_HEADS)
    is_a = (lane_grp < 3) & (lane_in >= 2 * M_HEADS) & (lane_in < 3 * M_HEADS)
    is_wi = (lane_grp < 3) & (lane_in >= 3 * M_HEADS)

    def chunk_body(c, carry):
        t0 = pl.multiple_of(c * L, L)
        rows = pl.ds(t0, L)
        rowg = row_ref[c]
        ca = cola_ref[rows, :]
        cb = colb_ref[rows, :]
        m_row = m_ref[...]
        a_l = jnp.maximum(m_row, ca)
        w_inter_l = jnp.exp(m_row - a_l)
        floor_l = jnp.exp(-(cb + a_l))
        a_last = a_l[L - 1:L, :]
        wk_l = jnp.exp(ca - pltpu.roll(a_last, LANES - M_HEADS, axis=1))
        decay_row = jnp.exp(m_row - a_last)
        m_ref[...] = cb[L - 1:L, :] + a_last
        heads = range(M_HEADS)
        per_head = lambda ref: jnp.stack([ref[rows, h * dh:(h + 1) * dh] for h in heads])
        bmm = lambda x, y, cx, cy: lax.dot_general(
            x, y, (((cx,), (cy,)), ((0,), (0,))), preferred_element_type=F32)
        x = jnp.where(is_wk, wk_l, jnp.where(is_a, a_l, jnp.where(is_wi, w_inter_l, 0.0)))
        x_hi = x.astype(BF16)
        x_r1 = x - x_hi.astype(F32)
        x_mid = x_r1.astype(BF16)
        x_lo = (x_r1 - x_mid.astype(F32)).astype(BF16)
        parts = jnp.where(lane_grp == 0, x_hi, jnp.where(lane_grp == 1, x_mid, x_lo))
        rep = jnp.dot(parts, spread_ref[...], preferred_element_type=F32)
        wide = lambda qi: jnp.stack([rep[:, (qi * M_HEADS + h) * dh:(qi * M_HEADS + h + 1) * dh]
                                     for h in heads])
        a = wide(0)[..., :L]
        w_inter = wide(1)
        wk_wide = wide(2)
        floor = jnp.stack([floor_l[:, 3 * 32 + h:3 * 32 + h + 1] for h in heads])
        decay = jnp.stack([decay_row[:, 2 * M_HEADS + h:2 * M_HEADS + h + 1] for h in heads])
        q = per_head(q_ref)
        k = per_head(k_ref)
        v = per_head(v_ref)
        w_intra = jnp.exp(jnp.where(causal[None], rowg[:, None, :] - a, -jnp.inf))
        s = bmm(q, k, 2, 2) * scale * w_intra
        s_hi = s.astype(BF16)
        s_lo = (s - s_hi.astype(F32)).astype(BF16)
        sv = bmm(s_hi, jnp.concatenate([v, ones_v], axis=-1), 2, 1)
        cn = cn_ref[...]
        qc = bmm(q, cn.astype(BF16), 2, 1)
        num = sv[..., :dh] + w_inter * qc[..., :dh]
        den = sv[..., dh:] + bmm(s_lo, ones_v, 2, 1) + w_inter * qc[..., dh:]
        hval = num / jnp.maximum(jnp.abs(den), floor)
        wkv =jnp.concatenate([wk_wide * v.astype(F32), wk_wide], axis=-1).astype(BF16)
        cn_ref[...] = decay * cn + scale * bmm(k, wkv, 1, 1)
        mu = bmm(hval.astype(BF16), lane_mean, 2, 1)
        hc = hval - mu
        var = bmm((hc * hc).astype(BF16), lane_mean, 2, 1)
        hn = hc * lax.rsqrt(var + LN_EPS)
        for h in heads:
            hs = slice(h * dh, (h + 1) * dh)
            og = jax.nn.sigmoid(o_ref[rows, hs].astype(F32))
            y_ref[rows, hs] = (og * hn[h] * ng_ref[:, hs]).astype(BF16)
        return carry

    lax.fori_loop(0, ts // L, chunk_body, 0, unroll=8)


_COL_GROUP = 32


def _spread_matrix():
    k = np.arange(LANES)
    grp, slot, head = k // _COL_GROUP, (k % _COL_GROUP) // M_HEADS, k % M_HEADS
    block = ((slot + 1) % 3) * M_HEADS + head
    valid = (grp < 3) & (slot >= 1)
    cols = np.arange(3 * M_HEADS * M_HEAD_DIM) // M_HEAD_DIM
    return jnp.asarray((block[:, None] == cols[None, :]) & valid[:, None], dtype=BF16)


def _mlstm(proj, rowg, cola, colb, norm_g, batch, seq, ts=1024):
    n = proj.shape[0]
    ts = min(ts, seq)
    nb = seq // ts
    blk = lambda col: pl.BlockSpec((ts, D_MLSTM), lambda b, i, col=col: (b * nb + i, col))
    return pl.pallas_call(
        functools.partial(_mlstm_kernel, ts=ts),
        grid=(batch, nb),
        in_specs=[
            blk(3), blk(4), blk(5), blk(6),
            pl.BlockSpec((ts // CHUNK, M_HEADS, CHUNK), lambda b, i: (b * nb + i, 0, 0)),
            pl.BlockSpec((ts, LANES), lambda b, i: (b * nb + i, 0)),
            pl.BlockSpec((ts, LANES), lambda b, i: (b * nb + i, 0)),
            pl.BlockSpec((LANES, 3 * M_HEADS * M_HEAD_DIM), lambda b, i: (0, 0),
                         pipeline_mode=pl.Buffered(1)),
            pl.BlockSpec((1, D_MLSTM), lambda b, i: (0, 0)),
        ],
        out_specs=pl.BlockSpec((ts, D_MLSTM), lambda b, i: (b * nb + i, 0)),
        out_shape=jax.ShapeDtypeStruct((n, D_MLSTM), BF16),
        scratch_shapes=[
            pltpu.VMEM((M_HEADS, M_HEAD_DIM, 2 * M_HEAD_DIM), F32),
            pltpu.VMEM((1, LANES), F32),
        ],
        compiler_params=_cparams(("parallel", "arbitrary")),
        name="mlstm",
    )(proj, proj, proj, proj, rowg, cola, colb, _spread_matrix(), norm_g)


_MIX_CHUNKS = 2


def _mix_kernel(cb_ref, cc_ref, ch_ref, pc_ref, ph_ref, ym_ref, x_ref, g0_ref, b0_ref, cw_ref,
                cbias_ref, wo_ref, g_ref, b_ref, h1_ref, h1b_ref, *, blocks_per_seq):
    tm = cb_ref.shape[0]
    rows_per_chunk = tm // _MIX_CHUNKS
    first = (pl.program_id(0) % blocks_per_seq) == 0
    halo = pc_ref[...].astype(F32) * ph_ref[...].astype(F32)
    hr = halo.shape[0]
    tail = jnp.where(first, 0.0, halo[hr - 2:hr, :])
    row = lax.broadcasted_iota(jnp.int32, (rows_per_chunk, D_CONV), 0)
    for c in range(_MIX_CHUNKS):
        rows = slice(c * rows_per_chunk, (c + 1) * rows_per_chunk)
        z = cc_ref[rows, :].astype(F32) * ch_ref[rows, :].astype(F32)
        z1 = jnp.where(row == 0, tail[1:2, :], pltpu.roll(z, 1, axis=0))
        z2 = pltpu.roll(z, 2, axis=0)
        z2 = jnp.where(row == 0, tail[0:1, :], z2)
        z2 = jnp.where(row == 1, tail[1:2, :], z2)
        conv = cbias_ref[...] + cw_ref[0:1, :] * z2 + cw_ref[1:2, :] * z1 + cw_ref[2:3, :] * z
        y_conv = (cb_ref[rows, :].astype(F32) * conv).astype(BF16)
        mix = (jnp.dot(y_conv, wo_ref[0:D_CONV, :], preferred_element_type=F32)
               + jnp.dot(ym_ref[rows, :], wo_ref[D_CONV:, :], preferred_element_type=F32))
        h = _layernorm(x_ref[rows, :], g0_ref[...], b0_ref[...])
        h1 = _layernorm(ALPHA * h + mix, g_ref[...], b_ref[...])
        h1_ref[rows, :] = h1
        h1b_ref[rows, :] = h1.astype(BF16)
        tail = z[rows_per_chunk - 2:rows_per_chunk, :]


def _mix(proj, ym, x, g0, b0, conv_w, conv_b, w_out, g, b, seq, tm=512):
    n = proj.shape[0]
    tm = min(tm, seq)
    halo = 16
    hb = tm // halo
    cur = lambda col: pl.BlockSpec((tm, D_CONV), lambda i, col=col: (i, col))
    prev = lambda col: pl.BlockSpec((halo, D_CONV),
                                    lambda i, col=col: (jnp.maximum(i * hb - 1, 0), col))
    const = lambda shape: pl.BlockSpec(shape, lambda i: (0, 0))
    return pl.pallas_call(
        functools.partial(_mix_kernel, blocks_per_seq=seq // tm),
        grid=(n // tm,),
        in_specs=[
            cur(0), cur(1), cur(2), prev(1), prev(2),
            pl.BlockSpec((tm, D_MLSTM), lambda i: (i, 0)),
            pl.BlockSpec((tm, D_MODEL), lambda i: (i, 0)),
            const((1, D_MODEL)), const((1, D_MODEL)),
            const((8, D_CONV)), const((1, D_CONV)),
            pl.BlockSpec((D_MODEL, D_MODEL), lambda i: (0, 0), pipeline_mode=pl.Buffered(1)),
            const((1, D_MODEL)), const((1, D_MODEL)),
        ],
        out_specs=[pl.BlockSpec((tm, D_MODEL), lambda i: (i, 0)),
                   pl.BlockSpec((tm, D_MODEL), lambda i: (i, 0))],
        out_shape=[jax.ShapeDtypeStruct((n, D_MODEL), F32),
                   jax.ShapeDtypeStruct((n, D_MODEL), BF16)],
        compiler_params=_cparams(("parallel",)),
        name="mix",
    )(proj, proj, proj, proj, proj, ym, x, g0, b0, conv_w, conv_b, w_out, g, b)


def _qproj_kernel(x_ref, wqt_ref, qt_ref):
    qt_ref[...] = lax.dot_general(wqt_ref[...], x_ref[...], (((1,), (1,)), ((), ())),
                                  preferred_element_type=F32).astype(BF16)


def _qproj(h1b, wqt, tm=1024):
    n = h1b.shape[0]
    tm = min(tm, n)
    dq = wqt.shape[0]
    return pl.pallas_call(
        _qproj_kernel,
        grid=(n // tm,),
        in_specs=[pl.BlockSpec((tm, D_MODEL), lambda i: (i, 0)),
                  pl.BlockSpec((dq, D_MODEL), lambda i: (0, 0), pipeline_mode=pl.Buffered(1))],
        out_specs=pl.BlockSpec((dq, tm), lambda i: (0, i)),
        out_shape=jax.ShapeDtypeStruct((dq, n), BF16),
        compiler_params=_cparams(("parallel",)),
        name="peer_q",
    )(h1b, wqt)


def _oddeven_sort_pairs(n):
    pairs = []
    p = 1
    while p < n:
        k = p
        while k >= 1:
            for j in range(k % p, n - k, 2 * k):
                for i in range(min(k, n - j - k)):
                    if (i + j) // (2 * p) == (i + j + k) // (2 * p):
                        pairs.append((i + j, i + j + k))
            k //= 2
        p *= 2
    return pairs


_SORT16 = _oddeven_sort_pairs(PEER_TOPK)
_BITONIC16 = [(i, i + d) for d in (8, 4, 2, 1) for i in range(PEER_TOPK) if not i & d]


def _compare_exchange(v, ix, pairs):
    for i, j in pairs:
        up = v[j] > v[i]
        v[i], v[j] = jnp.maximum(v[i], v[j]), jnp.minimum(v[i], v[j])
        ix[i], ix[j] = jnp.where(up, ix[j], ix[i]), jnp.where(up, ix[i], ix[j])


def _network_top16(cur_scr, srt_v, srt_i, sv_scr, si_scr, p, lanes):
    K = PEER_TOPK
    for g in range(N_KEYS // K):
        v = [cur_scr[g * K + i, :, lanes] for i in range(K)]
        ix = [jnp.int32(g * K + i) for i in range(K)]
        _compare_exchange(v, ix, _SORT16)
        for i in range(K):
            srt_v[g * K + i] = v[i]
            srt_i[g * K + i] = ix[i]
    dropped = jnp.full(v[0].shape, -jnp.inf, F32)
    span = K
    while span < N_KEYS:
        for a in range(0, N_KEYS, 2 * span):
            b = a + span
            va = [srt_v[a + i] for i in range(K)]
            ia = [srt_i[a + i] for i in range(K)]
            v, ix = [], []
            for i in range(K):
                vb = srt_v[b + K - 1 - i]
                up = vb > va[i]
                v.append(jnp.maximum(va[i], vb))
                ix.append(jnp.where(up, srt_i[b + K - 1 - i], ia[i]))
                dropped = jnp.maximum(dropped, jnp.minimum(va[i], vb))
            _compare_exchange(v, ix, _BITONIC16)
            for i in range(K):
                srt_v[a + i] = v[i]
                srt_i[a + i] = ix[i]
        span *= 2
    distinct = v[K - 1] > dropped
    for i in range(K - 1):
        distinct = distinct & (v[i] > v[i + 1])
    for i in range(K):
        sv_scr[p, i, :, lanes] = v[i]
        si_scr[p, i, :, lanes] = ix[i]
    return jnp.where(distinct, 0, 1)


def _retrieve_kernel(q_ref, kb_ref, eid_ref, gate_ref, cur_scr, sv_scr, si_scr, srt_v, srt_i):
    tb = q_ref.shape[1]
    dh = PEER_HEADS * PEER_HALF

    for p in range(2):
        s = jnp.dot(kb_ref[p], q_ref[p * dh:(p + 1) * dh, :],
                    preferred_element_type=F32)
        cur_scr[...] = s.reshape(N_KEYS, PEER_HEADS, tb)

        def lane_body(lh, tied, p=p):
            lanes = pl.ds(pl.multiple_of(lh * LANES, LANES), LANES)
            return jnp.maximum(tied, _network_top16(cur_scr, srt_v, srt_i, sv_scr, si_scr, p, lanes))

        tied = lax.fori_loop(0, tb // LANES, lane_body, jnp.zeros((PEER_HEADS, LANES), jnp.int32))

        def round_body(k, taken, p=p):
            vals, ids = [], list(range(N_KEYS))
            for n in range(N_KEYS):
                c = jnp.where(taken == n, -jnp.inf, cur_scr[n])
                cur_scr[n] = c
                vals.append(c)
            while len(vals) > 1:
                right = [vals[i + 1] > vals[i] for i in range(0, len(vals), 2)]
                ids = [jnp.where(g, ids[2 * i + 1], ids[2 * i]) for i, g in enumerate(right)]
                vals = [jnp.maximum(vals[2 * i], vals[2 * i + 1]) for i in range(len(right))]
            sv_scr[p, k] = vals[0]
            si_scr[p, k] = ids[0]
            return ids[0]

        @pl.when(jnp.max(tied) > 0)
        def _(round_body=round_body):
            lax.fori_loop(0, PEER_TOPK, round_body, jnp.full((PEER_HEADS, tb), N_KEYS, jnp.int32))

    sv1 = sv_scr[0]
    sv2 = sv_scr[1]
    e1 = si_scr[0] * N_KEYS
    e2 = si_scr[1]
    a_iota = lax.broadcasted_iota(jnp.int32, (PEER_TOPK, PEER_HEADS, tb), 0)
    front = sv1 + sv2[0][None]
    ptr = jnp.zeros_like(a_iota)
    tops, eids = [], []
    for k in range(PEER_TOPK):
        mx = jnp.max(front, axis=0)
        awin = jnp.min(jnp.where(front == mx[None], a_iota, PEER_TOPK), axis=0)
        hit = a_iota == awin[None]
        bwin = jnp.sum(jnp.where(hit, ptr, 0), axis=0)
        eid = (jnp.sum(jnp.where(hit, e1, 0), axis=0)
               + jnp.sum(jnp.where(a_iota == bwin[None], e2, 0), axis=0))
        nxt = jnp.max(jnp.where(a_iota == (bwin + 1)[None], sv2, -jnp.inf), axis=0)
        front = jnp.where(hit, sv1 + nxt[None], front)
        ptr = jnp.where(hit, ptr + 1, ptr)
        tops.append(mx)
        eids.append(eid)
    top_s = jnp.stack(tops)
    e = jnp.exp(top_s - tops[0][None])
    gate = e / jnp.sum(e, axis=0)[None]
    nj = PEER_TOPK * PEER_HEADS
    eid_ref[...] = jnp.stack(eids).reshape(nj, tb).T
    gate_ref[...] = gate.reshape(nj, tb).T


def _retrieve(q, kb, tb=256):
    n = q.shape[1]
    tb = min(tb, n)
    nj = PEER_HEADS * PEER_TOPK
    dq = 2 * PEER_HEADS * PEER_HALF
    return pl.pallas_call(
        _retrieve_kernel,
        grid=(n // tb,),
        in_specs=[
            pl.BlockSpec((dq, tb), lambda i: (0, i)),
            pl.BlockSpec((2, N_KEYS * PEER_HEADS, dq // 2), lambda i: (0, 0, 0),
                         pipeline_mode=pl.Buffered(1)),
        ],
        out_specs=[pl.BlockSpec((tb, nj), lambda i: (i, 0)),
                   pl.BlockSpec((tb, nj), lambda i: (i, 0))],
        out_shape=[jax.ShapeDtypeStruct((n, nj), jnp.int32),
                   jax.ShapeDtypeStruct((n, nj), F32)],
        scratch_shapes=[
            pltpu.VMEM((N_KEYS, PEER_HEADS, tb), F32),
            pltpu.VMEM((2, PEER_TOPK, PEER_HEADS, tb), F32),
            pltpu.VMEM((2, PEER_TOPK, PEER_HEADS, tb), jnp.int32),
            pltpu.VMEM((N_KEYS, PEER_HEADS, LANES), F32),
            pltpu.VMEM((N_KEYS, PEER_HEADS, LANES), jnp.int32),
        ],
        compiler_params=_cparams(("parallel",)),
        name="retrieve",
    )(q, kb)


def _block_diag_keys(keys):
    eye_h = jnp.eye(PEER_HEADS, dtype=keys.dtype)
    kb = jnp.einsum("hpnc,hg->pnhgc", keys, eye_h)
    return kb.reshape(2, N_KEYS * PEER_HEADS, PEER_HEADS * PEER_HALF)


def _half_major_query_weight(wq):
    d = wq.shape[0]
    return wq.reshape(d, PEER_HEADS, 2, PEER_HALF).transpose(2, 1, 3, 0).reshape(-1, d)


_W_GROUP = 16
_W_PITCH = _W_GROUP + 4


def _scatter_w_kernel(eid_ref, gate_ref, w_ref, scr_a, scr_b):
    tb = eid_ref.shape[0]
    nj = eid_ref.shape[1]
    ngrp = tb // _W_GROUP
    sub = lax.broadcasted_iota(jnp.int32, (N_KEYS, nj), 0)

    def build(grp, scr):
        base = pl.multiple_of(grp * _W_GROUP, _W_GROUP)
        for tl in range(_W_GROUP):
            e = eid_ref[pl.ds(base + tl, 1), :]
            g = gate_ref[pl.ds(base + tl, 1), :]
            p = jnp.where(sub == (e >> 7), 1.0, 0.0).astype(BF16)
            q = jnp.where(sub == (e & (N_KEYS - 1)), g, 0.0).astype(BF16)
            wt = lax.dot_general(p, q, (((1,), (1,)), ((), ())), preferred_element_type=F32)
            scr[pl.ds(tl, N_KEYS, stride=_W_PITCH), :] = wt

    def emit(grp, scr):
        base = pl.multiple_of(grp * _W_GROUP, _W_GROUP)
        for r in range(N_KEYS):
            tile = scr[r * _W_PITCH:r * _W_PITCH + _W_GROUP, :]
            w_ref[pl.ds(base, _W_GROUP), r * N_KEYS:(r + 1) * N_KEYS] = tile.astype(BF16)

    build(0, scr_a)

    def pair_body(i, carry):
        build(2 * i + 1, scr_b)
        emit(2 * i, scr_a)
        build(2 * i + 2, scr_a)
        emit(2 * i + 1, scr_b)
        return carry

    lax.fori_loop(0, ngrp // 2 - 1, pair_body, 0)
    build(ngrp - 1, scr_b)
    emit(ngrp - 2, scr_a)
    emit(ngrp - 1, scr_b)


def _scatter_w(eid, gate, tb=128):
    n, nj = eid.shape
    tb = min(tb, n)
    assert tb % (2 * _W_GROUP) == 0
    return pl.pallas_call(
        _scatter_w_kernel,
        grid=(n // tb,),
        in_specs=[pl.BlockSpec((tb, nj), lambda i: (i, 0)),
                  pl.BlockSpec((tb, nj), lambda i: (i, 0))],
        out_specs=pl.BlockSpec((tb, N_EXPERTS), lambda i: (i, 0)),
        out_shape=jax.ShapeDtypeStruct((n, N_EXPERTS), BF16),
        scratch_shapes=[pltpu.VMEM((N_KEYS * _W_PITCH, N_KEYS), F32)] * 2,
        compiler_params=_cparams(("parallel",)),
        name="scatter_w",
    )(eid, gate)


def _experts_kernel(xb_ref, u_ref, v_ref, w_ref, h1_ref, g_ref, b_ref, out_ref):
    k = pl.program_id(1)

    @pl.when(k == 0)
    def _():
        out_ref[...] = ALPHA * h1_ref[...]

    a = lax.dot_general(xb_ref[...], u_ref[...], (((1,), (1,)), ((), ())),
                        preferred_element_type=F32)
    act = 0.5 * a * (1.0 + lax.erf(a * (0.5 ** 0.5)))
    wa = (w_ref[...].astype(F32) * act).astype(BF16)
    out_ref[...] += jnp.dot(wa, v_ref[...], preferred_element_type=F32)

    @pl.when(k == pl.num_programs(1) - 1)
    def _():
        out_ref[...] = _layernorm(out_ref[...], g_ref[...], b_ref[...])


def _experts(h1b, u, v, w, h1, g, b, tm=1024, ec=1024):
    n = h1b.shape[0]
    tm = min(tm, n)
    once = pl.Buffered(1)
    return pl.pallas_call(
        _experts_kernel,
        grid=(n // tm, N_EXPERTS // ec),
        in_specs=[
            pl.BlockSpec((tm, D_MODEL), lambda i, k: (i, 0), pipeline_mode=once),
            pl.BlockSpec((ec, D_MODEL), lambda i, k: (k, 0)),
            pl.BlockSpec((ec, D_MODEL), lambda i, k: (k, 0)),
            pl.BlockSpec((tm, ec), lambda i, k: (i, k)),
            pl.BlockSpec((tm, D_MODEL), lambda i, k: (i, 0), pipeline_mode=once),
            pl.BlockSpec((1, D_MODEL), lambda i, k: (0, 0)),
            pl.BlockSpec((1, D_MODEL), lambda i, k: (0, 0)),
        ],
        out_specs=pl.BlockSpec((tm, D_MODEL), lambda i, k: (i, 0), pipeline_mode=once),
        out_shape=jax.ShapeDtypeStruct((n, D_MODEL), F32),
        compiler_params=_cparams(("parallel", "arbitrary")),
        name="experts",
    )(h1b, u, v, w, h1, g, b)


def kernel(x, ln_in_g, ln_in_b, w_in, b_gate, conv_w, conv_b, mh_norm_g, w_out, ln1_g, ln1_b,
           peer_wq, peer_keys, peer_u, peer_v, ln2_g, ln2_b):
    batch, seq, d = x.shape
    n = batch * seq
    row = lambda t: t.reshape(1, -1).astype(F32)
    l = 0
    w_main = w_in[l, :, :D_MAIN].astype(BF16)
    w_gate = jnp.pad(w_in[l, :, D_MAIN:], ((0, 0), (0, LANES - 2 * M_HEADS))).astype(BF16)

    x2 = x.reshape(n, d)
    proj, gates = _ln_proj(x2, row(ln_in_g), row(ln_in_b), w_main, w_gate, tm=_TILES["proj_tm"])

    r_rows, cola, colb = _gate_scan(gates, b_gate[l].reshape(2 * M_HEADS, 1), tl=_TILES["scan_tl"])
    rowg = r_rows.reshape(M_HEADS, n // CHUNK, CHUNK).transpose(1, 0, 2)
    ym = _mlstm(proj, rowg, cola, colb, row(mh_norm_g[l]), batch, seq, ts=_TILES["mlstm_ts"])

    cw = jnp.pad(conv_w[l], ((0, 8 - conv_w.shape[1]), (0, 0)))
    h1, h1b = _mix(proj, ym, x2, row(ln_in_g), row(ln_in_b), cw, row(conv_b[l]), w_out[l].astype(BF16),
                   row(ln1_g[l]), row(ln1_b[l]), seq, tm=_TILES["mix_tm"])

    kb = _block_diag_keys(peer_keys[l].astype(BF16))
    q = _qproj(h1b, _half_major_query_weight(peer_wq[l].astype(BF16)))
    eid, gate = _retrieve(q, kb, tb=_TILES["retrieve_tb"])
    w = _scatter_w(eid, gate, tb=_TILES["scatter_tb"])
    out = _experts(h1b, peer_u[l].astype(BF16), peer_v[l].astype(BF16), w, h1,
                   row(ln2_g[l]), row(ln2_b[l]), tm=_TILES["experts_tm"], ec=_TILES["experts_ec"])
    return out.reshape(batch, seq, d)
```

```python
import functools

import jax
import jax.numpy as jnp
import numpy as np
from jax import lax
from jax.experimental import pallas as pl
from jax.experimental.pallas import tpu as pltpu

F32 = jnp.float32
BF16 = jnp.bfloat16

D_MODEL = 2048
CHUNK = 64
D_CONV = 1024
D_MLSTM = 1024
M_HEADS = 8
M_HEAD_DIM = 128
D_MAIN = 3 * D_CONV + 4 * D_MLSTM
PEER_HEADS = 8
N_KEYS = 128
N_EXPERTS = N_KEYS * N_KEYS
PEER_TOPK = 16
PEER_HALF = 128
DEPTH = 1
ALPHA = (2 * DEPTH) ** 0.25
LN_EPS = 1e-5
LANES = 128
VMEM_LIMIT = 56 * 1024 * 1024

_TILES = dict(proj_tm=1024, scan_tl=2048, mlstm_ts=1024, mix_tm=512, retrieve_tb=512,
              scatter_tb=512, experts_tm=1024, experts_ec=1024)


def _cparams(sem):
    return pltpu.CompilerParams(dimension_semantics=sem, vmem_limit_bytes=VMEM_LIMIT)


def _layernorm(x, g, b):
    mu = jnp.mean(x, axis=-1, keepdims=True)
    xc = x - mu
    var = jnp.mean(xc * xc, axis=-1, keepdims=True)
    return xc * lax.rsqrt(var + LN_EPS) * g + b


_LN_ROWS = 256


def _ln_proj_kernel(x_ref, g_ref, b_ref, w_ref, wg_ref, proj_ref, gate_ref, hb_ref):
    @pl.when(pl.program_id(1) == 0)
    def _():
        def rows_body(r, carry):
            rows = pl.ds(pl.multiple_of(r * _LN_ROWS, _LN_ROWS), _LN_ROWS)
            hb = _layernorm(x_ref[rows, :], g_ref[...], b_ref[...]).astype(BF16)
            hb_ref[rows, :] = hb
            gate_ref[rows, :] = jnp.dot(hb, wg_ref[...], preferred_element_type=F32)
            return carry

        lax.fori_loop(0, x_ref.shape[0] // _LN_ROWS, rows_body, 0)

    proj_ref[...] = jnp.dot(hb_ref[...], w_ref[...], preferred_element_type=F32).astype(BF16)


def _ln_proj(x, g, b, w_main, w_gate, tm=1024, tn=1792):
    n = x.shape[0]
    tm = min(tm, n)
    return pl.pallas_call(
        _ln_proj_kernel,
        grid=(n // tm, D_MAIN // tn),
        in_specs=[
            pl.BlockSpec((tm, D_MODEL), lambda i, j: (i, 0)),
            pl.BlockSpec((1, D_MODEL), lambda i, j: (0, 0)),
            pl.BlockSpec((1, D_MODEL), lambda i, j: (0, 0)),
            pl.BlockSpec((D_MODEL, tn), lambda i, j: (0, j)),
            pl.BlockSpec((D_MODEL, LANES), lambda i, j: (0, 0)),
        ],
        out_specs=[
            pl.BlockSpec((tm, tn), lambda i, j: (i, j)),
            pl.BlockSpec((tm, LANES), lambda i, j: (i, 0)),
        ],
        out_shape=[
            jax.ShapeDtypeStruct((n, D_MAIN), BF16),
            jax.ShapeDtypeStruct((n, LANES), F32),
        ],
        scratch_shapes=[pltpu.VMEM((tm, D_MODEL), BF16)],
        compiler_params=_cparams(("parallel", "arbitrary")),
        name="ln_proj",
    )(x, g, b, w_main, w_gate)


def _gate_scan_kernel(gt_ref, b_ref, r_ref, cola_ref, colb_ref):
    z = gt_ref[...].T[:2 * M_HEADS] + b_ref[...]
    li = z[:M_HEADS]
    lf = jax.nn.log_sigmoid(z[M_HEADS:])
    pos = lax.broadcasted_iota(jnp.int32, lf.shape, 1) % CHUNK
    bcum = lf
    shift = 1
    while shift < CHUNK:
        bcum = bcum + jnp.where(pos >= shift, pltpu.roll(bcum, shift, axis=1), 0.0)
        shift *= 2
    r = li - bcum
    cm = r
    shift = 1
    while shift < CHUNK:
        cm = jnp.maximum(cm, jnp.where(pos >= shift, pltpu.roll(cm, shift, axis=1), -jnp.inf))
        shift *= 2
    r_ref[...] = r
    zero = jnp.zeros_like(r)
    cola_ref[...] = jnp.concatenate([zero, r, cm, cm] * 3 + [cm, zero, zero, zero], axis=0).T
    colb_ref[...] = jnp.concatenate([zero, zero, bcum, bcum] * 3 + [bcum, zero, zero, zero], axis=0).T


def _gate_scan(gt, bias, tl=2048):
    n = gt.shape[0]
    tl = min(tl, n)
    return pl.pallas_call(
        _gate_scan_kernel,
        grid=(n // tl,),
        in_specs=[pl.BlockSpec((tl, LANES), lambda i: (i, 0)),
                  pl.BlockSpec((2 * M_HEADS, 1), lambda i: (0, 0))],
        out_specs=[pl.BlockSpec((M_HEADS, tl), lambda i: (0, i)),
                   pl.BlockSpec((tl, LANES), lambda i: (i, 0)),
                   pl.BlockSpec((tl, LANES), lambda i: (i, 0))],
        out_shape=[jax.ShapeDtypeStruct((M_HEADS, n), F32),
                   jax.ShapeDtypeStruct((n, LANES), F32),
                   jax.ShapeDtypeStruct((n, LANES), F32)],
        compiler_params=_cparams(("parallel",)),
        name="gate_scan",
    )(gt, bias)


def _mlstm_kernel(q_ref, k_ref, v_ref, o_ref, row_ref, cola_ref, colb_ref, spread_ref, ng_ref,
                  y_ref, cn_ref, m_ref, *, ts):
    @pl.when(pl.program_id(1) == 0)
    def _():
        cn_ref[...] = jnp.zeros_like(cn_ref)
        m_ref[...] = jnp.zeros_like(m_ref)

    L = CHUNK
    dh = M_HEAD_DIM
    scale = dh ** -0.5
    ones_v = jnp.ones((M_HEADS, L, dh), BF16)
    lane_mean = jnp.full((M_HEADS, dh, dh), 1.0 / dh, BF16)
    li = lax.broadcasted_iota(jnp.int32, (L, L), 0)
    lj = lax.broadcasted_iota(jnp.int32, (L, L), 1)
    causal = lj <= li
    lane = lax.broadcasted_iota(jnp.int32, (L, LANES), 1)
    lane_grp = lane // _COL_GROUP
    lane_in = lane % _COL_GROUP
    is_wk = (lane_grp < 3) & (lane_in >= M_HEADS) & (lane_in < 2 * M_HEADS)
    is_a = (lane_grp < 3) & (lane_in >= 2 * M_HEADS) & (lane_in < 3 * M_HEADS)
    is_wi = (lane_grp < 3) & (lane_in >= 3 * M_HEADS)

    def chunk_body(c, carry):
        t0 = pl.multiple_of(c * L, L)
        rows = pl.ds(t0, L)
        rowg = row_ref[c]
        ca = cola_ref[rows, :]
        cb = colb_ref[rows, :]
        m_row = m_ref[...]
        a_l = jnp.maximum(m_row, ca)
        w_inter_l = jnp.exp(m_row - a_l)
        floor_l = jnp.exp(-(cb + a_l))
        a_last = a_l[L - 1:L, :]
        wk_l = jnp.exp(ca - pltpu.roll(a_last, LANES - M_HEADS, axis=1))
        decay_row = jnp.exp(m_row - a_last)
        m_ref[...] = cb[L - 1:L, :] + a_last
        heads = range(M_HEADS)
        per_head = lambda ref: jnp.stack([ref[rows, h * dh:(h + 1) * dh] for h in heads])
        bmm = lambda x, y, cx, cy: lax.dot_general(
            x, y, (((cx,), (cy,)), ((0,), (0,))), preferred_element_type=F32)
        x = jnp.where(is_wk, wk_l, jnp.where(is_a, a_l, jnp.where(is_wi, w_inter_l, 0.0)))
        x_hi = x.astype(BF16)
        x_r1 = x - x_hi.astype(F32)
        x_mid = x_r1.astype(BF16)
        x_lo = (x_r1 - x_mid.astype(F32)).astype(BF16)
        parts = jnp.where(lane_grp == 0, x_hi, jnp.where(lane_grp == 1, x_mid, x_lo))
        rep = jnp.dot(parts, spread_ref[...], preferred_element_type=F32)
        wide = lambda qi: jnp.stack([rep[:, (qi * M_HEADS + h) * dh:(qi * M_HEADS + h + 1) * dh]
                                     for h in heads])
        a = wide(0)[..., :L]
        w_inter = wide(1)
        wk_wide = wide(2)
        floor = jnp.stack([floor_l[:, 3 * 32 + h:3 * 32 + h + 1] for h in heads])
        decay = jnp.stack([decay_row[:, 2 * M_HEADS + h:2 * M_HEADS + h + 1] for h in heads])
        q = per_head(q_ref)
        k = per_head(k_ref)
        v = per_head(v_ref)
        w_intra = jnp.exp(jnp.where(causal[None], rowg[:, None, :] - a, -jnp.inf))
        s = bmm(q, k, 2, 2) * scale * w_intra
        s_hi = s.astype(BF16)
        s_lo = (s - s_hi.astype(F32)).astype(BF16)
        sv = bmm(s_hi, jnp.concatenate([v, ones_v], axis=-1), 2, 1)
        cn = cn_ref[...]
        qc = bmm(q, cn.astype(BF16), 2, 1)
        num = sv[..., :dh] + w_inter * qc[..., :dh]
        den = sv[..., dh:] + bmm(s_lo, ones_v, 2, 1) + w_inter * qc[..., dh:]
        hval = num / jnp.maximum(jnp.abs(den), floor)
        wkv =jnp.concatenate([wk_wide * v.astype(F32), wk_wide], axis=-1).astype(BF16)
        cn_ref[...] = decay * cn + scale * bmm(k, wkv, 1, 1)
        mu = bmm(hval.astype(BF16), lane_mean, 2, 1)
        hc = hval - mu
        var = bmm((hc * hc).astype(BF16), lane_mean, 2, 1)
        hn = hc * lax.rsqrt(var + LN_EPS)
        for h in heads:
            hs = slice(h * dh, (h + 1) * dh)
            og = jax.nn.sigmoid(o_ref[rows, hs].astype(F32))
            y_ref[rows, hs] = (og * hn[h] * ng_ref[:, hs]).astype(BF16)
        return carry

    lax.fori_loop(0, ts // L, chunk_body, 0, unroll=8)


_COL_GROUP = 32


def _spread_matrix():
    k = np.arange(LANES)
    grp, slot, head = k // _COL_GROUP, (k % _COL_GROUP) // M_HEADS, k % M_HEADS
    block = ((slot + 1) % 3) * M_HEADS + head
    valid = (grp < 3) & (slot >= 1)
    cols = np.arange(3 * M_HEADS * M_HEAD_DIM) // M_HEAD_DIM
    return jnp.asarray((block[:, None] == cols[None, :]) & valid[:, None], dtype=BF16)


def _mlstm(proj, rowg, cola, colb, norm_g, batch, seq, ts=1024):
    n = proj.shape[0]
    ts = min(ts, seq)
    nb = seq // ts
    blk = lambda col: pl.BlockSpec((ts, D_MLSTM), lambda b, i, col=col: (b * nb + i, col))
    return pl.pallas_call(
        functools.partial(_mlstm_kernel, ts=ts),
        grid=(batch, nb),
        in_specs=[
            blk(3), blk(4), blk(5), blk(6),
            pl.BlockSpec((ts // CHUNK, M_HEADS, CHUNK), lambda b, i: (b * nb + i, 0, 0)),
            pl.BlockSpec((ts, LANES), lambda b, i: (b * nb + i, 0)),
            pl.BlockSpec((ts, LANES), lambda b, i: (b * nb + i, 0)),
            pl.BlockSpec((LANES, 3 * M_HEADS * M_HEAD_DIM), lambda b, i: (0, 0),
                         pipeline_mode=pl.Buffered(1)),
            pl.BlockSpec((1, D_MLSTM), lambda b, i: (0, 0)),
        ],
        out_specs=pl.BlockSpec((ts, D_MLSTM), lambda b, i: (b * nb + i, 0)),
        out_shape=jax.ShapeDtypeStruct((n, D_MLSTM), BF16),
        scratch_shapes=[
            pltpu.VMEM((M_HEADS, M_HEAD_DIM, 2 * M_HEAD_DIM), F32),
            pltpu.VMEM((1, LANES), F32),
        ],
        compiler_params=_cparams(("parallel", "arbitrary")),
        name="mlstm",
    )(proj, proj, proj, proj, rowg, cola, colb, _spread_matrix(), norm_g)


_MIX_CHUNKS = 2


def _mix_kernel(cb_ref, cc_ref, ch_ref, pc_ref, ph_ref, ym_ref, x_ref, g0_ref, b0_ref, cw_ref,
                cbias_ref, wo_ref, g_ref, b_ref, h1_ref, h1b_ref, *, blocks_per_seq):
    tm = cb_ref.shape[0]
    rows_per_chunk = tm // _MIX_CHUNKS
    first = (pl.program_id(0) % blocks_per_seq) == 0
    halo = pc_ref[...].astype(F32) * ph_ref[...].astype(F32)
    hr = halo.shape[0]
    tail = jnp.where(first, 0.0, halo[hr - 2:hr, :])
    row = lax.broadcasted_iota(jnp.int32, (rows_per_chunk, D_CONV), 0)
    for c in range(_MIX_CHUNKS):
        rows = slice(c * rows_per_chunk, (c + 1) * rows_per_chunk)
        z = cc_ref[rows, :].astype(F32) * ch_ref[rows, :].astype(F32)
        z1 = jnp.where(row == 0, tail[1:2, :], pltpu.roll(z, 1, axis=0))
        z2 = pltpu.roll(z, 2, axis=0)
        z2 = jnp.where(row == 0, tail[0:1, :], z2)
        z2 = jnp.where(row == 1, tail[1:2, :], z2)
        conv = cbias_ref[...] + cw_ref[0:1, :] * z2 + cw_ref[1:2, :] * z1 + cw_ref[2:3, :] * z
        y_conv = (cb_ref[rows, :].astype(F32) * conv).astype(BF16)
        mix = (jnp.dot(y_conv, wo_ref[0:D_CONV, :], preferred_element_type=F32)
               + jnp.dot(ym_ref[rows, :], wo_ref[D_CONV:, :], preferred_element_type=F32))
        h = _layernorm(x_ref[rows, :], g0_ref[...], b0_ref[...])
        h1 = _layernorm(ALPHA * h + mix, g_ref[...], b_ref[...])
        h1_ref[rows, :] = h1
        h1b_ref[rows, :] = h1.astype(BF16)
        tail = z[rows_per_chunk - 2:rows_per_chunk, :]


def _mix(proj, ym, x, g0, b0, conv_w, conv_b, w_out, g, b, seq, tm=512):
    n = proj.shape[0]
    tm = min(tm, seq)
    halo = 16
    hb = tm // halo
    cur = lambda col: pl.BlockSpec((tm, D_CONV), lambda i, col=col: (i, col))
    prev = lambda col: pl.BlockSpec((halo, D_CONV),
                                    lambda i, col=col: (jnp.maximum(i * hb - 1, 0), col))
    const = lambda shape: pl.BlockSpec(shape, lambda i: (0, 0))
    return pl.pallas_call(
        functools.partial(_mix_kernel, blocks_per_seq=seq // tm),
        grid=(n // tm,),
        in_specs=[
            cur(0), cur(1), cur(2), prev(1), prev(2),
            pl.BlockSpec((tm, D_MLSTM), lambda i: (i, 0)),
            pl.BlockSpec((tm, D_MODEL), lambda i: (i, 0)),
            const((1, D_MODEL)), const((1, D_MODEL)),
            const((8, D_CONV)), const((1, D_CONV)),
            pl.BlockSpec((D_MODEL, D_MODEL), lambda i: (0, 0), pipeline_mode=pl.Buffered(1)),
            const((1, D_MODEL)), const((1, D_MODEL)),
        ],
        out_specs=[pl.BlockSpec((tm, D_MODEL), lambda i: (i, 0)),
                   pl.BlockSpec((tm, D_MODEL), lambda i: (i, 0))],
        out_shape=[jax.ShapeDtypeStruct((n, D_MODEL), F32),
                   jax.ShapeDtypeStruct((n, D_MODEL), BF16)],
        compiler_params=_cparams(("parallel",)),
        name="mix",
    )(proj, proj, proj, proj, proj, ym, x, g0, b0, conv_w, conv_b, w_out, g, b)


def _qproj_kernel(x_ref, wqt_ref, qt_ref):
    qt_ref[...] = lax.dot_general(wqt_ref[...], x_ref[...], (((1,), (1,)), ((), ())),
                                  preferred_element_type=F32).astype(BF16)


def _qproj(h1b, wqt, tm=1024):
    n = h1b.shape[0]
    tm = min(tm, n)
    dq = wqt.shape[0]
    return pl.pallas_call(
        _qproj_kernel,
        grid=(n // tm,),
        in_specs=[pl.BlockSpec((tm, D_MODEL), lambda i: (i, 0)),
                  pl.BlockSpec((dq, D_MODEL), lambda i: (0, 0), pipeline_mode=pl.Buffered(1))],
        out_specs=pl.BlockSpec((dq, tm), lambda i: (0, i)),
        out_shape=jax.ShapeDtypeStruct((dq, n), BF16),
        compiler_params=_cparams(("parallel",)),
        name="peer_q",
    )(h1b, wqt)


def _oddeven_sort_pairs(n):
    pairs = []
    p = 1
    while p < n:
        k = p
        while k >= 1:
            for j in range(k % p, n - k, 2 * k):
                for i in range(min(k, n - j - k)):
                    if (i + j) // (2 * p) == (i + j + k) // (2 * p):
                        pairs.append((i + j, i + j + k))
            k //= 2
        p *= 2
    return pairs


_SORT16 = _oddeven_sort_pairs(PEER_TOPK)
_BITONIC16 = [(i, i + d) for d in (8, 4, 2, 1) for i in range(PEER_TOPK) if not i & d]


def _compare_exchange(v, ix, pairs):
    for i, j in pairs:
        up = v[j] > v[i]
        v[i], v[j] = jnp.maximum(v[i], v[j]), jnp.minimum(v[i], v[j])
        ix[i], ix[j] = jnp.where(up, ix[j], ix[i]), jnp.where(up, ix[i], ix[j])


def _network_top16(cur_scr, srt_v, srt_i, sv_scr, si_scr, p, lanes):
    K = PEER_TOPK
    for g in range(N_KEYS // K):
        v = [cur_scr[g * K + i, :, lanes] for i in range(K)]
        ix = [jnp.int32(g * K + i) for i in range(K)]
        _compare_exchange(v, ix, _SORT16)
        for i in range(K):
            srt_v[g * K + i] = v[i]
            srt_i[g * K + i] = ix[i]
    dropped = jnp.full(v[0].shape, -jnp.inf, F32)
    span = K
    while span < N_KEYS:
        for a in range(0, N_KEYS, 2 * span):
            b = a + span
            va = [srt_v[a + i] for i in range(K)]
            ia = [srt_i[a + i] for i in range(K)]
            v, ix = [], []
            for i in range(K):
                vb = srt_v[b + K - 1 - i]
                up = vb > va[i]
                v.append(jnp.maximum(va[i], vb))
                ix.append(jnp.where(up, srt_i[b + K - 1 - i], ia[i]))
                dropped = jnp.maximum(dropped, jnp.minimum(va[i], vb))
            _compare_exchange(v, ix, _BITONIC16)
            for i in range(K):
                srt_v[a + i] = v[i]
                srt_i[a + i] = ix[i]
        span *= 2
    distinct = v[K - 1] > dropped
    for i in range(K - 1):
        distinct = distinct & (v[i] > v[i + 1])
    for i in range(K):
        sv_scr[p, i, :, lanes] = v[i]
        si_scr[p, i, :, lanes] = ix[i]
    return jnp.where(distinct, 0, 1)


def _retrieve_kernel(q_ref, kb_ref, eid_ref, gate_ref, cur_scr, sv_scr, si_scr, srt_v, srt_i):
    tb = q_ref.shape[1]
    dh = PEER_HEADS * PEER_HALF

    for p in range(2):
        s = jnp.dot(kb_ref[p], q_ref[p * dh:(p + 1) * dh, :],
                    preferred_element_type=F32)
        cur_scr[...] = s.reshape(N_KEYS, PEER_HEADS, tb)

        def lane_body(lh, tied, p=p):
            lanes = pl.ds(pl.multiple_of(lh * LANES, LANES), LANES)
            return jnp.maximum(tied, _network_top16(cur_scr, srt_v, srt_i, sv_scr, si_scr, p, lanes))

        tied = lax.fori_loop(0, tb // LANES, lane_body, jnp.zeros((PEER_HEADS, LANES), jnp.int32))

        def round_body(k, taken, p=p):
            vals, ids = [], list(range(N_KEYS))
            for n in range(N_KEYS):
                c = jnp.where(taken == n, -jnp.inf, cur_scr[n])
                cur_scr[n] = c
                vals.append(c)
            while len(vals) > 1:
                right = [vals[i + 1] > vals[i] for i in range(0, len(vals), 2)]
                ids = [jnp.where(g, ids[2 * i + 1], ids[2 * i]) for i, g in enumerate(right)]
                vals = [jnp.maximum(vals[2 * i], vals[2 * i + 1]) for i in range(len(right))]
            sv_scr[p, k] = vals[0]
            si_scr[p, k] = ids[0]
            return ids[0]

        @pl.when(jnp.max(tied) > 0)
        def _(round_body=round_body):
            lax.fori_loop(0, PEER_TOPK, round_body, jnp.full((PEER_HEADS, tb), N_KEYS, jnp.int32))

    sv1 = sv_scr[0]
    sv2 = sv_scr[1]
    e1 = si_scr[0] * N_KEYS
    e2 = si_scr[1]
    a_iota = lax.broadcasted_iota(jnp.int32, (PEER_TOPK, PEER_HEADS, tb), 0)
    front = sv1 + sv2[0][None]
    ptr = jnp.zeros_like(a_iota)
    tops, eids = [], []
    for k in range(PEER_TOPK):
        mx = jnp.max(front, axis=0)
        awin = jnp.min(jnp.where(front == mx[None], a_iota, PEER_TOPK), axis=0)
        hit = a_iota == awin[None]
        bwin = jnp.sum(jnp.where(hit, ptr, 0), axis=0)
        eid = (jnp.sum(jnp.where(hit, e1, 0), axis=0)
               + jnp.sum(jnp.where(a_iota == bwin[None], e2, 0), axis=0))
        nxt = jnp.max(jnp.where(a_iota == (bwin + 1)[None], sv2, -jnp.inf), axis=0)
        front = jnp.where(hit, sv1 + nxt[None], front)
        ptr = jnp.where(hit, ptr + 1, ptr)
        tops.append(mx)
        eids.append(eid)
    top_s = jnp.stack(tops)
    e = jnp.exp(top_s - tops[0][None])
    gate = e / jnp.sum(e, axis=0)[None]
    nj = PEER_TOPK * PEER_HEADS
    eid_ref[...] = jnp.stack(eids).reshape(nj, tb).T
    gate_ref[...] = gate.reshape(nj, tb).T


def _retrieve(q, kb, tb=256):
    n = q.shape[1]
    tb = min(tb, n)
    nj = PEER_HEADS * PEER_TOPK
    dq = 2 * PEER_HEADS * PEER_HALF
    return pl.pallas_call(
        _retrieve_kernel,
        grid=(n // tb,),
        in_specs=[
            pl.BlockSpec((dq, tb), lambda i: (0, i)),
            pl.BlockSpec((2, N_KEYS * PEER_HEADS, dq // 2), lambda i: (0, 0, 0),
                         pipeline_mode=pl.Buffered(1)),
        ],
        out_specs=[pl.BlockSpec((tb, nj), lambda i: (i, 0)),
                   pl.BlockSpec((tb, nj), lambda i: (i, 0))],
        out_shape=[jax.ShapeDtypeStruct((n, nj), jnp.int32),
                   jax.ShapeDtypeStruct((n, nj), F32)],
        scratch_shapes=[
            pltpu.VMEM((N_KEYS, PEER_HEADS, tb), F32),
            pltpu.VMEM((2, PEER_TOPK, PEER_HEADS, tb), F32),
            pltpu.VMEM((2, PEER_TOPK, PEER_HEADS, tb), jnp.int32),
            pltpu.VMEM((N_KEYS, PEER_HEADS, LANES), F32),
            pltpu.VMEM((N_KEYS, PEER_HEADS, LANES), jnp.int32),
        ],
        compiler_params=_cparams(("parallel",)),
        name="retrieve",
    )(q, kb)


def _block_diag_keys(keys):
    eye_h = jnp.eye(PEER_HEADS, dtype=keys.dtype)
    kb = jnp.einsum("hpnc,hg->pnhgc", keys, eye_h)
    return kb.reshape(2, N_KEYS * PEER_HEADS, PEER_HEADS * PEER_HALF)


def _half_major_query_weight(wq):
    d = wq.shape[0]
    return wq.reshape(d, PEER_HEADS, 2, PEER_HALF).transpose(2, 1, 3, 0).reshape(-1, d)


_W_GROUP = 16
_W_PITCH = _W_GROUP + 4


def _scatter_w_kernel(eid_ref, gate_ref, w_ref, scr_a, scr_b):
    tb = eid_ref.shape[0]
    nj = eid_ref.shape[1]
    ngrp = tb // _W_GROUP
    sub = lax.broadcasted_iota(jnp.int32, (N_KEYS, nj), 0)

    def build(grp, scr):
        base = pl.multiple_of(grp * _W_GROUP, _W_GROUP)
        for tl in range(_W_GROUP):
            e = eid_ref[pl.ds(base + tl, 1), :]
            g = gate_ref[pl.ds(base + tl, 1), :]
            p = jnp.where(sub == (e >> 7), 1.0, 0.0).astype(BF16)
            q = jnp.where(sub == (e & (N_KEYS - 1)), g, 0.0).astype(BF16)
            wt = lax.dot_general(p, q, (((1,), (1,)), ((), ())), preferred_element_type=F32)
            scr[pl.ds(tl, N_KEYS, stride=_W_PITCH), :] = wt

    def emit(grp, scr):
        base = pl.multiple_of(grp * _W_GROUP, _W_GROUP)
        for r in range(N_KEYS):
            tile = scr[r * _W_PITCH:r * _W_PITCH + _W_GROUP, :]
            w_ref[pl.ds(base, _W_GROUP), r * N_KEYS:(r + 1) * N_KEYS] = tile.astype(BF16)

    build(0, scr_a)

    def pair_body(i, carry):
        build(2 * i + 1, scr_b)
        emit(2 * i, scr_a)
        build(2 * i + 2, scr_a)
        emit(2 * i + 1, scr_b)
        return carry

    lax.fori_loop(0, ngrp // 2 - 1, pair_body, 0)
    build(ngrp - 1, scr_b)
    emit(ngrp - 2, scr_a)
    emit(ngrp - 1, scr_b)


def _retrieve_scatter_kernel(q_ref, kb_ref, w_ref, cur_scr, sv_scr, si_scr, srt_v, srt_i,
                             eid_scr, gate_scr, scr_a, scr_b):
    _retrieve_kernel(q_ref, kb_ref, eid_scr, gate_scr, cur_scr, sv_scr, si_scr, srt_v, srt_i)
    _scatter_w_kernel(eid_scr, gate_scr, w_ref, scr_a, scr_b)


def _retrieve_scatter(q, kb, tb=512):
    n = q.shape[1]
    tb = min(tb, n)
    nj = PEER_HEADS * PEER_TOPK
    dq = 2 * PEER_HEADS * PEER_HALF
    assert tb % (2 * _W_GROUP) == 0
    return pl.pallas_call(
        _retrieve_scatter_kernel,
        grid=(n // tb,),
        in_specs=[
            pl.BlockSpec((dq, tb), lambda i: (0, i)),
            pl.BlockSpec((2, N_KEYS * PEER_HEADS, dq // 2), lambda i: (0, 0, 0),
                         pipeline_mode=pl.Buffered(1)),
        ],
        out_specs=pl.BlockSpec((tb, N_EXPERTS), lambda i: (i, 0)),
        out_shape=jax.ShapeDtypeStruct((n, N_EXPERTS), BF16),
        scratch_shapes=[
            pltpu.VMEM((N_KEYS, PEER_HEADS, tb), F32),
            pltpu.VMEM((2, PEER_TOPK, PEER_HEADS, tb), F32),
            pltpu.VMEM((2, PEER_TOPK, PEER_HEADS, tb), jnp.int32),
            pltpu.VMEM((N_KEYS, PEER_HEADS, LANES), F32),
            pltpu.VMEM((N_KEYS, PEER_HEADS, LANES), jnp.int32),
            pltpu.VMEM((tb, nj), jnp.int32),
            pltpu.VMEM((tb, nj), F32),
            pltpu.VMEM((N_KEYS * _W_PITCH, N_KEYS), F32),
            pltpu.VMEM((N_KEYS * _W_PITCH, N_KEYS), F32),
        ],
        compiler_params=_cparams(("parallel",)),
        name="retrieve_scatter",
    )(q, kb)


def _scatter_w(eid, gate, tb=128):
    n, nj = eid.shape
    tb = min(tb, n)
    assert tb % (2 * _W_GROUP) == 0
    return pl.pallas_call(
        _scatter_w_kernel,
        grid=(n // tb,),
        in_specs=[pl.BlockSpec((tb, nj), lambda i: (i, 0)),
                  pl.BlockSpec((tb, nj), lambda i: (i, 0))],
        out_specs=pl.BlockSpec((tb, N_EXPERTS), lambda i: (i, 0)),
        out_shape=jax.ShapeDtypeStruct((n, N_EXPERTS), BF16),
        scratch_shapes=[pltpu.VMEM((N_KEYS * _W_PITCH, N_KEYS), F32)] * 2,
        compiler_params=_cparams(("parallel",)),
        name="scatter_w",
    )(eid, gate)


def _experts_kernel(xb_ref, u_ref, v_ref, w_ref, h1_ref, g_ref, b_ref, out_ref):
    k = pl.program_id(1)

    @pl.when(k == 0)
    def _():
        out_ref[...] = ALPHA * h1_ref[...]

    a = lax.dot_general(xb_ref[...], u_ref[...], (((1,), (1,)), ((), ())),
                        preferred_element_type=F32)
    act = 0.5 * a * (1.0 + lax.erf(a * (0.5 ** 0.5)))
    wa = (w_ref[...].astype(F32) * act).astype(BF16)
    out_ref[...] += jnp.dot(wa, v_ref[...], preferred_element_type=F32)

    @pl.when(k == pl.num_programs(1) - 1)
    def _():
        out_ref[...] = _layernorm(out_ref[...], g_ref[...], b_ref[...])


def _experts(h1b, u, v, w, h1, g, b, tm=1024, ec=1024):
    n = h1b.shape[0]
    tm = min(tm, n)
    once = pl.Buffered(1)
    return pl.pallas_call(
        _experts_kernel,
        grid=(n // tm, N_EXPERTS // ec),
        in_specs=[
            pl.BlockSpec((tm, D_MODEL), lambda i, k: (i, 0), pipeline_mode=once),
            pl.BlockSpec((ec, D_MODEL), lambda i, k: (k, 0)),
            pl.BlockSpec((ec, D_MODEL), lambda i, k: (k, 0)),
            pl.BlockSpec((tm, ec), lambda i, k: (i, k)),
            pl.BlockSpec((tm, D_MODEL), lambda i, k: (i, 0), pipeline_mode=once),
            pl.BlockSpec((1, D_MODEL), lambda i, k: (0, 0)),
            pl.BlockSpec((1, D_MODEL), lambda i, k: (0, 0)),
        ],
        out_specs=pl.BlockSpec((tm, D_MODEL), lambda i, k: (i, 0), pipeline_mode=once),
        out_shape=jax.ShapeDtypeStruct((n, D_MODEL), F32),
        compiler_params=_cparams(("parallel", "arbitrary")),
        name="experts",
    )(h1b, u, v, w, h1, g, b)


def kernel(x, ln_in_g, ln_in_b, w_in, b_gate, conv_w, conv_b, mh_norm_g, w_out, ln1_g, ln1_b,
           peer_wq, peer_keys, peer_u, peer_v, ln2_g, ln2_b):
    batch, seq, d = x.shape
    n = batch * seq
    row = lambda t: t.reshape(1, -1).astype(F32)
    l = 0
    w_main = w_in[l, :, :D_MAIN].astype(BF16)
    w_gate = jnp.pad(w_in[l, :, D_MAIN:], ((0, 0), (0, LANES - 2 * M_HEADS))).astype(BF16)

    x2 = x.reshape(n, d)
    proj, gates = _ln_proj(x2, row(ln_in_g), row(ln_in_b), w_main, w_gate, tm=_TILES["proj_tm"])

    r_rows, cola, colb = _gate_scan(gates, b_gate[l].reshape(2 * M_HEADS, 1), tl=_TILES["scan_tl"])
    rowg = r_rows.reshape(M_HEADS, n // CHUNK, CHUNK).transpose(1, 0, 2)
    ym = _mlstm(proj, rowg, cola, colb, row(mh_norm_g[l]), batch, seq, ts=_TILES["mlstm_ts"])

    cw = jnp.pad(conv_w[l], ((0, 8 - conv_w.shape[1]), (0, 0)))
    h1, h1b = _mix(proj, ym, x2, row(ln_in_g), row(ln_in_b), cw, row(conv_b[l]), w_out[l].astype(BF16),
                   row(ln1_g[l]), row(ln1_b[l]), seq, tm=_TILES["mix_tm"])

    kb = _block_diag_keys(peer_keys[l].astype(BF16))
    q = _qproj(h1b, _half_major_query_weight(peer_wq[l].astype(BF16)))
    w = _retrieve_scatter(q, kb, tb=_TILES["retrieve_tb"])
    out = _experts(h1b, peer_u[l].astype(BF16), peer_v[l].astype(BF16), w, h1,
                   row(ln2_g[l]), row(ln2_b[l]), tm=_TILES["experts_tm"], ec=_TILES["experts_ec"])
    return out.reshape(batch, seq, d)
```
